```python
import jax, jax.numpy as jnp
from jax import lax
import numpy as np

D_MODEL = 1024
BATCH = 8
SEQ = 2048
DEPTH = 4

GRID_W = 64
CTX_LEN = 256
N_EVEN = (DEPTH + 1) // 2
N_ODD = DEPTH // 2

MLA_HEADS = 8
MLA_Q_RANK = 256
MLA_KV_RANK = 128
MLA_NOPE = 64
MLA_ROPE = 32
MLA_V = 64
MLA_SCALE = (MLA_NOPE + MLA_ROPE) ** -0.5
Q_BLOCK = 128
ROPE_BASE = 10000.0
GLA_HEADS = 4
GLA_DK = 64
GLA_DV = 128
GLA_GATE_RANK = 16
GLA_GATE_NORM = 16.0
GLA_CHUNK = 64
CONV_CH = 768
CONV_W = 31
FNET_GROUPS = 4
FNET_GROUP_CH = 64
FNET_CH = FNET_GROUPS * FNET_GROUP_CH
D_FF = 2816
FFN_CONV_W = 3
EPS = 1e-6

EVEN_SIZES = [MLA_Q_RANK, MLA_KV_RANK, MLA_ROPE,
              GLA_HEADS * GLA_DK, GLA_HEADS * GLA_DK, GLA_HEADS * GLA_DV,
              GLA_GATE_RANK, GLA_GATE_RANK, GLA_HEADS * GLA_DV]
EVEN_IN = int(sum(EVEN_SIZES))
EVEN_OFFSETS = [int(v) for v in np.cumsum(EVEN_SIZES)[:-1]]
ODD_IN = 2 * CONV_CH + FNET_CH

kernel_name = "hybrid_mla_gla_conformer_fnet_dit"


def rms_norm(x, g):
    xf = x.astype(jnp.float32)
    y = xf * lax.rsqrt(jnp.mean(xf * xf, axis=-1, keepdims=True) + EPS)
    return (y * g.astype(jnp.float32)).astype(x.dtype)


def layer_norm(x, g, b):
    xf = x.astype(jnp.float32)
    mu = jnp.mean(xf, axis=-1, keepdims=True)
    xc = xf - mu
    y = xc * lax.rsqrt(jnp.mean(xc * xc, axis=-1, keepdims=True) + EPS)
    return (y * g.astype(jnp.float32) + b.astype(jnp.float32)).astype(x.dtype)


def dwconv(x, w, b):
    k, ch = w.shape
    pad = (k - 1) // 2
    y = lax.conv_general_dilated(x, w[:, None, :].astype(x.dtype), window_strides=(1,),
                                 padding=[(pad, pad)], dimension_numbers=("NWC", "WIO", "NWC"),
                                 feature_group_count=ch)
    return y + b.astype(x.dtype)


def to_heads(t, h):
    bsz, n, _ = t.shape
    return t.reshape(bsz, n, h, -1).transpose(0, 2, 1, 3)


def merge_heads(t):
    bsz, h, n, d = t.shape
    return t.transpose(0, 2, 1, 3).reshape(bsz, n, h * d)


def axial_rope_tables(n, dtype):
    rows = n // GRID_W
    row = jnp.repeat(jnp.arange(rows), GRID_W).astype(jnp.float32)
    col = jnp.tile(jnp.arange(GRID_W), rows).astype(jnp.float32)
    half = MLA_ROPE // 2
    inv = ROPE_BASE ** (-jnp.arange(0, half, 2, dtype=jnp.float32) / half)
    ar = row[:, None] * inv
    ac = col[:, None] * inv
    ang = jnp.concatenate([ar, ar, ac, ac], axis=-1)
    return jnp.cos(ang).astype(dtype), jnp.sin(ang).astype(dtype)


def rope_2d(x, cos, sin):
    x0, x1, x2, x3 = jnp.split(x, 4, axis=-1)
    rot = jnp.concatenate([-x1, x0, -x3, x2], axis=-1)
    return x * cos + rot * sin


def mla_queries(qc, q_norm, w_uq):
    bsz, n, _ = qc.shape
    q = (rms_norm(qc, q_norm) @ w_uq).reshape(bsz, n, MLA_HEADS, MLA_NOPE + MLA_ROPE).transpose(0, 2, 1, 3)
    return q[..., :MLA_NOPE], q[..., MLA_NOPE:]


def mla_keys(kvc, kv_norm, w_ukv):
    bsz, n, _ = kvc.shape
    kv = (rms_norm(kvc, kv_norm) @ w_ukv).reshape(bsz, n, MLA_HEADS, MLA_NOPE + MLA_V).transpose(0, 2, 1, 3)
    return kv[..., :MLA_NOPE], kv[..., MLA_NOPE:]


def mla_attend(qn, qr, kn, kr, v):
    s = jnp.einsum("bhqd,bhkd->bhqk", qn, kn) + jnp.einsum("bhqr,bkr->bhqk", qr, kr)
    p = jax.nn.softmax(s.astype(jnp.float32) * MLA_SCALE, axis=-1).astype(v.dtype)
    return jnp.einsum("bhqk,bhkd->bhqd", p, v)


def mla_attend_blocks(qn, qr, kn, kr, v):
    bsz, h, n, _ = qn.shape
    nb = n // Q_BLOCK
    qn_b = qn.reshape(bsz, h, nb, Q_BLOCK, -1).transpose(2, 0, 1, 3, 4)
    qr_b = qr.reshape(bsz, h, nb, Q_BLOCK, -1).transpose(2, 0, 1, 3, 4)
    o = lax.map(lambda a: mla_attend(a[0], a[1], kn, kr, v), (qn_b, qr_b))
    return o.transpose(1, 2, 0, 3, 4).reshape(bsz, h, n, -1)


def gla_chunk_scan(q, k, v, lg, s0):
    bsz, h, n, dk = q.shape
    dv = v.shape[-1]
    nc = n // GLA_CHUNK
    cs = GLA_CHUNK
    lower = jnp.tril(jnp.ones((cs, cs), dtype=bool))[:, :, None]

    def chunks(t):
        return t.reshape(bsz, h, nc, cs, t.shape[-1]).transpose(2, 0, 1, 3, 4)

    def step(state, inp):
        qc, kc, vc, gc = inp
        b = jnp.cumsum(gc, axis=2)
        o_inter = jnp.einsum("bhid,bhde->bhie", qc * jnp.exp(b), state)
        diff = b[:, :, :, None, :] - b[:, :, None, :, :]
        decay = jnp.where(lower, jnp.exp(jnp.where(lower, diff, 0.0)), 0.0)
        a = jnp.einsum("bhid,bhjd,bhijd->bhij", qc, kc, decay)
        o_intra = jnp.einsum("bhij,bhje->bhie", a, vc)
        b_end = b[:, :, -1:, :]
        state = jnp.exp(b_end[:, :, 0, :])[..., None] * state + jnp.einsum(
            "bhjd,bhje->bhde", kc * jnp.exp(b_end - b), vc)
        return state, o_inter + o_intra

    s_fin, o = lax.scan(step, s0, (chunks(q), chunks(k), chunks(v), chunks(lg)))
    return o.transpose(1, 2, 0, 3, 4).reshape(bsz, h, n, dv), s_fin


def _flip(t):
    return jnp.flip(t, axis=2)


def gla_features(gq, gk, gv, g_lr, w_g, b_g):
    f32 = jnp.float32
    q = to_heads(gq, GLA_HEADS).astype(f32) * (GLA_DK ** -0.5)
    k = to_heads(gk, GLA_HEADS).astype(f32)
    v = to_heads(gv, GLA_HEADS).astype(f32)
    lg = to_heads(jax.nn.log_sigmoid((g_lr @ w_g + b_g).astype(f32)) / GLA_GATE_NORM, GLA_HEADS)
    return q, k, v, lg


def gla_output(o, gr, o_norm):
    bsz, h, n, dv = o.shape
    o = o.transpose(0, 2, 1, 3).astype(gr.dtype)
    y = rms_norm(o, o_norm) * jax.nn.silu(gr.reshape(bsz, n, h, dv))
    return y.reshape(bsz, n, h * dv)


def even_mixer(h_lat, h_ctx, cos, sin, need_ctx, w_in, q_norm, kv_norm, w_uq, w_ukv,
               w_gfw, b_gfw, w_gbw, b_gbw, o_norm, w_out):
    pl = jnp.split(h_lat @ w_in, EVEN_OFFSETS, axis=-1)
    pc = jnp.split(h_ctx @ w_in, EVEN_OFFSETS, axis=-1)
    qn_l, qr_l = mla_queries(pl[0], q_norm, w_uq)
    qr_l = rope_2d(qr_l, cos, sin)
    kn_l, v_l = mla_keys(pl[1], kv_norm, w_ukv)
    kr_l = rope_2d(pl[2], cos, sin)
    kn_c, v_c = mla_keys(pc[1], kv_norm, w_ukv)
    kr_c = pc[2]
    kn_all = jnp.concatenate([kn_c, kn_l], axis=2)
    kr_all = jnp.concatenate([kr_c, kr_l], axis=1)
    v_all = jnp.concatenate([v_c, v_l], axis=2)
    a_lat = merge_heads(mla_attend_blocks(qn_l, qr_l, kn_all, kr_all, v_all))
    q_l, k_l, vv_l, lf_l = gla_features(pl[3], pl[4], pl[5], pl[6], w_gfw, b_gfw)
    lb_l = to_heads(jax.nn.log_sigmoid((pl[7] @ w_gbw + b_gbw).astype(jnp.float32)) / GLA_GATE_NORM, GLA_HEADS)
    q_c, k_c, vv_c, lf_c = gla_features(pc[3], pc[4], pc[5], pc[6], w_gfw, b_gfw)
    lb_c = to_heads(jax.nn.log_sigmoid((pc[7] @ w_gbw + b_gbw).astype(jnp.float32)) / GLA_GATE_NORM, GLA_HEADS)
    zeros = jnp.zeros((h_lat.shape[0], GLA_HEADS, GLA_DK, GLA_DV), jnp.float32)
    o_cf, s_cf = gla_chunk_scan(q_c, k_c, vv_c, lf_c, zeros)
    o_lf, _ = gla_chunk_scan(q_l, k_l, vv_l, lf_l, s_cf)
    o_cb, s_cb = gla_chunk_scan(_flip(q_c), _flip(k_c), _flip(vv_c), _flip(lb_c), zeros)
    o_lb, _ = gla_chunk_scan(_flip(q_l), _flip(k_l), _flip(vv_l), _flip(lb_l), s_cb)
    g_lat = gla_output(o_lf + _flip(o_lb), pl[8], o_norm)
    y_lat = jnp.concatenate([a_lat, g_lat], axis=-1) @ w_out
    y_ctx = None
    if need_ctx:
        qn_c, qr_c = mla_queries(pc[0], q_norm, w_uq)
        a_ctx = merge_heads(mla_attend(qn_c, qr_c, kn_c, kr_c, v_c))
        g_ctx = gla_output(o_cf + _flip(o_cb), pc[8], o_norm)
        y_ctx = jnp.concatenate([a_ctx, g_ctx], axis=-1) @ w_out
    return y_lat, y_ctx


def odd_mixer(h, w_in, dw_w, dw_b, ln_g, ln_b, w_out):
    a, gt, f = jnp.split(h @ w_in, [CONV_CH, 2 * CONV_CH], axis=-1)
    u = a * jax.nn.sigmoid(gt)
    u = jax.nn.silu(layer_norm(dwconv(u, dw_w, dw_b), ln_g, ln_b))
    bsz, n, _ = f.shape
    fg = f.reshape(bsz, n, FNET_GROUPS, FNET_GROUP_CH).astype(jnp.float32)
    fm = jnp.fft.fftn(fg, axes=(1, 3), norm="ortho").real.reshape(bsz, n, FNET_CH).astype(h.dtype)
    return jnp.concatenate([u, fm], axis=-1) @ w_out


def conv_ffn(h, w_in, dw_w, dw_b, w_out):
    g, v = jnp.split(h @ w_in, 2, axis=-1)
    return (jax.nn.gelu(dwconv(g, dw_w, dw_b), approximate=False) * v) @ w_out


def setup_inputs(seed: int = 0) -> dict:
    key = jax.random.key(seed)
    it = iter(list(jax.random.split(key, 40)))
    D = D_MODEL

    def nrm(shape, scale):
        return jax.random.normal(next(it), shape, jnp.float32) * scale

    def gain(shape):
        return 1.0 + nrm(shape, 0.02)

    return {
        "x": nrm((BATCH, SEQ, D), 1.0),
        "c": nrm((BATCH, D), 1.0),
        "ctx": nrm((BATCH, CTX_LEN, D), 1.0),
        "c_ctx": nrm((D,), 1.0),
        "mod_w": nrm((DEPTH, D, 6 * D), D ** -0.5),
        "mod_b": nrm((DEPTH, 6 * D), 0.02),
        "pre_mix_g": gain((DEPTH, D)),
        "post_mix_g": gain((DEPTH, D)),
        "pre_ffn_g": gain((DEPTH, D)),
        "post_ffn_g": gain((DEPTH, D)),
        "ev_in_w": nrm((N_EVEN, D, EVEN_IN), D ** -0.5),
        "mla_q_norm": gain((N_EVEN, MLA_Q_RANK)),
        "mla_kv_norm": gain((N_EVEN, MLA_KV_RANK)),
        "mla_w_uq": nrm((N_EVEN, MLA_Q_RANK, MLA_HEADS * (MLA_NOPE + MLA_ROPE)), MLA_Q_RANK ** -0.5),
        "mla_w_ukv": nrm((N_EVEN, MLA_KV_RANK, MLA_HEADS * (MLA_NOPE + MLA_V)), MLA_KV_RANK ** -0.5),
        "gla_w_gate_fw": nrm((N_EVEN, GLA_GATE_RANK, GLA_HEADS * GLA_DK), GLA_GATE_RANK ** -0.5),
        "gla_b_gate_fw": nrm((N_EVEN, GLA_HEADS * GLA_DK), 0.5),
        "gla_w_gate_bw": nrm((N_EVEN, GLA_GATE_RANK, GLA_HEADS * GLA_DK), GLA_GATE_RANK ** -0.5),
        "gla_b_gate_bw": nrm((N_EVEN, GLA_HEADS * GLA_DK), 0.5),
        "gla_o_norm": gain((N_EVEN, GLA_DV)),
        "ev_out_w": nrm((N_EVEN, D, D), D ** -0.5),
        "od_in_w": nrm((N_ODD, D, ODD_IN), D ** -0.5),
        "conf_dw_w": nrm((N_ODD, CONV_W, CONV_CH), CONV_W ** -0.5),
        "conf_dw_b": nrm((N_ODD, CONV_CH), 0.02),
        "conf_ln_g": gain((N_ODD, CONV_CH)),
        "conf_ln_b": nrm((N_ODD, CONV_CH), 0.02),
        "od_out_w": nrm((N_ODD, D, D), D ** -0.5),
        "ffn_in_w": nrm((DEPTH, D, 2 * D_FF), D ** -0.5),
        "ffn_dw_w": nrm((DEPTH, FFN_CONV_W, D_FF), FFN_CONV_W ** -0.5),
        "ffn_dw_b": nrm((DEPTH, D_FF), 0.02),
        "ffn_out_w": nrm((DEPTH, D_FF, D), D_FF ** -0.5),
    }


def reference(x, c, ctx, c_ctx, mod_w, mod_b, pre_mix_g, post_mix_g, pre_ffn_g, post_ffn_g,
              ev_in_w, mla_q_norm, mla_kv_norm, mla_w_uq, mla_w_ukv,
              gla_w_gate_fw, gla_b_gate_fw, gla_w_gate_bw, gla_b_gate_bw, gla_o_norm, ev_out_w,
              od_in_w, conf_dw_w, conf_dw_b, conf_ln_g, conf_ln_b, od_out_w,
              ffn_in_w, ffn_dw_w, ffn_dw_b, ffn_out_w):
    n = x.shape[1]
    cos, sin = axial_rope_tables(n, x.dtype)
    last_ctx_reader = ((DEPTH - 1) // 2) * 2
    for l in range(DEPTH):
        need_ctx = l < last_ctx_reader
        use_ctx = need_ctx or (l % 2 == 0)
        i = l // 2
        m_lat = (jax.nn.silu(c) @ mod_w[l] + mod_b[l])[:, None, :]
        sh1, sc1, g1, sh2, sc2, g2 = jnp.split(m_lat, 6, axis=-1)
        h_lat = rms_norm(x, pre_mix_g[l]) * (1 + sc1) + sh1
        h_ctx = None
        if use_ctx:
            m_ctx = jax.nn.silu(c_ctx) @ mod_w[l] + mod_b[l]
            csh1, csc1, cg1, csh2, csc2, cg2 = jnp.split(m_ctx, 6, axis=-1)
            h_ctx = rms_norm(ctx, pre_mix_g[l]) * (1 + csc1) + csh1
        if l % 2 == 0:
            y_lat, y_ctx = even_mixer(h_lat, h_ctx, cos, sin, need_ctx, ev_in_w[i], mla_q_norm[i],
                                      mla_kv_norm[i], mla_w_uq[i], mla_w_ukv[i],
                                      gla_w_gate_fw[i], gla_b_gate_fw[i], gla_w_gate_bw[i],
                                      gla_b_gate_bw[i], gla_o_norm[i], ev_out_w[i])
        else:
            y_lat = odd_mixer(h_lat, od_in_w[i], conf_dw_w[i], conf_dw_b[i], conf_ln_g[i],
                              conf_ln_b[i], od_out_w[i])
            y_ctx = None
            if need_ctx:
                y_ctx = odd_mixer(h_ctx, od_in_w[i], conf_dw_w[i], conf_dw_b[i], conf_ln_g[i],
                                  conf_ln_b[i], od_out_w[i])
        x = x + g1 * rms_norm(y_lat, post_mix_g[l])
        f_lat = conv_ffn(rms_norm(x, pre_ffn_g[l]) * (1 + sc2) + sh2,
                         ffn_in_w[l], ffn_dw_w[l], ffn_dw_b[l], ffn_out_w[l])
        x = x + g2 * rms_norm(f_lat, post_ffn_g[l])
        if need_ctx:
            ctx = ctx + cg1 * rms_norm(y_ctx, post_mix_g[l])
            f_ctx = conv_ffn(rms_norm(ctx, pre_ffn_g[l]) * (1 + csc2) + csh2,
                             ffn_in_w[l], ffn_dw_w[l], ffn_dw_b[l], ffn_out_w[l])
            ctx = ctx + cg2 * rms_norm(f_ctx, post_ffn_g[l])
    return x
```

```python
import functools
import math

import numpy as np
import jax
import jax.numpy as jnp
from jax import lax
from jax.experimental import pallas as pl
from jax.experimental.pallas import tpu as pltpu

F32 = jnp.float32
BF16 = jnp.bfloat16

GRID_W = 64
ROPE_BASE = 10000.0
MLA_HEADS = 8
MLA_Q_RANK = 256
MLA_KV_RANK = 128
MLA_NOPE = 64
MLA_ROPE = 32
MLA_V = 64
MLA_SCALE = (MLA_NOPE + MLA_ROPE) ** -0.5
GLA_HEADS = 4
GLA_DK = 64
GLA_DV = 128
GLA_GATE_RANK = 16
GLA_GATE_NORM = 16.0
GLA_CHUNK = 64
CONV_CH = 768
CONV_W = 31
FNET_GROUPS = 4
FNET_GROUP_CH = 64
FNET_CH = FNET_GROUPS * FNET_GROUP_CH
FFN_CHUNK = 256
EPS = 1e-6

P_QC, P_GQ, P_GK = 0, 256, 512
P_KVC, P_TAIL = 768, 896
P_GV, P_GR = 1024, 1536
P_WIDTH = 2048
T_KR, T_KRROT, T_GF, T_GB = 0, 32, 64, 80

NT_DIMS = (((1,), (1,)), ((), ()))
TN_DIMS = (((0,), (0,)), ((), ()))

VMEM_LIMIT = 56 * 1024 * 1024


def _params(*sem):
    return pltpu.CompilerParams(dimension_semantics=sem, vmem_limit_bytes=VMEM_LIMIT)


def _rms(xf, gain):
    return xf * lax.rsqrt(jnp.mean(xf * xf, axis=-1, keepdims=True) + EPS) * gain


def _sigmoid(x):
    return 1.0 / (1.0 + jnp.exp(-x))


def _dot(a, b):
    return jnp.dot(a, b, preferred_element_type=F32)


def _tile(n, want):
    t = min(n, want)
    assert n % t == 0, (n, want)
    return t


def _mod_kernel(c_ref, w_ref, b_ref, o_ref):
    c = c_ref[...]
    s = c * _sigmoid(c)
    o_ref[0] = _dot(s.astype(BF16), w_ref[0].astype(BF16)) + b_ref[0]


def _modulation(cvec, mod_w, mod_b):
    depth, d, n6 = mod_w.shape
    rows = cvec.shape[0]
    tn = _tile(n6, 1536)
    return pl.pallas_call(
        _mod_kernel,
        grid=(depth, n6 // tn),
        in_specs=[
            pl.BlockSpec((rows, d), lambda l, j: (0, 0)),
            pl.BlockSpec((1, d, tn), lambda l, j: (l, 0, j)),
            pl.BlockSpec((1, 1, tn), lambda l, j: (l, 0, j)),
        ],
        out_specs=pl.BlockSpec((1, rows, tn), lambda l, j: (l, 0, j)),
        out_shape=jax.ShapeDtypeStruct((depth, rows, n6), F32),
        compiler_params=_params("parallel", "parallel"),
        name="modulation",
    )(cvec, mod_w, mod_b.reshape(depth, 1, n6))


def _nmm_kernel(*refs, modulate, col_w):
    if modulate:
        x_ref, g_ref, sc_ref, sh_ref, w_ref, o_ref = refs
    else:
        x_ref, g_ref, w_ref, o_ref = refs
    h = _rms(x_ref[0].astype(F32), g_ref[...])
    if modulate:
        h = h * (1.0 + sc_ref[0]) + sh_ref[0]
    hb = h.astype(BF16)
    for j in range(o_ref.shape[2] // col_w):
        cols = slice(j * col_w, (j + 1) * col_w)
        o_ref[0, :, cols] = _dot(hb, w_ref[:, cols]).astype(o_ref.dtype)


def _nmm(x, col_off, gain, w, sc=None, sh=None, tm=512, name="nmm"):
    bsz, n, _ = x.shape
    k, nout = w.shape
    assert col_off % k == 0
    cb = col_off // k
    tm = _tile(n, tm)
    col_w = next(cw for cw in (512, 256, 128) if nout % cw == 0)
    modulate = sc is not None
    in_specs = [pl.BlockSpec((1, tm, k), lambda b, i: (b, i, cb)),
                pl.BlockSpec((1, k), lambda b, i: (0, 0))]
    args = [x, gain.reshape(1, k)]
    if modulate:
        per_batch = sc.shape[0] > 1
        mod_spec = pl.BlockSpec((1, 1, k), (lambda b, i: (b, 0, 0)) if per_batch else (lambda b, i: (0, 0, 0)))
        in_specs += [mod_spec, mod_spec]
        args += [sc, sh]
    in_specs.append(pl.BlockSpec((k, nout), lambda b, i: (0, 0)))
    args.append(w)
    return pl.pallas_call(
        functools.partial(_nmm_kernel, modulate=modulate, col_w=col_w),
        grid=(bsz, n // tm),
        in_specs=in_specs,
        out_specs=pl.BlockSpec((1, tm, nout), lambda b, i: (b, i, 0)),
        out_shape=jax.ShapeDtypeStruct((bsz, n, nout), BF16),
        compiler_params=_params("parallel", "parallel"),
        name=name,
    )(*args)


def _mnr_kernel(a1_ref, a2_ref, w1_ref, w2_ref, x_ref, gate_ref, pg_ref, o_ref):
    y = _dot(a1_ref[0], w1_ref[...]) + _dot(a2_ref[0], w2_ref[...])
    o_ref[0] = x_ref[0] + gate_ref[0] * _rms(y, pg_ref[...])


def _mnr(a1, a2, w1, w2, x, gate, post_g, tm=512, name="mnr"):
    bsz, n, d = x.shape
    k1, k2 = a1.shape[2], a2.shape[2]
    tm = _tile(n, tm)
    per_batch = gate.shape[0] > 1
    return pl.pallas_call(
        _mnr_kernel,
        grid=(bsz, n // tm),
        in_specs=[
            pl.BlockSpec((1, tm, k1), lambda b, i: (b, i, 0)),
            pl.BlockSpec((1, tm, k2), lambda b, i: (b, i, 0)),
            pl.BlockSpec((k1, d), lambda b, i: (0, 0)),
            pl.BlockSpec((k2, d), lambda b, i: (0, 0)),
            pl.BlockSpec((1, tm, d), lambda b, i: (b, i, 0)),
            pl.BlockSpec((1, 1, d), (lambda b, i: (b, 0, 0)) if per_batch else (lambda b, i: (0, 0, 0))),
            pl.BlockSpec((1, d), lambda b, i: (0, 0)),
        ],
        out_specs=pl.BlockSpec((1, tm, d), lambda b, i: (b, i, 0)),
        out_shape=jax.ShapeDtypeStruct((bsz, n, d), F32),
        compiler_params=_params("parallel", "parallel"),
        name=name,
    )(a1, a2, w1, w2, x, gate, post_g.reshape(1, d))


def _ffn_kernel(x_ref, xp_ref, xn_ref, g_ref, sc_ref, sh_ref, gate_ref, pg_ref,
                wg_ref, wv_ref, cw_ref, wo_ref, o_ref, xh_ref, acc_ref, *, seq_len):
    tm = x_ref.shape[1]
    n_chunks = wg_ref.shape[0]
    i = pl.program_id(1)
    gain = g_ref[...]
    scale = 1.0 + sc_ref[0]
    shift = sh_ref[0]

    def modulated(rows):
        return (_rms(rows, gain) * scale + shift).astype(BF16)

    x = x_ref[0]
    xh_ref[0:tm] = modulated(x)
    xh_ref[tm:tm + 16] = modulated(jnp.concatenate([xp_ref[0], xn_ref[0]], axis=0))

    local = lax.broadcasted_iota(jnp.int32, (tm, 1), 0)
    pos = (i * tm + local) % seq_len
    seq_first = pos == 0
    seq_last = pos == seq_len - 1
    tile_first = local == 0
    tile_last = local == tm - 1
    acc_ref[...] = jnp.zeros_like(acc_ref)

    def body(c, carry):
        xh = xh_ref[...]
        g = _dot(xh, wg_ref[c])
        v = _dot(xh[0:tm], wv_ref[c])
        gm = g[0:tm]
        g_prev = jnp.where(tile_first, g[tm + 7:tm + 8], pltpu.roll(gm, 1, axis=0))
        g_prev = jnp.where(seq_first, 0.0, g_prev)
        g_next = jnp.where(tile_last, g[tm + 8:tm + 9], pltpu.roll(gm, tm - 1, axis=0))
        g_next = jnp.where(seq_last, 0.0, g_next)
        cw = cw_ref[c]
        y = g_prev * cw[0:1] + gm * cw[1:2] + g_next * cw[2:3] + cw[3:4]
        u = 0.5 * y * (1.0 + lax.erf(y * (1.0 / math.sqrt(2.0)))) * v
        acc_ref[...] += _dot(u.astype(BF16), wo_ref[c])
        return carry

    lax.fori_loop(0, n_chunks, body, 0)
    o_ref[0] = x + gate_ref[0] * _rms(acc_ref[...], pg_ref[...])


def _ffn(x, pre_g, sc, sh, gate, post_g, wg, wv, cwb, wo, seq_len, tm=512, name="ffn"):
    bsz, n, d = x.shape
    nf, _, fc = wg.shape
    tm = _tile(n, tm)
    nb8 = n // 8
    per_batch = sc.shape[0] > 1
    mod_spec = pl.BlockSpec((1, 1, d), (lambda b, i: (b, 0, 0)) if per_batch else (lambda b, i: (0, 0, 0)))
    vec_spec = pl.BlockSpec((1, d), lambda b, i: (0, 0))
    return pl.pallas_call(
        functools.partial(_ffn_kernel, seq_len=seq_len),
        grid=(bsz, n // tm),
        in_specs=[
            pl.BlockSpec((1, tm, d), lambda b, i: (b, i, 0)),
            pl.BlockSpec((1, 8, d), lambda b, i: (b, jnp.maximum(i * (tm // 8) - 1, 0), 0)),
            pl.BlockSpec((1, 8, d), lambda b, i: (b, jnp.minimum((i + 1) * (tm // 8), nb8 - 1), 0)),
            vec_spec, mod_spec, mod_spec, mod_spec, vec_spec,
            pl.BlockSpec((nf, d, fc), lambda b, i: (0, 0, 0)),
            pl.BlockSpec((nf, d, fc), lambda b, i: (0, 0, 0)),
            pl.BlockSpec((nf, 8, fc), lambda b, i: (0, 0, 0)),
            pl.BlockSpec((nf, fc, d), lambda b, i: (0, 0, 0)),
        ],
        out_specs=pl.BlockSpec((1, tm, d), lambda b, i: (b, i, 0)),
        out_shape=jax.ShapeDtypeStruct((bsz, n, d), F32),
        scratch_shapes=[pltpu.VMEM((tm + 16, d), BF16), pltpu.VMEM((tm, d), F32)],
        compiler_params=_params("parallel", "parallel"),
        name=name,
    )(x, x, x, pre_g.reshape(1, d), sc, sh, gate, post_g.reshape(1, d), wg, wv, cwb, wo)


def _mla_kernel(*refs, n_ctx, n_lat):
    if n_lat:
        q_ref, cq_ref, sq_ref, kvc_ref, tc_ref, kvl_ref, tl_ref, ck_ref, sk_ref, o_ref, kc_ref, vc_ref = refs
    else:
        q_ref, cq_ref, sq_ref, kvc_ref, tc_ref, o_ref, kc_ref, vc_ref = refs
    hv = MLA_HEADS * MLA_NOPE

    @pl.when(pl.program_id(1) == 0)
    def _build_keys():
        def fill(kv_ref, t_ref, off, n, roped):
            tail = t_ref[0].astype(F32)
            kr = tail[:, T_KR:T_KR + MLA_ROPE]
            if roped:
                kr = kr * ck_ref[...] + tail[:, T_KRROT:T_KRROT + MLA_ROPE] * sk_ref[...]
            kr = kr.astype(BF16)
            for h in range(MLA_HEADS):
                kc_ref[h, off:off + n, 0:MLA_NOPE] = kv_ref[0, :, h * MLA_NOPE:(h + 1) * MLA_NOPE]
                kc_ref[h, off:off + n, MLA_NOPE:MLA_NOPE + MLA_ROPE] = kr
                kc_ref[h, off:off + n, MLA_NOPE + MLA_ROPE:128] = jnp.zeros((n, 128 - MLA_NOPE - MLA_ROPE), BF16)
                vc_ref[h, off:off + n, 0:MLA_V] = kv_ref[0, :, hv + h * MLA_V:hv + (h + 1) * MLA_V]
                vc_ref[h, off:off + n, MLA_V:128] = jnp.ones((n, 128 - MLA_V), BF16)

        fill(kvc_ref, tc_ref, 0, n_ctx, False)
        if n_lat:
            fill(kvl_ref, tl_ref, n_ctx, n_lat, True)

    tq = q_ref.shape[1]
    hr = MLA_HEADS * MLA_ROPE
    qr = (q_ref[0, :, hv:hv + hr].astype(F32) * cq_ref[...]
          + q_ref[0, :, hv + hr:hv + 2 * hr].astype(F32) * sq_ref[...]).astype(BF16)
    pad = jnp.zeros((tq, 128 - MLA_NOPE - MLA_ROPE), BF16)
    for h in range(MLA_HEADS):
        qh = jnp.concatenate([q_ref[0, :, h * MLA_NOPE:(h + 1) * MLA_NOPE],
                              qr[:, h * MLA_ROPE:(h + 1) * MLA_ROPE], pad], axis=1)
        s = lax.dot_general(qh, kc_ref[h], NT_DIMS, preferred_element_type=F32)
        p = jnp.exp(s - jnp.max(s, axis=-1, keepdims=True)).astype(BF16)
        oe = _dot(p, vc_ref[h])
        o_ref[0, :, h * MLA_V:(h + 1) * MLA_V] = (oe[:, 0:MLA_V] / oe[:, MLA_V:MLA_V + 1]).astype(o_ref.dtype)


def _mla(q, cosq, sinq, kv_c, p_c, kv_l=None, p_l=None, cosk=None, sink=None, tq=256, name="mla"):
    bsz, nq, qw = q.shape
    n_ctx = kv_c.shape[1]
    n_lat = 0 if kv_l is None else kv_l.shape[1]
    nk = n_ctx + n_lat
    tq = _tile(nq, tq)
    hr = MLA_HEADS * MLA_ROPE
    tail_blk = P_TAIL // 128
    in_specs = [
        pl.BlockSpec((1, tq, qw), lambda b, i: (b, i, 0)),
        pl.BlockSpec((tq, hr), lambda b, i: (i, 0)),
        pl.BlockSpec((tq, hr), lambda b, i: (i, 0)),
        pl.BlockSpec((1, n_ctx, kv_c.shape[2]), lambda b, i: (b, 0, 0)),
        pl.BlockSpec((1, n_ctx, 128), lambda b, i: (b, 0, tail_blk)),
    ]
    args = [q, cosq, sinq, kv_c, p_c]
    if n_lat:
        in_specs += [
            pl.BlockSpec((1, n_lat, kv_l.shape[2]), lambda b, i: (b, 0, 0)),
            pl.BlockSpec((1, n_lat, 128), lambda b, i: (b, 0, tail_blk)),
            pl.BlockSpec((n_lat, MLA_ROPE), lambda b, i: (0, 0)),
            pl.BlockSpec((n_lat, MLA_ROPE), lambda b, i: (0, 0)),
        ]
        args += [kv_l, p_l, cosk, sink]
    return pl.pallas_call(
        functools.partial(_mla_kernel, n_ctx=n_ctx, n_lat=n_lat),
        grid=(bsz, nq // tq),
        in_specs=in_specs,
        out_specs=pl.BlockSpec((1, tq, MLA_HEADS * MLA_V), lambda b, i: (b, i, 0)),
        out_shape=jax.ShapeDtypeStruct((bsz, nq, MLA_HEADS * MLA_V), BF16),
        scratch_shapes=[pltpu.VMEM((MLA_HEADS, nk, 128), BF16), pltpu.VMEM((MLA_HEADS, nk, 128), BF16)],
        compiler_params=_params("parallel", "arbitrary"),
        name=name,
    )(*args)


def _gla_kernel(*refs, want_ctx):
    (qc_ref, kc_ref, tc_ref, vc_ref, grc_ref, ql_ref, kl_ref, tl_ref, vl_ref, grl_ref,
     wg_ref, bg_ref, on_ref) = refs[:13]
    if want_ctx:
        oc_ref, ol_ref = refs[13:15]
        scratch = refs[15:]
    else:
        oc_ref, ol_ref = None, refs[13]
        scratch = refs[14:]
    lgc_ref, lgl_ref, accc_ref, accl_ref, st_ref = scratch
    cs = GLA_CHUNK
    hk = GLA_HEADS * GLA_DK

    for t_ref, lg_ref in ((tc_ref, lgc_ref), (tl_ref, lgl_ref)):
        z = _dot(t_ref[0], wg_ref[...]) + bg_ref[...]
        lg_ref[...] = (jnp.minimum(z, 0.0) - jnp.log(1.0 + jnp.exp(-jnp.abs(z)))) * (1.0 / GLA_GATE_NORM)

    row = lax.broadcasted_iota(jnp.int32, (cs, cs), 0)
    col = lax.broadcasted_iota(jnp.int32, (cs, cs), 1)
    masks = (row >= col, row <= col)
    tris = tuple(m.astype(F32) for m in masks)
    st_ref[...] = jnp.zeros_like(st_ref)

    def chunk(q_ref, k_ref, v_ref, lg_ref, acc_ref, d, r0):
        rows = pl.ds(pl.multiple_of(r0, cs), cs)
        g = lg_ref[rows, d * hk:(d + 1) * hk]
        cum = jnp.dot(tris[d], g, precision=lax.Precision.HIGHEST, preferred_element_type=F32)
        tot = cum[cs - 1:cs] if d == 0 else cum[0:1]
        ref = 0.5 * tot
        q = q_ref[0, rows, :].astype(F32)
        k = k_ref[0, rows, :].astype(F32)
        v = v_ref[0, rows, :]
        qt = (q * jnp.exp(cum - ref)).astype(BF16)
        kt = (k * jnp.exp(ref - cum)).astype(BF16)
        qs = (q * jnp.exp(cum)).astype(BF16)
        kd = (k * jnp.exp(tot - cum)).astype(BF16)
        dec = jnp.exp(tot)
        for h in range(GLA_HEADS):
            ks = slice(h * GLA_DK, (h + 1) * GLA_DK)
            vs = slice(h * GLA_DV, (h + 1) * GLA_DV)
            a = lax.dot_general(qt[:, ks], kt[:, ks], NT_DIMS, preferred_element_type=F32)
            a = jnp.where(masks[d], a, 0.0).astype(BF16)
            st = st_ref[d, h]
            acc_ref[d, rows, vs] = _dot(a, v[:, vs]) + lax.dot_general(
                qs[:, ks], st.astype(BF16), NT_DIMS, preferred_element_type=F32)
            st_ref[d, h] = st * dec[:, ks] + lax.dot_general(
                v[:, vs], kd[:, ks], TN_DIMS, preferred_element_type=F32)

    def scan(q_ref, k_ref, v_ref, lg_ref, acc_ref):
        n_chunks = q_ref.shape[1] // cs

        def body(i, carry):
            chunk(q_ref, k_ref, v_ref, lg_ref, acc_ref, 0, i * cs)
            chunk(q_ref, k_ref, v_ref, lg_ref, acc_ref, 1, (n_chunks - 1 - i) * cs)
            return carry

        lax.fori_loop(0, n_chunks, body, 0)

    def finish(acc_ref, gr_ref, o_ref):
        n = acc_ref.shape[1]
        tr = _tile(n, 256)

        def body(i, carry):
            rows = pl.ds(pl.multiple_of(i * tr, tr), tr)
            gr = gr_ref[0, rows, :].astype(F32)
            gate = gr * _sigmoid(gr)
            for h in range(GLA_HEADS):
                vs = slice(h * GLA_DV, (h + 1) * GLA_DV)
                o = acc_ref[0, rows, vs] + acc_ref[1, rows, vs]
                o_ref[0, rows, vs] = (_rms(o, on_ref[...]) * gate[:, vs]).astype(o_ref.dtype)
            return carry

        lax.fori_loop(0, n // tr, body, 0)

    scan(qc_ref, kc_ref, vc_ref, lgc_ref, accc_ref)
    scan(ql_ref, kl_ref, vl_ref, lgl_ref, accl_ref)
    if want_ctx:
        finish(accc_ref, grc_ref, oc_ref)
    finish(accl_ref, grl_ref, ol_ref)


def _gla(p_c, p_l, w_gate, b_gate, o_norm, want_ctx, name="gla"):
    bsz, n_ctx, _ = p_c.shape
    n_lat = p_l.shape[1]
    hk, hv = GLA_HEADS * GLA_DK, GLA_HEADS * GLA_DV

    def specs(n):
        return [
            pl.BlockSpec((1, n, hk), lambda b: (b, 0, P_GQ // hk)),
            pl.BlockSpec((1, n, hk), lambda b: (b, 0, P_GK // hk)),
            pl.BlockSpec((1, n, 128), lambda b: (b, 0, P_TAIL // 128)),
            pl.BlockSpec((1, n, hv), lambda b: (b, 0, P_GV // hv)),
            pl.BlockSpec((1, n, hv), lambda b: (b, 0, P_GR // hv)),
        ]

    in_specs = specs(n_ctx) + specs(n_lat) + [
        pl.BlockSpec((128, 2 * hk), lambda b: (0, 0)),
        pl.BlockSpec((1, 2 * hk), lambda b: (0, 0)),
        pl.BlockSpec((1, GLA_DV), lambda b: (0, 0)),
    ]
    out_specs = [pl.BlockSpec((1, n_lat, hv), lambda b: (b, 0, 0))]
    out_shape = [jax.ShapeDtypeStruct((bsz, n_lat, hv), BF16)]
    if want_ctx:
        out_specs.insert(0, pl.BlockSpec((1, n_ctx, hv), lambda b: (b, 0, 0)))
        out_shape.insert(0, jax.ShapeDtypeStruct((bsz, n_ctx, hv), BF16))
    outs = pl.pallas_call(
        functools.partial(_gla_kernel, want_ctx=want_ctx),
        grid=(bsz,),
        in_specs=in_specs,
        out_specs=out_specs,
        out_shape=out_shape,
        scratch_shapes=[
            pltpu.VMEM((n_ctx, 2 * hk), F32), pltpu.VMEM((n_lat, 2 * hk), F32),
            pltpu.VMEM((2, n_ctx, hv), F32), pltpu.VMEM((2, n_lat, hv), F32),
            pltpu.VMEM((2, GLA_HEADS, GLA_DV, GLA_DK), F32),
        ],
        compiler_params=_params("parallel"),
        name=name,
    )(p_c, p_c, p_c, p_c, p_c, p_l, p_l, p_l, p_l, p_l, w_gate, b_gate, o_norm.reshape(1, GLA_DV))
    return (outs[0], outs[1]) if want_ctx else (None, outs[0])


def _conf_kernel(a_ref, gt_ref, w_ref, b_ref, lg_ref, lb_ref, o_ref, u_ref):
    n = a_ref.shape[1]
    pad = (CONV_W - 1) // 2
    lead = 16
    u_ref[0:lead] = jnp.zeros((lead, CONV_CH), F32)
    u_ref[lead + n:lead + n + lead] = jnp.zeros((lead, CONV_CH), F32)
    u_ref[lead:lead + n] = a_ref[0].astype(F32) * _sigmoid(gt_ref[0].astype(F32))
    tr = _tile(n, 256)

    def body(i, carry):
        r0 = pl.multiple_of(i * tr, tr)
        ext = u_ref[pl.ds(r0, tr + 2 * lead), :]
        y = jnp.zeros((tr, CONV_CH), F32) + b_ref[...]
        for k in range(CONV_W):
            off = lead - pad + k
            y = y + pltpu.roll(ext, tr + 2 * lead - off, axis=0)[0:tr] * w_ref[k:k + 1, :]
        mu = jnp.mean(y, axis=-1, keepdims=True)
        yc = y - mu
        z = yc * lax.rsqrt(jnp.mean(yc * yc, axis=-1, keepdims=True) + EPS) * lg_ref[...] + lb_ref[...]
        o_ref[0, pl.ds(r0, tr), :] = (z * _sigmoid(z)).astype(o_ref.dtype)
        return carry

    lax.fori_loop(0, n // tr, body, 0)


def _conformer(p, w, b, ln_g, ln_b, name="conformer"):
    bsz, n, _ = p.shape
    vec = pl.BlockSpec((1, CONV_CH), lambda i: (0, 0))
    return pl.pallas_call(
        _conf_kernel,
        grid=(bsz,),
        in_specs=[
            pl.BlockSpec((1, n, CONV_CH), lambda i: (i, 0, 0)),
            pl.BlockSpec((1, n, CONV_CH), lambda i: (i, 0, 1)),
            pl.BlockSpec((CONV_W + 1, CONV_CH), lambda i: (0, 0)),
            vec, vec, vec,
        ],
        out_specs=pl.BlockSpec((1, n, CONV_CH), lambda i: (i, 0, 0)),
        out_shape=jax.ShapeDtypeStruct((bsz, n, CONV_CH), BF16),
        scratch_shapes=[pltpu.VMEM((n + 32, CONV_CH), F32)],
        compiler_params=_params("parallel"),
        name=name,
    )(p, p, w, b.reshape(1, CONV_CH), ln_g.reshape(1, CONV_CH), ln_b.reshape(1, CONV_CH))


def _fnet_kernel(f_ref, cs_ref, wd_ref, o_ref, z_ref, *, scale):
    n = f_ref.shape[1]

    @pl.when(pl.program_id(1) == 0)
    def _channel_dft():
        zz = _dot(f_ref[0], cs_ref[...])
        z_ref[0:n] = zz[:, 0:FNET_CH].astype(BF16)
        z_ref[n:2 * n] = zz[:, FNET_CH:2 * FNET_CH].astype(BF16)

    o_ref[0] = (_dot(wd_ref[...], z_ref[...]) * scale).astype(o_ref.dtype)


def _dft_tables(n):
    k = np.arange(n, dtype=np.float64)
    ang = 2.0 * np.pi * np.outer(k, k) / n
    return np.cos(ang), np.sin(ang)


def _fnet(p, name="fnet"):
    bsz, n, width = p.shape
    cn, sn = _dft_tables(n)
    wd = jnp.asarray(np.concatenate([cn, -sn], axis=1), BF16)
    cg, sg = _dft_tables(FNET_GROUP_CH)
    eye = np.eye(FNET_GROUPS)
    cs = jnp.asarray(np.concatenate([np.kron(eye, cg), np.kron(eye, sg)], axis=1), BF16)
    tm = _tile(n, 512)
    scale = 1.0 / math.sqrt(n * FNET_GROUP_CH)
    return pl.pallas_call(
        functools.partial(_fnet_kernel, scale=scale),
        grid=(bsz, n // tm),
        in_specs=[
            pl.BlockSpec((1, n, FNET_CH), lambda b, i: (b, 0, (width - FNET_CH) // FNET_CH)),
            pl.BlockSpec((FNET_CH, 2 * FNET_CH), lambda b, i: (0, 0)),
            pl.BlockSpec((tm, 2 * n), lambda b, i: (i, 0)),
        ],
        out_specs=pl.BlockSpec((1, tm, FNET_CH), lambda b, i: (b, i, 0)),
        out_shape=jax.ShapeDtypeStruct((bsz, n, FNET_CH), BF16),
        scratch_shapes=[pltpu.VMEM((2 * n, FNET_CH), BF16)],
        compiler_params=_params("parallel", "arbitrary"),
        name=name,
    )(p, cs, wd)


def _rot_cols(w):
    a, b, c, d = jnp.split(w, 4, axis=-1)
    return jnp.concatenate([-b, a, -d, c], axis=-1)


def _even_weights(w_in, w_uq, w_ukv, w_gfw, b_gfw, w_gbw, b_gbw):
    d = w_in.shape[0]
    sizes = [MLA_Q_RANK, MLA_KV_RANK, MLA_ROPE, GLA_HEADS * GLA_DK, GLA_HEADS * GLA_DK, GLA_HEADS * GLA_DV,
             GLA_GATE_RANK, GLA_GATE_RANK, GLA_HEADS * GLA_DV]
    qc, kvc, kr, gq, gk, gv, glf, glb, gr = jnp.split(w_in, np.cumsum(sizes)[:-1].tolist(), axis=1)
    tail_pad = jnp.zeros((d, 128 - 2 * MLA_ROPE - 2 * GLA_GATE_RANK), w_in.dtype)
    w_p = jnp.concatenate([qc, gq * (GLA_DK ** -0.5), gk, kvc, kr, _rot_cols(kr), glf, glb, tail_pad, gv, gr],
                          axis=1).astype(BF16)
    uq = w_uq.reshape(MLA_Q_RANK, MLA_HEADS, MLA_NOPE + MLA_ROPE) * MLA_SCALE
    qn = uq[:, :, :MLA_NOPE].reshape(MLA_Q_RANK, -1)
    qr = uq[:, :, MLA_NOPE:]
    w_q = jnp.concatenate([qn, qr.reshape(MLA_Q_RANK, -1), _rot_cols(qr).reshape(MLA_Q_RANK, -1)],
                          axis=1).astype(BF16)
    ukv = w_ukv.reshape(MLA_KV_RANK, MLA_HEADS, MLA_NOPE + MLA_V)
    w_kv = jnp.concatenate([ukv[:, :, :MLA_NOPE].reshape(MLA_KV_RANK, -1),
                            ukv[:, :, MLA_NOPE:].reshape(MLA_KV_RANK, -1)], axis=1).astype(BF16)
    hk = GLA_HEADS * GLA_DK
    w_gate = jnp.zeros((128, 2 * hk), F32)
    w_gate = w_gate.at[T_GF:T_GF + GLA_GATE_RANK, :hk].set(w_gfw)
    w_gate = w_gate.at[T_GB:T_GB + GLA_GATE_RANK, hk:].set(w_gbw).astype(BF16)
    b_gate = jnp.concatenate([b_gfw, b_gbw]).reshape(1, 2 * hk)
    return w_p, w_q, w_kv, w_gate, b_gate


def _ffn_weights(w_in, dw_w, dw_b, w_out):
    d, f2 = w_in.shape
    f = f2 // 2
    nf = f // FFN_CHUNK
    wg = w_in[:, :f].reshape(d, nf, FFN_CHUNK).transpose(1, 0, 2).astype(BF16)
    wv = w_in[:, f:].reshape(d, nf, FFN_CHUNK).transpose(1, 0, 2).astype(BF16)
    taps = jnp.concatenate([dw_w, dw_b[None], jnp.zeros((8 - dw_w.shape[0] - 1, f), F32)], axis=0)
    cwb = taps.reshape(8, nf, FFN_CHUNK).transpose(1, 0, 2)
    wo = w_out.reshape(nf, FFN_CHUNK, d).astype(BF16)
    return wg, wv, cwb, wo


def _rope_tables(n):
    rows = n // GRID_W
    row = jnp.repeat(jnp.arange(rows), GRID_W).astype(F32)
    col = jnp.tile(jnp.arange(GRID_W), rows).astype(F32)
    half = MLA_ROPE // 2
    inv = ROPE_BASE ** (-jnp.arange(0, half, 2, dtype=F32) / half)
    ar = row[:, None] * inv
    ac = col[:, None] * inv
    ang = jnp.concatenate([ar, ar, ac, ac], axis=-1)
    return jnp.cos(ang), jnp.sin(ang)


def kernel(x, c, ctx, c_ctx, mod_w, mod_b, pre_mix_g, post_mix_g, pre_ffn_g, post_ffn_g, ev_in_w, mla_q_norm, mla_kv_norm, mla_w_uq, mla_w_ukv, gla_w_gate_fw, gla_b_gate_fw, gla_w_gate_bw, gla_b_gate_bw, gla_o_norm, ev_out_w, od_in_w, conf_dw_w, conf_dw_b, conf_ln_g, conf_ln_b, od_out_w, ffn_in_w, ffn_dw_w, ffn_dw_b, ffn_out_w):
    bsz, n, d = x.shape
    n_ctx = ctx.shape[1]
    depth = mod_w.shape[0]
    last_ctx_reader = ((depth - 1) // 2) * 2

    rows = -(-(bsz + 1) // 8) * 8
    cvec = jnp.concatenate([c, c_ctx[None], jnp.zeros((rows - bsz - 1, d), F32)], axis=0)
    mod = _modulation(cvec, mod_w, mod_b)

    cos, sin = _rope_tables(n)
    cosq, sinq = jnp.tile(cos, (1, MLA_HEADS)), jnp.tile(sin, (1, MLA_HEADS))
    ones_q = jnp.ones((n_ctx, MLA_HEADS * MLA_ROPE), F32)
    zeros_q = jnp.zeros((n_ctx, MLA_HEADS * MLA_ROPE), F32)

    def flat(t):
        return t.reshape(1, bsz * n_ctx, t.shape[-1])

    def unflat(t):
        return t.reshape(bsz, n_ctx, t.shape[-1])

    x_lat, x_ctx = x, ctx
    for l in range(depth):
        need_ctx = l < last_ctx_reader
        use_ctx = need_ctx or (l % 2 == 0)
        i = l // 2
        m_lat = [t.reshape(bsz, 1, d) for t in jnp.split(mod[l, :bsz], 6, axis=-1)]
        m_ctx = [t.reshape(1, 1, d) for t in jnp.split(mod[l, bsz:bsz + 1], 6, axis=-1)]
        sh1, sc1, g1, sh2, sc2, g2 = m_lat
        csh1, csc1, cg1, csh2, csc2, cg2 = m_ctx

        if l % 2 == 0:
            w_p, w_q, w_kv, w_gate, b_gate = _even_weights(
                ev_in_w[i], mla_w_uq[i], mla_w_ukv[i], gla_w_gate_fw[i], gla_b_gate_fw[i],
                gla_w_gate_bw[i], gla_b_gate_bw[i])
            w_o = ev_out_w[i].astype(BF16)
            k1 = MLA_HEADS * MLA_V
            p_lat = _nmm(x_lat, 0, pre_mix_g[l], w_p, sc1, sh1, name="even_in_lat")
            p_ctx = unflat(_nmm(flat(x_ctx), 0, pre_mix_g[l], w_p, csc1, csh1, name="even_in_ctx"))
            q_lat = _nmm(p_lat, P_QC, mla_q_norm[i], w_q, name="mla_q_lat")
            kv_lat = _nmm(p_lat, P_KVC, mla_kv_norm[i], w_kv, name="mla_kv_lat")
            kv_ctx = unflat(_nmm(flat(p_ctx), P_KVC, mla_kv_norm[i], w_kv, name="mla_kv_ctx"))
            a_lat = _mla(q_lat, cosq, sinq, kv_ctx, p_ctx, kv_lat, p_lat, cos, sin, name="mla_lat")
            g_ctx, g_lat = _gla(p_ctx, p_lat, w_gate, b_gate, gla_o_norm[i], need_ctx)
            x_lat = _mnr(a_lat, g_lat, w_o[:k1], w_o[k1:], x_lat, g1, post_mix_g[l], name="even_out_lat")
            if need_ctx:
                q_ctx = unflat(_nmm(flat(p_ctx), P_QC, mla_q_norm[i], w_q, name="mla_q_ctx"))
                a_ctx = _mla(q_ctx, ones_q, zeros_q, kv_ctx, p_ctx, name="mla_ctx")
                x_ctx = unflat(_mnr(flat(a_ctx), flat(g_ctx), w_o[:k1], w_o[k1:], flat(x_ctx), cg1,
                                    post_mix_g[l], name="even_out_ctx"))
        else:
            w_p = od_in_w[i].astype(BF16)
            w_o = od_out_w[i].astype(BF16)
            w_dw = jnp.concatenate([conf_dw_w[i], jnp.zeros((1, CONV_CH), F32)], axis=0)
            streams = [(x_lat, sc1, sh1, g1, "lat")]
            if need_ctx:
                streams.append((x_ctx, csc1, csh1, cg1, "ctx"))
            outs = []
            for xs, sc, sh, gate, tag in streams:
                per_batch = sc.shape[0] > 1
                xin = xs if per_batch else flat(xs)
                p = _nmm(xin, 0, pre_mix_g[l], w_p, sc, sh, name="odd_in_" + tag)
                p = p if per_batch else unflat(p)
                u = _conformer(p, w_dw, conf_dw_b[i], conf_ln_g[i], conf_ln_b[i], name="conformer_" + tag)
                fm = _fnet(p, name="fnet_" + tag)
                if not per_batch:
                    u, fm = flat(u), flat(fm)
                y = _mnr(u, fm, w_o[:CONV_CH], w_o[CONV_CH:], xin, gate, post_mix_g[l], name="odd_out_" + tag)
                outs.append(y if per_batch else unflat(y))
            x_lat = outs[0]
            if need_ctx:
                x_ctx = outs[1]

        wg, wv, cwb, wo = _ffn_weights(ffn_in_w[l], ffn_dw_w[l], ffn_dw_b[l], ffn_out_w[l])
        x_lat = _ffn(x_lat, pre_ffn_g[l], sc2, sh2, g2, post_ffn_g[l], wg, wv, cwb, wo, n, name="ffn_lat")
        if need_ctx:
            x_ctx = unflat(_ffn(flat(x_ctx), pre_ffn_g[l], csc2, csh2, cg2, post_ffn_g[l], wg, wv, cwb, wo,
                                n_ctx, name="ffn_ctx"))
    return x_lat
```

```python
import functools
import math

import numpy as np
import jax
import jax.numpy as jnp
from jax import lax
from jax.experimental import pallas as pl
from jax.experimental.pallas import tpu as pltpu

F32 = jnp.float32
BF16 = jnp.bfloat16

GRID_W = 64
ROPE_BASE = 10000.0
MLA_HEADS = 8
MLA_Q_RANK = 256
MLA_KV_RANK = 128
MLA_NOPE = 64
MLA_ROPE = 32
MLA_V = 64
MLA_SCALE = (MLA_NOPE + MLA_ROPE) ** -0.5
GLA_HEADS = 4
GLA_DK = 64
GLA_DV = 128
GLA_GATE_RANK = 16
GLA_GATE_NORM = 16.0
GLA_CHUNK = 64
CONV_CH = 768
CONV_W = 31
FNET_GROUPS = 4
FNET_GROUP_CH = 64
FNET_CH = FNET_GROUPS * FNET_GROUP_CH
FFN_CHUNK = 256
EPS = 1e-6

P_QC, P_GQ, P_GK = 0, 256, 512
P_KVC, P_TAIL = 768, 896
P_GV, P_GR = 1024, 1536
P_WIDTH = 2048
T_KR, T_KRROT, T_GF, T_GB = 0, 32, 64, 80

NT_DIMS = (((1,), (1,)), ((), ()))
TN_DIMS = (((0,), (0,)), ((), ()))

VMEM_LIMIT = 56 * 1024 * 1024


def _params(*sem):
    return pltpu.CompilerParams(dimension_semantics=sem, vmem_limit_bytes=VMEM_LIMIT)


def _rms(xf, gain):
    return xf * lax.rsqrt(jnp.mean(xf * xf, axis=-1, keepdims=True) + EPS) * gain


def _sigmoid(x):
    return 1.0 / (1.0 + jnp.exp(-x))


def _dot(a, b):
    return jnp.dot(a, b, preferred_element_type=F32)


def _tile(n, want):
    t = min(n, want)
    assert n % t == 0, (n, want)
    return t


def _mod_kernel(c_ref, w_ref, b_ref, o_ref):
    c = c_ref[...]
    s = c * _sigmoid(c)
    o_ref[0] = _dot(s.astype(BF16), w_ref[0].astype(BF16)) + b_ref[0]


def _modulation(cvec, mod_w, mod_b):
    depth, d, n6 = mod_w.shape
    rows = cvec.shape[0]
    tn = _tile(n6, 1536)
    return pl.pallas_call(
        _mod_kernel,
        grid=(depth, n6 // tn),
        in_specs=[
            pl.BlockSpec((rows, d), lambda l, j: (0, 0)),
            pl.BlockSpec((1, d, tn), lambda l, j: (l, 0, j)),
            pl.BlockSpec((1, 1, tn), lambda l, j: (l, 0, j)),
        ],
        out_specs=pl.BlockSpec((1, rows, tn), lambda l, j: (l, 0, j)),
        out_shape=jax.ShapeDtypeStruct((depth, rows, n6), F32),
        compiler_params=_params("parallel", "parallel"),
        name="modulation",
    )(cvec, mod_w, mod_b.reshape(depth, 1, n6))


def _nmm_kernel(*refs, modulate, col_w):
    if modulate:
        x_ref, g_ref, sc_ref, sh_ref, w_ref, o_ref = refs
    else:
        x_ref, g_ref, w_ref, o_ref = refs
    h = _rms(x_ref[0].astype(F32), g_ref[...])
    if modulate:
        h = h * (1.0 + sc_ref[0]) + sh_ref[0]
    hb = h.astype(BF16)
    for j in range(o_ref.shape[2] // col_w):
        cols = slice(j * col_w, (j + 1) * col_w)
        o_ref[0, :, cols] = _dot(hb, w_ref[:, cols]).astype(o_ref.dtype)


def _nmm(x, col_off, gain, w, sc=None, sh=None, tm=512, name="nmm"):
    bsz, n, _ = x.shape
    k, nout = w.shape
    assert col_off % k == 0
    cb = col_off // k
    tm = _tile(n, tm)
    col_w = next(cw for cw in (512, 256, 128) if nout % cw == 0)
    modulate = sc is not None
    in_specs = [pl.BlockSpec((1, tm, k), lambda b, i: (b, i, cb)),
                pl.BlockSpec((1, k), lambda b, i: (0, 0))]
    args = [x, gain.reshape(1, k)]
    if modulate:
        per_batch = sc.shape[0] > 1
        mod_spec = pl.BlockSpec((1, 1, k), (lambda b, i: (b, 0, 0)) if per_batch else (lambda b, i: (0, 0, 0)))
        in_specs += [mod_spec, mod_spec]
        args += [sc, sh]
    in_specs.append(pl.BlockSpec((k, nout), lambda b, i: (0, 0)))
    args.append(w)
    return pl.pallas_call(
        functools.partial(_nmm_kernel, modulate=modulate, col_w=col_w),
        grid=(bsz, n // tm),
        in_specs=in_specs,
        out_specs=pl.BlockSpec((1, tm, nout), lambda b, i: (b, i, 0)),
        out_shape=jax.ShapeDtypeStruct((bsz, n, nout), BF16),
        compiler_params=_params("parallel", "parallel"),
        name=name,
    )(*args)


def _mnr_kernel(a1_ref, a2_ref, w1_ref, w2_ref, x_ref, gate_ref, pg_ref, o_ref):
    y = _dot(a1_ref[0], w1_ref[...]) + _dot(a2_ref[0], w2_ref[...])
    o_ref[0] = x_ref[0] + gate_ref[0] * _rms(y, pg_ref[...])


def _mnr(a1, a2, w1, w2, x, gate, post_g, tm=512, name="mnr"):
    bsz, n, d = x.shape
    k1, k2 = a1.shape[2], a2.shape[2]
    tm = _tile(n, tm)
    per_batch = gate.shape[0] > 1
    return pl.pallas_call(
        _mnr_kernel,
        grid=(bsz, n // tm),
        in_specs=[
            pl.BlockSpec((1, tm, k1), lambda b, i: (b, i, 0)),
            pl.BlockSpec((1, tm, k2), lambda b, i: (b, i, 0)),
            pl.BlockSpec((k1, d), lambda b, i: (0, 0)),
            pl.BlockSpec((k2, d), lambda b, i: (0, 0)),
            pl.BlockSpec((1, tm, d), lambda b, i: (b, i, 0)),
            pl.BlockSpec((1, 1, d), (lambda b, i: (b, 0, 0)) if per_batch else (lambda b, i: (0, 0, 0))),
            pl.BlockSpec((1, d), lambda b, i: (0, 0)),
        ],
        out_specs=pl.BlockSpec((1, tm, d), lambda b, i: (b, i, 0)),
        out_shape=jax.ShapeDtypeStruct((bsz, n, d), F32),
        compiler_params=_params("parallel", "parallel"),
        name=name,
    )(a1, a2, w1, w2, x, gate, post_g.reshape(1, d))


def _ffn_kernel(x_ref, xp_ref, xn_ref, g_ref, sc_ref, sh_ref, gate_ref, pg_ref,
                wg_ref, wv_ref, cw_ref, wo_ref, o_ref, xh_ref, acc_ref, g0_ref, g1_ref, v0_ref, v1_ref, *, seq_len):
    tm = x_ref.shape[1]
    n_chunks = wg_ref.shape[0]
    i = pl.program_id(1)
    gain = g_ref[...]
    scale = 1.0 + sc_ref[0]
    shift = sh_ref[0]

    def modulated(rows):
        return (_rms(rows, gain) * scale + shift).astype(BF16)

    xh_ref[0:tm] = modulated(x_ref[0])
    xh_ref[tm:tm + 16] = modulated(jnp.concatenate([xp_ref[0], xn_ref[0]], axis=0))

    local = lax.broadcasted_iota(jnp.int32, (tm, 1), 0)
    pos = (i * tm + local) % seq_len
    seq_first = pos == 0
    seq_last = pos == seq_len - 1
    tile_first = local == 0
    tile_last = local == tm - 1
    acc_ref[...] = jnp.zeros_like(acc_ref)
    slots = ((g0_ref, v0_ref), (g1_ref, v1_ref))

    def project(c, slot):
        gs_ref, vs_ref = slots[slot]
        gs_ref[...] = _dot(xh_ref[...], wg_ref[c])
        vs_ref[...] = _dot(xh_ref[0:tm], wv_ref[c])

    def mix(c, slot):
        gs_ref, vs_ref = slots[slot]
        gm = gs_ref[0:tm]
        g_prev = jnp.where(tile_first, gs_ref[tm + 7:tm + 8], pltpu.roll(gm, 1, axis=0))
        g_prev = jnp.where(seq_first, 0.0, g_prev)
        g_next = jnp.where(tile_last, gs_ref[tm + 8:tm + 9], pltpu.roll(gm, tm - 1, axis=0))
        g_next = jnp.where(seq_last, 0.0, g_next)
        cw = cw_ref[c]
        y = g_prev * cw[0:1] + gm * cw[1:2] + g_next * cw[2:3] + cw[3:4]
        u = 0.5 * y * (1.0 + lax.erf(y * (1.0 / math.sqrt(2.0)))) * vs_ref[...]
        acc_ref[...] += _dot(u.astype(BF16), wo_ref[c])

    project(0, 0)

    def body(j, carry):
        c = 2 * j
        project(c + 1, 1)
        mix(c, 0)
        project(c + 2, 0)
        mix(c + 1, 1)
        return carry

    lax.fori_loop(0, (n_chunks - 1) // 2, body, 0)
    if n_chunks % 2 == 0:
        project(n_chunks - 1, 1)
        mix(n_chunks - 2, 0)
        mix(n_chunks - 1, 1)
    else:
        mix(n_chunks - 1, 0)
    o_ref[0] = x_ref[0] + gate_ref[0] * _rms(acc_ref[...], pg_ref[...])


def _ffn(x, pre_g, sc, sh, gate, post_g, wg, wv, cwb, wo, seq_len, tm=512, name="ffn"):
    bsz, n, d = x.shape
    nf, _, fc = wg.shape
    tm = _tile(n, tm)
    nb8 = n // 8
    per_batch = sc.shape[0] > 1
    mod_spec = pl.BlockSpec((1, 1, d), (lambda b, i: (b, 0, 0)) if per_batch else (lambda b, i: (0, 0, 0)))
    vec_spec = pl.BlockSpec((1, d), lambda b, i: (0, 0))
    return pl.pallas_call(
        functools.partial(_ffn_kernel, seq_len=seq_len),
        grid=(bsz, n // tm),
        in_specs=[
            pl.BlockSpec((1, tm, d), lambda b, i: (b, i, 0)),
            pl.BlockSpec((1, 8, d), lambda b, i: (b, jnp.maximum(i * (tm // 8) - 1, 0), 0)),
            pl.BlockSpec((1, 8, d), lambda b, i: (b, jnp.minimum((i + 1) * (tm // 8), nb8 - 1), 0)),
            vec_spec, mod_spec, mod_spec, mod_spec, vec_spec,
            pl.BlockSpec((nf, d, fc), lambda b, i: (0, 0, 0), pipeline_mode=pl.Buffered(1)),
            pl.BlockSpec((nf, d, fc), lambda b, i: (0, 0, 0), pipeline_mode=pl.Buffered(1)),
            pl.BlockSpec((nf, 8, fc), lambda b, i: (0, 0, 0), pipeline_mode=pl.Buffered(1)),
            pl.BlockSpec((nf, fc, d), lambda b, i: (0, 0, 0), pipeline_mode=pl.Buffered(1)),
        ],
        out_specs=pl.BlockSpec((1, tm, d), lambda b, i: (b, i, 0)),
        out_shape=jax.ShapeDtypeStruct((bsz, n, d), F32),
        scratch_shapes=[pltpu.VMEM((tm + 16, d), BF16), pltpu.VMEM((tm, d), F32),
                        pltpu.VMEM((tm + 16, fc), F32), pltpu.VMEM((tm + 16, fc), F32),
                        pltpu.VMEM((tm, fc), F32), pltpu.VMEM((tm, fc), F32)],
        compiler_params=_params("parallel", "parallel"),
        name=name,
    )(x, x, x, pre_g.reshape(1, d), sc, sh, gate, post_g.reshape(1, d), wg, wv, cwb, wo)


def _mla_kernel(*refs, n_ctx, n_lat):
    if n_lat:
        q_ref, cq_ref, sq_ref, kvc_ref, tc_ref, kvl_ref, tl_ref, ck_ref, sk_ref, o_ref, kc_ref, vc_ref = refs
    else:
        q_ref, cq_ref, sq_ref, kvc_ref, tc_ref, o_ref, kc_ref, vc_ref = refs
    hv = MLA_HEADS * MLA_NOPE

    @pl.when(pl.program_id(1) == 0)
    def _build_keys():
        def fill(kv_ref, t_ref, off, n, roped):
            tail = t_ref[0].astype(F32)
            kr = tail[:, T_KR:T_KR + MLA_ROPE]
            if roped:
                kr = kr * ck_ref[...] + tail[:, T_KRROT:T_KRROT + MLA_ROPE] * sk_ref[...]
            kr = kr.astype(BF16)
            for h in range(MLA_HEADS):
                kc_ref[h, off:off + n, 0:MLA_NOPE] = kv_ref[0, :, h * MLA_NOPE:(h + 1) * MLA_NOPE]
                kc_ref[h, off:off + n, MLA_NOPE:MLA_NOPE + MLA_ROPE] = kr
                kc_ref[h, off:off + n, MLA_NOPE + MLA_ROPE:128] = jnp.zeros((n, 128 - MLA_NOPE - MLA_ROPE), BF16)
                vc_ref[h, off:off + n, 0:MLA_V] = kv_ref[0, :, hv + h * MLA_V:hv + (h + 1) * MLA_V]
                vc_ref[h, off:off + n, MLA_V:128] = jnp.ones((n, 128 - MLA_V), BF16)

        fill(kvc_ref, tc_ref, 0, n_ctx, False)
        if n_lat:
            fill(kvl_ref, tl_ref, n_ctx, n_lat, True)

    tq = q_ref.shape[1]
    hr = MLA_HEADS * MLA_ROPE
    qr = (q_ref[0, :, hv:hv + hr].astype(F32) * cq_ref[...]
          + q_ref[0, :, hv + hr:hv + 2 * hr].astype(F32) * sq_ref[...]).astype(BF16)
    pad = jnp.zeros((tq, 128 - MLA_NOPE - MLA_ROPE), BF16)
    for h in range(MLA_HEADS):
        qh = jnp.concatenate([q_ref[0, :, h * MLA_NOPE:(h + 1) * MLA_NOPE],
                              qr[:, h * MLA_ROPE:(h + 1) * MLA_ROPE], pad], axis=1)
        s = lax.dot_general(qh, kc_ref[h], NT_DIMS, preferred_element_type=F32)
        p = jnp.exp(s - jnp.max(s, axis=-1, keepdims=True)).astype(BF16)
        oe = _dot(p, vc_ref[h])
        o_ref[0, :, h * MLA_V:(h + 1) * MLA_V] = (oe[:, 0:MLA_V] / oe[:, MLA_V:MLA_V + 1]).astype(o_ref.dtype)


def _mla(q, cosq, sinq, kv_c, p_c, kv_l=None, p_l=None, cosk=None, sink=None, tq=256, name="mla"):
    bsz, nq, qw = q.shape
    n_ctx = kv_c.shape[1]
    n_lat = 0 if kv_l is None else kv_l.shape[1]
    nk = n_ctx + n_lat
    tq = _tile(nq, tq)
    hr = MLA_HEADS * MLA_ROPE
    tail_blk = P_TAIL // 128
    in_specs = [
        pl.BlockSpec((1, tq, qw), lambda b, i: (b, i, 0)),
        pl.BlockSpec((tq, hr), lambda b, i: (i, 0)),
        pl.BlockSpec((tq, hr), lambda b, i: (i, 0)),
        pl.BlockSpec((1, n_ctx, kv_c.shape[2]), lambda b, i: (b, 0, 0)),
        pl.BlockSpec((1, n_ctx, 128), lambda b, i: (b, 0, tail_blk)),
    ]
    args = [q, cosq, sinq, kv_c, p_c]
    if n_lat:
        in_specs += [
            pl.BlockSpec((1, n_lat, kv_l.shape[2]), lambda b, i: (b, 0, 0)),
            pl.BlockSpec((1, n_lat, 128), lambda b, i: (b, 0, tail_blk)),
            pl.BlockSpec((n_lat, MLA_ROPE), lambda b, i: (0, 0)),
            pl.BlockSpec((n_lat, MLA_ROPE), lambda b, i: (0, 0)),
        ]
        args += [kv_l, p_l, cosk, sink]
    return pl.pallas_call(
        functools.partial(_mla_kernel, n_ctx=n_ctx, n_lat=n_lat),
        grid=(bsz, nq // tq),
        in_specs=in_specs,
        out_specs=pl.BlockSpec((1, tq, MLA_HEADS * MLA_V), lambda b, i: (b, i, 0)),
        out_shape=jax.ShapeDtypeStruct((bsz, nq, MLA_HEADS * MLA_V), BF16),
        scratch_shapes=[pltpu.VMEM((MLA_HEADS, nk, 128), BF16), pltpu.VMEM((MLA_HEADS, nk, 128), BF16)],
        compiler_params=_params("parallel", "arbitrary"),
        name=name,
    )(*args)


def _gla_kernel(*refs, want_ctx):
    (qc_ref, kc_ref, tc_ref, vc_ref, grc_ref, ql_ref, kl_ref, tl_ref, vl_ref, grl_ref,
     wg_ref, bg_ref, on_ref) = refs[:13]
    if want_ctx:
        oc_ref, ol_ref = refs[13:15]
        scratch = refs[15:]
    else:
        oc_ref, ol_ref = None, refs[13]
        scratch = refs[14:]
    lgc_ref, lgl_ref, accc_ref, accl_ref, st_ref = scratch
    cs = GLA_CHUNK
    hk = GLA_HEADS * GLA_DK

    for t_ref, lg_ref in ((tc_ref, lgc_ref), (tl_ref, lgl_ref)):
        z = _dot(t_ref[0], wg_ref[...]) + bg_ref[...]
        lg_ref[...] = (jnp.minimum(z, 0.0) - jnp.log(1.0 + jnp.exp(-jnp.abs(z)))) * (1.0 / GLA_GATE_NORM)

    row = lax.broadcasted_iota(jnp.int32, (cs, cs), 0)
    col = lax.broadcasted_iota(jnp.int32, (cs, cs), 1)
    masks = (row >= col, row <= col)
    tris = tuple(m.astype(F32) for m in masks)
    st_ref[...] = jnp.zeros_like(st_ref)

    def chunk(q_ref, k_ref, v_ref, lg_ref, acc_ref, d, r0):
        rows = pl.ds(pl.multiple_of(r0, cs), cs)
        g = lg_ref[rows, d * hk:(d + 1) * hk]
        cum = jnp.dot(tris[d], g, precision=lax.Precision.HIGHEST, preferred_element_type=F32)
        tot = cum[cs - 1:cs] if d == 0 else cum[0:1]
        ref = 0.5 * tot
        q = q_ref[0, rows, :].astype(F32)
        k = k_ref[0, rows, :].astype(F32)
        v = v_ref[0, rows, :]
        qt = (q * jnp.exp(cum - ref)).astype(BF16)
        kt = (k * jnp.exp(ref - cum)).astype(BF16)
        qs = (q * jnp.exp(cum)).astype(BF16)
        kd = (k * jnp.exp(tot - cum)).astype(BF16)
        dec = jnp.exp(tot)
        for h in range(GLA_HEADS):
            ks = slice(h * GLA_DK, (h + 1) * GLA_DK)
            vs = slice(h * GLA_DV, (h + 1) * GLA_DV)
            a = lax.dot_general(qt[:, ks], kt[:, ks], NT_DIMS, preferred_element_type=F32)
            a = jnp.where(masks[d], a, 0.0).astype(BF16)
            st = st_ref[d, h]
            acc_ref[d, rows, vs] = _dot(a, v[:, vs]) + lax.dot_general(
                qs[:, ks], st.astype(BF16), NT_DIMS, preferred_element_type=F32)
            st_ref[d, h] = st * dec[:, ks] + lax.dot_general(
                v[:, vs], kd[:, ks], TN_DIMS, preferred_element_type=F32)

    def scan(q_ref, k_ref, v_ref, lg_ref, acc_ref):
        n_chunks = q_ref.shape[1] // cs

        def body(i, carry):
            chunk(q_ref, k_ref, v_ref, lg_ref, acc_ref, 0, i * cs)
            chunk(q_ref, k_ref, v_ref, lg_ref, acc_ref, 1, (n_chunks - 1 - i) * cs)
            return carry

        lax.fori_loop(0, n_chunks, body, 0)

    def finish(acc_ref, gr_ref, o_ref):
        n = acc_ref.shape[1]
        tr = _tile(n, 256)

        def body(i, carry):
            rows = pl.ds(pl.multiple_of(i * tr, tr), tr)
            gr = gr_ref[0, rows, :].astype(F32)
            gate = gr * _sigmoid(gr)
            for h in range(GLA_HEADS):
                vs = slice(h * GLA_DV, (h + 1) * GLA_DV)
                o = acc_ref[0, rows, vs] + acc_ref[1, rows, vs]
                o_ref[0, rows, vs] = (_rms(o, on_ref[...]) * gate[:, vs]).astype(o_ref.dtype)
            return carry

        lax.fori_loop(0, n // tr, body, 0)

    scan(qc_ref, kc_ref, vc_ref, lgc_ref, accc_ref)
    scan(ql_ref, kl_ref, vl_ref, lgl_ref, accl_ref)
    if want_ctx:
        finish(accc_ref, grc_ref, oc_ref)
    finish(accl_ref, grl_ref, ol_ref)


def _gla(p_c, p_l, w_gate, b_gate, o_norm, want_ctx, name="gla"):
    bsz, n_ctx, _ = p_c.shape
    n_lat = p_l.shape[1]
    hk, hv = GLA_HEADS * GLA_DK, GLA_HEADS * GLA_DV

    def specs(n):
        return [
            pl.BlockSpec((1, n, hk), lambda b: (b, 0, P_GQ // hk)),
            pl.BlockSpec((1, n, hk), lambda b: (b, 0, P_GK // hk)),
            pl.BlockSpec((1, n, 128), lambda b: (b, 0, P_TAIL // 128)),
            pl.BlockSpec((1, n, hv), lambda b: (b, 0, P_GV // hv)),
            pl.BlockSpec((1, n, hv), lambda b: (b, 0, P_GR // hv)),
        ]

    in_specs = specs(n_ctx) + specs(n_lat) + [
        pl.BlockSpec((128, 2 * hk), lambda b: (0, 0)),
        pl.BlockSpec((1, 2 * hk), lambda b: (0, 0)),
        pl.BlockSpec((1, GLA_DV), lambda b: (0, 0)),
    ]
    out_specs = [pl.BlockSpec((1, n_lat, hv), lambda b: (b, 0, 0))]
    out_shape = [jax.ShapeDtypeStruct((bsz, n_lat, hv), BF16)]
    if want_ctx:
        out_specs.insert(0, pl.BlockSpec((1, n_ctx, hv), lambda b: (b, 0, 0)))
        out_shape.insert(0, jax.ShapeDtypeStruct((bsz, n_ctx, hv), BF16))
    outs = pl.pallas_call(
        functools.partial(_gla_kernel, want_ctx=want_ctx),
        grid=(bsz,),
        in_specs=in_specs,
        out_specs=out_specs,
        out_shape=out_shape,
        scratch_shapes=[
            pltpu.VMEM((n_ctx, 2 * hk), F32), pltpu.VMEM((n_lat, 2 * hk), F32),
            pltpu.VMEM((2, n_ctx, hv), F32), pltpu.VMEM((2, n_lat, hv), F32),
            pltpu.VMEM((2, GLA_HEADS, GLA_DV, GLA_DK), F32),
        ],
        compiler_params=_params("parallel"),
        name=name,
    )(p_c, p_c, p_c, p_c, p_c, p_l, p_l, p_l, p_l, p_l, w_gate, b_gate, o_norm.reshape(1, GLA_DV))
    return (outs[0], outs[1]) if want_ctx else (None, outs[0])


def _conf_kernel(a_ref, gt_ref, w_ref, b_ref, lg_ref, lb_ref, o_ref, u_ref):
    n = a_ref.shape[1]
    pad = (CONV_W - 1) // 2
    lead = 16
    u_ref[0:lead] = jnp.zeros((lead, CONV_CH), F32)
    u_ref[lead + n:lead + n + lead] = jnp.zeros((lead, CONV_CH), F32)
    u_ref[lead:lead + n] = a_ref[0].astype(F32) * _sigmoid(gt_ref[0].astype(F32))
    tr = _tile(n, 256)

    def body(i, carry):
        r0 = pl.multiple_of(i * tr, tr)
        ext = u_ref[pl.ds(r0, tr + 2 * lead), :]
        y = jnp.zeros((tr, CONV_CH), F32) + b_ref[...]
        for k in range(CONV_W):
            off = lead - pad + k
            y = y + pltpu.roll(ext, tr + 2 * lead - off, axis=0)[0:tr] * w_ref[k:k + 1, :]
        mu = jnp.mean(y, axis=-1, keepdims=True)
        yc = y - mu
        z = yc * lax.rsqrt(jnp.mean(yc * yc, axis=-1, keepdims=True) + EPS) * lg_ref[...] + lb_ref[...]
        o_ref[0, pl.ds(r0, tr), :] = (z * _sigmoid(z)).astype(o_ref.dtype)
        return carry

    lax.fori_loop(0, n // tr, body, 0)


def _conformer(p, w, b, ln_g, ln_b, name="conformer"):
    bsz, n, _ = p.shape
    vec = pl.BlockSpec((1, CONV_CH), lambda i: (0, 0))
    return pl.pallas_call(
        _conf_kernel,
        grid=(bsz,),
        in_specs=[
            pl.BlockSpec((1, n, CONV_CH), lambda i: (i, 0, 0)),
            pl.BlockSpec((1, n, CONV_CH), lambda i: (i, 0, 1)),
            pl.BlockSpec((CONV_W + 1, CONV_CH), lambda i: (0, 0)),
            vec, vec, vec,
        ],
        out_specs=pl.BlockSpec((1, n, CONV_CH), lambda i: (i, 0, 0)),
        out_shape=jax.ShapeDtypeStruct((bsz, n, CONV_CH), BF16),
        scratch_shapes=[pltpu.VMEM((n + 32, CONV_CH), F32)],
        compiler_params=_params("parallel"),
        name=name,
    )(p, p, w, b.reshape(1, CONV_CH), ln_g.reshape(1, CONV_CH), ln_b.reshape(1, CONV_CH))


def _fnet_kernel(f_ref, cs_ref, wd_ref, o_ref, z_ref, *, scale):
    n = f_ref.shape[1]

    @pl.when(pl.program_id(1) == 0)
    def _channel_dft():
        zz = _dot(f_ref[0], cs_ref[...])
        z_ref[0:n] = zz[:, 0:FNET_CH].astype(BF16)
        z_ref[n:2 * n] = zz[:, FNET_CH:2 * FNET_CH].astype(BF16)

    o_ref[0] = (_dot(wd_ref[...], z_ref[...]) * scale).astype(o_ref.dtype)


def _dft_tables(n):
    k = np.arange(n, dtype=np.float64)
    ang = 2.0 * np.pi * np.outer(k, k) / n
    return np.cos(ang), np.sin(ang)


def _fnet(p, name="fnet"):
    bsz, n, width = p.shape
    cn, sn = _dft_tables(n)
    wd = jnp.asarray(np.concatenate([cn, -sn], axis=1), BF16)
    cg, sg = _dft_tables(FNET_GROUP_CH)
    eye = np.eye(FNET_GROUPS)
    cs = jnp.asarray(np.concatenate([np.kron(eye, cg), np.kron(eye, sg)], axis=1), BF16)
    tm = _tile(n, 512)
    scale = 1.0 / math.sqrt(n * FNET_GROUP_CH)
    return pl.pallas_call(
        functools.partial(_fnet_kernel, scale=scale),
        grid=(bsz, n // tm),
        in_specs=[
            pl.BlockSpec((1, n, FNET_CH), lambda b, i: (b, 0, (width - FNET_CH) // FNET_CH)),
            pl.BlockSpec((FNET_CH, 2 * FNET_CH), lambda b, i: (0, 0)),
            pl.BlockSpec((tm, 2 * n), lambda b, i: (i, 0)),
        ],
        out_specs=pl.BlockSpec((1, tm, FNET_CH), lambda b, i: (b, i, 0)),
        out_shape=jax.ShapeDtypeStruct((bsz, n, FNET_CH), BF16),
        scratch_shapes=[pltpu.VMEM((2 * n, FNET_CH), BF16)],
        compiler_params=_params("parallel", "arbitrary"),
        name=name,
    )(p, cs, wd)


def _rot_cols(w):
    a, b, c, d = jnp.split(w, 4, axis=-1)
    return jnp.concatenate([-b, a, -d, c], axis=-1)


def _even_weights(w_in, w_uq, w_ukv, w_gfw, b_gfw, w_gbw, b_gbw):
    d = w_in.shape[0]
    sizes = [MLA_Q_RANK, MLA_KV_RANK, MLA_ROPE, GLA_HEADS * GLA_DK, GLA_HEADS * GLA_DK, GLA_HEADS * GLA_DV,
             GLA_GATE_RANK, GLA_GATE_RANK, GLA_HEADS * GLA_DV]
    qc, kvc, kr, gq, gk, gv, glf, glb, gr = jnp.split(w_in, np.cumsum(sizes)[:-1].tolist(), axis=1)
    tail_pad = jnp.zeros((d, 128 - 2 * MLA_ROPE - 2 * GLA_GATE_RANK), w_in.dtype)
    w_p = jnp.concatenate([qc, gq * (GLA_DK ** -0.5), gk, kvc, kr, _rot_cols(kr), glf, glb, tail_pad, gv, gr],
                          axis=1).astype(BF16)
    uq = w_uq.reshape(MLA_Q_RANK, MLA_HEADS, MLA_NOPE + MLA_ROPE) * MLA_SCALE
    qn = uq[:, :, :MLA_NOPE].reshape(MLA_Q_RANK, -1)
    qr = uq[:, :, MLA_NOPE:]
    w_q = jnp.concatenate([qn, qr.reshape(MLA_Q_RANK, -1), _rot_cols(qr).reshape(MLA_Q_RANK, -1)],
                          axis=1).astype(BF16)
    ukv = w_ukv.reshape(MLA_KV_RANK, MLA_HEADS, MLA_NOPE + MLA_V)
    w_kv = jnp.concatenate([ukv[:, :, :MLA_NOPE].reshape(MLA_KV_RANK, -1),
                            ukv[:, :, MLA_NOPE:].reshape(MLA_KV_RANK, -1)], axis=1).astype(BF16)
    hk = GLA_HEADS * GLA_DK
    w_gate = jnp.zeros((128, 2 * hk), F32)
    w_gate = w_gate.at[T_GF:T_GF + GLA_GATE_RANK, :hk].set(w_gfw)
    w_gate = w_gate.at[T_GB:T_GB + GLA_GATE_RANK, hk:].set(w_gbw).astype(BF16)
    b_gate = jnp.concatenate([b_gfw, b_gbw]).reshape(1, 2 * hk)
    return w_p, w_q, w_kv, w_gate, b_gate


def _ffn_weights(w_in, dw_w, dw_b, w_out):
    d, f2 = w_in.shape
    f = f2 // 2
    nf = f // FFN_CHUNK
    wg = w_in[:, :f].reshape(d, nf, FFN_CHUNK).transpose(1, 0, 2).astype(BF16)
    wv = w_in[:, f:].reshape(d, nf, FFN_CHUNK).transpose(1, 0, 2).astype(BF16)
    taps = jnp.concatenate([dw_w, dw_b[None], jnp.zeros((8 - dw_w.shape[0] - 1, f), F32)], axis=0)
    cwb = taps.reshape(8, nf, FFN_CHUNK).transpose(1, 0, 2)
    wo = w_out.reshape(nf, FFN_CHUNK, d).astype(BF16)
    return wg, wv, cwb, wo


def _rope_tables(n):
    rows = n // GRID_W
    row = jnp.repeat(jnp.arange(rows), GRID_W).astype(F32)
    col = jnp.tile(jnp.arange(GRID_W), rows).astype(F32)
    half = MLA_ROPE // 2
    inv = ROPE_BASE ** (-jnp.arange(0, half, 2, dtype=F32) / half)
    ar = row[:, None] * inv
    ac = col[:, None] * inv
    ang = jnp.concatenate([ar, ar, ac, ac], axis=-1)
    return jnp.cos(ang), jnp.sin(ang)


def kernel(x, c, ctx, c_ctx, mod_w, mod_b, pre_mix_g, post_mix_g, pre_ffn_g, post_ffn_g, ev_in_w, mla_q_norm, mla_kv_norm, mla_w_uq, mla_w_ukv, gla_w_gate_fw, gla_b_gate_fw, gla_w_gate_bw, gla_b_gate_bw, gla_o_norm, ev_out_w, od_in_w, conf_dw_w, conf_dw_b, conf_ln_g, conf_ln_b, od_out_w, ffn_in_w, ffn_dw_w, ffn_dw_b, ffn_out_w):
    bsz, n, d = x.shape
    n_ctx = ctx.shape[1]
    depth = mod_w.shape[0]
    last_ctx_reader = ((depth - 1) // 2) * 2

    rows = -(-(bsz + 1) // 8) * 8
    cvec = jnp.concatenate([c, c_ctx[None], jnp.zeros((rows - bsz - 1, d), F32)], axis=0)
    mod = _modulation(cvec, mod_w, mod_b)

    cos, sin = _rope_tables(n)
    cosq, sinq = jnp.tile(cos, (1, MLA_HEADS)), jnp.tile(sin, (1, MLA_HEADS))
    ones_q = jnp.ones((n_ctx, MLA_HEADS * MLA_ROPE), F32)
    zeros_q = jnp.zeros((n_ctx, MLA_HEADS * MLA_ROPE), F32)

    def flat(t):
        return t.reshape(1, bsz * n_ctx, t.shape[-1])

    def unflat(t):
        return t.reshape(bsz, n_ctx, t.shape[-1])

    x_lat, x_ctx = x, ctx
    for l in range(depth):
        need_ctx = l < last_ctx_reader
        use_ctx = need_ctx or (l % 2 == 0)
        i = l // 2
        m_lat = [t.reshape(bsz, 1, d) for t in jnp.split(mod[l, :bsz], 6, axis=-1)]
        m_ctx = [t.reshape(1, 1, d) for t in jnp.split(mod[l, bsz:bsz + 1], 6, axis=-1)]
        sh1, sc1, g1, sh2, sc2, g2 = m_lat
        csh1, csc1, cg1, csh2, csc2, cg2 = m_ctx

        if l % 2 == 0:
            w_p, w_q, w_kv, w_gate, b_gate = _even_weights(
                ev_in_w[i], mla_w_uq[i], mla_w_ukv[i], gla_w_gate_fw[i], gla_b_gate_fw[i],
                gla_w_gate_bw[i], gla_b_gate_bw[i])
            w_o = ev_out_w[i].astype(BF16)
            k1 = MLA_HEADS * MLA_V
            p_lat = _nmm(x_lat, 0, pre_mix_g[l], w_p, sc1, sh1, name="even_in_lat")
            p_ctx = unflat(_nmm(flat(x_ctx), 0, pre_mix_g[l], w_p, csc1, csh1, name="even_in_ctx"))
            q_lat = _nmm(p_lat, P_QC, mla_q_norm[i], w_q, name="mla_q_lat")
            kv_lat = _nmm(p_lat, P_KVC, mla_kv_norm[i], w_kv, name="mla_kv_lat")
            kv_ctx = unflat(_nmm(flat(p_ctx), P_KVC, mla_kv_norm[i], w_kv, name="mla_kv_ctx"))
            a_lat = _mla(q_lat, cosq, sinq, kv_ctx, p_ctx, kv_lat, p_lat, cos, sin, name="mla_lat")
            g_ctx, g_lat = _gla(p_ctx, p_lat, w_gate, b_gate, gla_o_norm[i], need_ctx)
            x_lat = _mnr(a_lat, g_lat, w_o[:k1], w_o[k1:], x_lat, g1, post_mix_g[l], name="even_out_lat")
            if need_ctx:
                q_ctx = unflat(_nmm(flat(p_ctx), P_QC, mla_q_norm[i], w_q, name="mla_q_ctx"))
                a_ctx = _mla(q_ctx, ones_q, zeros_q, kv_ctx, p_ctx, name="mla_ctx")
                x_ctx = unflat(_mnr(flat(a_ctx), flat(g_ctx), w_o[:k1], w_o[k1:], flat(x_ctx), cg1,
                                    post_mix_g[l], name="even_out_ctx"))
        else:
            w_p = od_in_w[i].astype(BF16)
            w_o = od_out_w[i].astype(BF16)
            w_dw = jnp.concatenate([conf_dw_w[i], jnp.zeros((1, CONV_CH), F32)], axis=0)
            streams = [(x_lat, sc1, sh1, g1, "lat")]
            if need_ctx:
                streams.append((x_ctx, csc1, csh1, cg1, "ctx"))
            outs = []
            for xs, sc, sh, gate, tag in streams:
                per_batch = sc.shape[0] > 1
                xin = xs if per_batch else flat(xs)
                p = _nmm(xin, 0, pre_mix_g[l], w_p, sc, sh, name="odd_in_" + tag)
                p = p if per_batch else unflat(p)
                u = _conformer(p, w_dw, conf_dw_b[i], conf_ln_g[i], conf_ln_b[i], name="conformer_" + tag)
                fm = _fnet(p, name="fnet_" + tag)
                if not per_batch:
                    u, fm = flat(u), flat(fm)
                y = _mnr(u, fm, w_o[:CONV_CH], w_o[CONV_CH:], xin, gate, post_mix_g[l], name="odd_out_" + tag)
                outs.append(y if per_batch else unflat(y))
            x_lat = outs[0]
            if need_ctx:
                x_ctx = outs[1]

        wg, wv, cwb, wo = _ffn_weights(ffn_in_w[l], ffn_dw_w[l], ffn_dw_b[l], ffn_out_w[l])
        x_lat = _ffn(x_lat, pre_ffn_g[l], sc2, sh2, g2, post_ffn_g[l], wg, wv, cwb, wo, n, name="ffn_lat")
        if need_ctx:
            x_ctx = unflat(_ffn(flat(x_ctx), pre_ffn_g[l], csc2, csh2, cg2, post_ffn_g[l], wg, wv, cwb, wo,
                                n_ctx, name="ffn_ctx"))
    return x_lat
```

```python
import functools
import math

import numpy as np
import jax
import jax.numpy as jnp
from jax import lax
from jax.experimental import pallas as pl
from jax.experimental.pallas import tpu as pltpu

F32 = jnp.float32
BF16 = jnp.bfloat16

GRID_W = 64
ROPE_BASE = 10000.0
MLA_HEADS = 8
MLA_Q_RANK = 256
MLA_KV_RANK = 128
MLA_NOPE = 64
MLA_ROPE = 32
MLA_V = 64
MLA_SCALE = (MLA_NOPE + MLA_ROPE) ** -0.5
GLA_HEADS = 4
GLA_DK = 64
GLA_DV = 128
GLA_GATE_RANK = 16
GLA_GATE_NORM = 16.0
GLA_CHUNK = 64
CONV_CH = 768
CONV_W = 31
FNET_GROUPS = 4
FNET_GROUP_CH = 64
FNET_CH = FNET_GROUPS * FNET_GROUP_CH
FFN_CHUNK = 256
EPS = 1e-6

P_QC, P_GQ, P_GK = 0, 256, 512
P_KVC, P_TAIL = 768, 896
P_GV, P_GR = 1024, 1536
P_WIDTH = 2048
T_KR, T_KRROT, T_GF, T_GB = 0, 32, 64, 80

NT_DIMS = (((1,), (1,)), ((), ()))
TN_DIMS = (((0,), (0,)), ((), ()))

VMEM_LIMIT = 56 * 1024 * 1024


def _params(*sem):
    return pltpu.CompilerParams(dimension_semantics=sem, vmem_limit_bytes=VMEM_LIMIT)


def _rms(xf, gain):
    return xf * lax.rsqrt(jnp.mean(xf * xf, axis=-1, keepdims=True) + EPS) * gain


def _sigmoid(x):
    return 1.0 / (1.0 + jnp.exp(-x))


def _dot(a, b):
    return jnp.dot(a, b, preferred_element_type=F32)


def _tile(n, want):
    t = min(n, want)
    assert n % t == 0, (n, want)
    return t


def _mod_kernel(c_ref, w_ref, b_ref, o_ref):
    c = c_ref[...]
    s = c * _sigmoid(c)
    o_ref[0] = _dot(s.astype(BF16), w_ref[0].astype(BF16)) + b_ref[0]


def _modulation(cvec, mod_w, mod_b):
    depth, d, n6 = mod_w.shape
    rows = cvec.shape[0]
    tn = _tile(n6, 1536)
    return pl.pallas_call(
        _mod_kernel,
        grid=(depth, n6 // tn),
        in_specs=[
            pl.BlockSpec((rows, d), lambda l, j: (0, 0)),
            pl.BlockSpec((1, d, tn), lambda l, j: (l, 0, j)),
            pl.BlockSpec((1, 1, tn), lambda l, j: (l, 0, j)),
        ],
        out_specs=pl.BlockSpec((1, rows, tn), lambda l, j: (l, 0, j)),
        out_shape=jax.ShapeDtypeStruct((depth, rows, n6), F32),
        compiler_params=_params("parallel", "parallel"),
        name="modulation",
    )(cvec, mod_w, mod_b.reshape(depth, 1, n6))


def _nmm_kernel(*refs, modulate, col_w):
    if modulate:
        x_ref, g_ref, sc_ref, sh_ref, w_ref, o_ref = refs
    else:
        x_ref, g_ref, w_ref, o_ref = refs
    h = _rms(x_ref[0].astype(F32), g_ref[...])
    if modulate:
        h = h * (1.0 + sc_ref[0]) + sh_ref[0]
    hb = h.astype(BF16)
    for j in range(o_ref.shape[2] // col_w):
        cols = slice(j * col_w, (j + 1) * col_w)
        o_ref[0, :, cols] = _dot(hb, w_ref[:, cols]).astype(o_ref.dtype)


def _nmm(x, col_off, gain, w, sc=None, sh=None, tm=512, name="nmm"):
    bsz, n, _ = x.shape
    k, nout = w.shape
    assert col_off % k == 0
    cb = col_off // k
    tm = _tile(n, tm)
    col_w = next(cw for cw in (512, 256, 128) if nout % cw == 0)
    modulate = sc is not None
    in_specs = [pl.BlockSpec((1, tm, k), lambda b, i: (b, i, cb)),
                pl.BlockSpec((1, k), lambda b, i: (0, 0))]
    args = [x, gain.reshape(1, k)]
    if modulate:
        per_batch = sc.shape[0] > 1
        mod_spec = pl.BlockSpec((1, 1, k), (lambda b, i: (b, 0, 0)) if per_batch else (lambda b, i: (0, 0, 0)))
        in_specs += [mod_spec, mod_spec]
        args += [sc, sh]
    in_specs.append(pl.BlockSpec((k, nout), lambda b, i: (0, 0)))
    args.append(w)
    return pl.pallas_call(
        functools.partial(_nmm_kernel, modulate=modulate, col_w=col_w),
        grid=(bsz, n // tm),
        in_specs=in_specs,
        out_specs=pl.BlockSpec((1, tm, nout), lambda b, i: (b, i, 0)),
        out_shape=jax.ShapeDtypeStruct((bsz, n, nout), BF16),
        compiler_params=_params("parallel", "parallel"),
        name=name,
    )(*args)


def _mnr_kernel(a1_ref, a2_ref, w1_ref, w2_ref, x_ref, gate_ref, pg_ref, o_ref):
    y = _dot(a1_ref[0], w1_ref[...]) + _dot(a2_ref[0], w2_ref[...])
    o_ref[0] = x_ref[0] + gate_ref[0] * _rms(y, pg_ref[...])


def _mnr(a1, a2, w1, w2, x, gate, post_g, tm=512, name="mnr"):
    bsz, n, d = x.shape
    k1, k2 = a1.shape[2], a2.shape[2]
    tm = _tile(n, tm)
    per_batch = gate.shape[0] > 1
    return pl.pallas_call(
        _mnr_kernel,
        grid=(bsz, n // tm),
        in_specs=[
            pl.BlockSpec((1, tm, k1), lambda b, i: (b, i, 0)),
            pl.BlockSpec((1, tm, k2), lambda b, i: (b, i, 0)),
            pl.BlockSpec((k1, d), lambda b, i: (0, 0)),
            pl.BlockSpec((k2, d), lambda b, i: (0, 0)),
            pl.BlockSpec((1, tm, d), lambda b, i: (b, i, 0)),
            pl.BlockSpec((1, 1, d), (lambda b, i: (b, 0, 0)) if per_batch else (lambda b, i: (0, 0, 0))),
            pl.BlockSpec((1, d), lambda b, i: (0, 0)),
        ],
        out_specs=pl.BlockSpec((1, tm, d), lambda b, i: (b, i, 0)),
        out_shape=jax.ShapeDtypeStruct((bsz, n, d), F32),
        compiler_params=_params("parallel", "parallel"),
        name=name,
    )(a1, a2, w1, w2, x, gate, post_g.reshape(1, d))


def _ffn_kernel(x_ref, xp_ref, xn_ref, g_ref, sc_ref, sh_ref, gate_ref, pg_ref,
                wg_ref, wv_ref, cw_ref, wo_ref, o_ref, xh_ref, acc_ref, g0_ref, g1_ref, v0_ref, v1_ref, *, seq_len):
    tm = x_ref.shape[1]
    n_chunks = wg_ref.shape[0]
    i = pl.program_id(1)
    gain = g_ref[...]
    scale = 1.0 + sc_ref[0]
    shift = sh_ref[0]

    def modulated(rows):
        return (_rms(rows, gain) * scale + shift).astype(BF16)

    xh_ref[0:tm] = modulated(x_ref[0])
    xh_ref[tm:tm + 16] = modulated(jnp.concatenate([xp_ref[0], xn_ref[0]], axis=0))

    local = lax.broadcasted_iota(jnp.int32, (tm, 1), 0)
    pos = (i * tm + local) % seq_len
    seq_first = pos == 0
    seq_last = pos == seq_len - 1
    tile_first = local == 0
    tile_last = local == tm - 1
    acc_ref[...] = jnp.zeros_like(acc_ref)
    slots = ((g0_ref, v0_ref), (g1_ref, v1_ref))

    def project(c, slot):
        gs_ref, vs_ref = slots[slot]
        gs_ref[...] = _dot(xh_ref[...], wg_ref[c])
        vs_ref[...] = _dot(xh_ref[0:tm], wv_ref[c])

    def mix(c, slot):
        gs_ref, vs_ref = slots[slot]
        gm = gs_ref[0:tm]
        g_prev = jnp.where(tile_first, gs_ref[tm + 7:tm + 8], pltpu.roll(gm, 1, axis=0))
        g_prev = jnp.where(seq_first, 0.0, g_prev)
        g_next = jnp.where(tile_last, gs_ref[tm + 8:tm + 9], pltpu.roll(gm, tm - 1, axis=0))
        g_next = jnp.where(seq_last, 0.0, g_next)
        cw = cw_ref[c]
        y = g_prev * cw[0:1] + gm * cw[1:2] + g_next * cw[2:3] + cw[3:4]
        u = 0.5 * y * (1.0 + lax.erf(y * (1.0 / math.sqrt(2.0)))) * vs_ref[...]
        acc_ref[...] += _dot(u.astype(BF16), wo_ref[c])

    project(0, 0)

    def body(j, carry):
        c = 2 * j
        project(c + 1, 1)
        mix(c, 0)
        project(c + 2, 0)
        mix(c + 1, 1)
        return carry

    lax.fori_loop(0, (n_chunks - 1) // 2, body, 0)
    if n_chunks % 2 == 0:
        project(n_chunks - 1, 1)
        mix(n_chunks - 2, 0)
        mix(n_chunks - 1, 1)
    else:
        mix(n_chunks - 1, 0)
    o_ref[0] = x_ref[0] + gate_ref[0] * _rms(acc_ref[...], pg_ref[...])


def _ffn(x, pre_g, sc, sh, gate, post_g, wg, wv, cwb, wo, seq_len, tm=512, name="ffn"):
    bsz, n, d = x.shape
    nf, _, fc = wg.shape
    tm = _tile(n, tm)
    nb8 = n // 8
    per_batch = sc.shape[0] > 1
    mod_spec = pl.BlockSpec((1, 1, d), (lambda b, i: (b, 0, 0)) if per_batch else (lambda b, i: (0, 0, 0)))
    vec_spec = pl.BlockSpec((1, d), lambda b, i: (0, 0))
    return pl.pallas_call(
        functools.partial(_ffn_kernel, seq_len=seq_len),
        grid=(bsz, n // tm),
        in_specs=[
            pl.BlockSpec((1, tm, d), lambda b, i: (b, i, 0)),
            pl.BlockSpec((1, 8, d), lambda b, i: (b, jnp.maximum(i * (tm // 8) - 1, 0), 0)),
            pl.BlockSpec((1, 8, d), lambda b, i: (b, jnp.minimum((i + 1) * (tm // 8), nb8 - 1), 0)),
            vec_spec, mod_spec, mod_spec, mod_spec, vec_spec,
            pl.BlockSpec((nf, d, fc), lambda b, i: (0, 0, 0), pipeline_mode=pl.Buffered(1)),
            pl.BlockSpec((nf, d, fc), lambda b, i: (0, 0, 0), pipeline_mode=pl.Buffered(1)),
            pl.BlockSpec((nf, 8, fc), lambda b, i: (0, 0, 0), pipeline_mode=pl.Buffered(1)),
            pl.BlockSpec((nf, fc, d), lambda b, i: (0, 0, 0), pipeline_mode=pl.Buffered(1)),
        ],
        out_specs=pl.BlockSpec((1, tm, d), lambda b, i: (b, i, 0)),
        out_shape=jax.ShapeDtypeStruct((bsz, n, d), F32),
        scratch_shapes=[pltpu.VMEM((tm + 16, d), BF16), pltpu.VMEM((tm, d), F32),
                        pltpu.VMEM((tm + 16, fc), F32), pltpu.VMEM((tm + 16, fc), F32),
                        pltpu.VMEM((tm, fc), F32), pltpu.VMEM((tm, fc), F32)],
        compiler_params=_params("parallel", "parallel"),
        name=name,
    )(x, x, x, pre_g.reshape(1, d), sc, sh, gate, post_g.reshape(1, d), wg, wv, cwb, wo)


def _mla_kernel(*refs, n_ctx, n_lat):
    if n_lat:
        (q_ref, cq_ref, sq_ref, kvc_ref, tc_ref, kvl_ref, tl_ref, ck_ref, sk_ref,
         o_ref, kc_ref, vt_ref, ot_ref, s0_ref, s1_ref) = refs
    else:
        q_ref, cq_ref, sq_ref, kvc_ref, tc_ref, o_ref, kc_ref, vt_ref, ot_ref, s0_ref, s1_ref = refs
    hv = MLA_HEADS * MLA_NOPE
    vrows = vt_ref.shape[1]

    @pl.when(pl.program_id(1) == 0)
    def _build_keys():
        r = lax.broadcasted_iota(jnp.int32, (vrows, MLA_V), 0)
        c = lax.broadcasted_iota(jnp.int32, (vrows, MLA_V), 1)
        sel = (r == c).astype(BF16)

        def fill(kv_ref, t_ref, off, n, roped):
            tail = t_ref[0].astype(F32)
            kr = tail[:, T_KR:T_KR + MLA_ROPE]
            if roped:
                kr = kr * ck_ref[...] + tail[:, T_KRROT:T_KRROT + MLA_ROPE] * sk_ref[...]
            kr = kr.astype(BF16)
            is_v = lax.broadcasted_iota(jnp.int32, (vrows, n), 0) < MLA_V
            for h in range(MLA_HEADS):
                kc_ref[h, off:off + n, 0:MLA_NOPE] = kv_ref[0, :, h * MLA_NOPE:(h + 1) * MLA_NOPE]
                kc_ref[h, off:off + n, MLA_NOPE:MLA_NOPE + MLA_ROPE] = kr
                kc_ref[h, off:off + n, MLA_NOPE + MLA_ROPE:128] = jnp.zeros((n, 128 - MLA_NOPE - MLA_ROPE), BF16)
                vt = lax.dot_general(sel, kv_ref[0, :, hv + h * MLA_V:hv + (h + 1) * MLA_V], NT_DIMS,
                                     preferred_element_type=F32)
                vt_ref[h, :, off:off + n] = jnp.where(is_v, vt, 1.0).astype(BF16)

        fill(kvc_ref, tc_ref, 0, n_ctx, False)
        if n_lat:
            fill(kvl_ref, tl_ref, n_ctx, n_lat, True)

    tq = q_ref.shape[1]
    hr = MLA_HEADS * MLA_ROPE
    qr = (q_ref[0, :, hv:hv + hr].astype(F32) * cq_ref[...]
          + q_ref[0, :, hv + hr:hv + 2 * hr].astype(F32) * sq_ref[...]).astype(BF16)
    pad = jnp.zeros((tq, 128 - MLA_NOPE - MLA_ROPE), BF16)
    st_refs = (s0_ref, s1_ref)

    def scores(h):
        qh = jnp.concatenate([q_ref[0, :, h * MLA_NOPE:(h + 1) * MLA_NOPE],
                              qr[:, h * MLA_ROPE:(h + 1) * MLA_ROPE], pad], axis=1)
        st_refs[h % 2][...] = lax.dot_general(kc_ref[h], qh, NT_DIMS, preferred_element_type=F32)

    scores(0)
    for h in range(MLA_HEADS):
        if h + 1 < MLA_HEADS:
            scores(h + 1)
        st = st_refs[h % 2][...]
        pt = jnp.exp2(st - jnp.max(st, axis=0, keepdims=True)).astype(BF16)
        oe = _dot(vt_ref[h], pt)
        ot_ref[h * MLA_V:(h + 1) * MLA_V, :] = oe[0:MLA_V] / oe[MLA_V:MLA_V + 1]
    o_ref[0] = ot_ref[...].T.astype(o_ref.dtype)


def _mla(q, cosq, sinq, kv_c, p_c, kv_l=None, p_l=None, cosk=None, sink=None, tq=512, name="mla"):
    bsz, nq, qw = q.shape
    n_ctx = kv_c.shape[1]
    n_lat = 0 if kv_l is None else kv_l.shape[1]
    nk = n_ctx + n_lat
    tq = _tile(nq, tq)
    hr = MLA_HEADS * MLA_ROPE
    tail_blk = P_TAIL // 128
    in_specs = [
        pl.BlockSpec((1, tq, qw), lambda b, i: (b, i, 0)),
        pl.BlockSpec((tq, hr), lambda b, i: (i, 0)),
        pl.BlockSpec((tq, hr), lambda b, i: (i, 0)),
        pl.BlockSpec((1, n_ctx, kv_c.shape[2]), lambda b, i: (b, 0, 0)),
        pl.BlockSpec((1, n_ctx, 128), lambda b, i: (b, 0, tail_blk)),
    ]
    args = [q, cosq, sinq, kv_c, p_c]
    if n_lat:
        in_specs += [
            pl.BlockSpec((1, n_lat, kv_l.shape[2]), lambda b, i: (b, 0, 0)),
            pl.BlockSpec((1, n_lat, 128), lambda b, i: (b, 0, tail_blk)),
            pl.BlockSpec((n_lat, MLA_ROPE), lambda b, i: (0, 0)),
            pl.BlockSpec((n_lat, MLA_ROPE), lambda b, i: (0, 0)),
        ]
        args += [kv_l, p_l, cosk, sink]
    return pl.pallas_call(
        functools.partial(_mla_kernel, n_ctx=n_ctx, n_lat=n_lat),
        grid=(bsz, nq // tq),
        in_specs=in_specs,
        out_specs=pl.BlockSpec((1, tq, MLA_HEADS * MLA_V), lambda b, i: (b, i, 0)),
        out_shape=jax.ShapeDtypeStruct((bsz, nq, MLA_HEADS * MLA_V), BF16),
        scratch_shapes=[pltpu.VMEM((MLA_HEADS, nk, 128), BF16),
                        pltpu.VMEM((MLA_HEADS, MLA_V + 16, nk), BF16),
                        pltpu.VMEM((MLA_HEADS * MLA_V, tq), F32),
                        pltpu.VMEM((nk, tq), F32), pltpu.VMEM((nk, tq), F32)],
        compiler_params=_params("parallel", "arbitrary"),
        name=name,
    )(*args)


def _gla_kernel(*refs, want_ctx):
    (qc_ref, kc_ref, tc_ref, vc_ref, grc_ref, ql_ref, kl_ref, tl_ref, vl_ref, grl_ref,
     wg_ref, bg_ref, on_ref) = refs[:13]
    if want_ctx:
        oc_ref, ol_ref = refs[13:15]
        scratch = refs[15:]
    else:
        oc_ref, ol_ref = None, refs[13]
        scratch = refs[14:]
    lgc_ref, lgl_ref, accc_ref, accl_ref, st_ref = scratch
    cs = GLA_CHUNK
    hk = GLA_HEADS * GLA_DK

    shift = cs.bit_length() - 1
    for t_ref, lg_ref in ((tc_ref, lgc_ref), (tl_ref, lgl_ref)):
        n = t_ref.shape[1]
        tr = _tile(n, 256)
        r = lax.broadcasted_iota(jnp.int32, (tr, tr), 0)
        c = lax.broadcasted_iota(jnp.int32, (tr, tr), 1)
        same_chunk = (r >> shift) == (c >> shift)
        tri_fw = (same_chunk & (r >= c)).astype(F32)
        tri_bw = (same_chunk & (r <= c)).astype(F32)

        def cum_body(i, carry, t_ref=t_ref, lg_ref=lg_ref, tr=tr, tri_fw=tri_fw, tri_bw=tri_bw):
            rows = pl.ds(pl.multiple_of(i * tr, tr), tr)
            z = _dot(t_ref[0, rows, :], wg_ref[...]) + bg_ref[...]
            lg = (jnp.minimum(z, 0.0) - jnp.log(1.0 + jnp.exp(-jnp.abs(z)))) * (1.0 / GLA_GATE_NORM)
            lg_ref[rows, 0:hk] = jnp.dot(tri_fw, lg[:, 0:hk], precision=lax.Precision.HIGHEST,
                                         preferred_element_type=F32)
            lg_ref[rows, hk:2 * hk] = jnp.dot(tri_bw, lg[:, hk:2 * hk], precision=lax.Precision.HIGHEST,
                                              preferred_element_type=F32)
            return carry

        lax.fori_loop(0, n // tr, cum_body, 0)

    row = lax.broadcasted_iota(jnp.int32, (cs, cs), 0)
    col = lax.broadcasted_iota(jnp.int32, (cs, cs), 1)
    masks = (row >= col, row <= col)
    st_ref[...] = jnp.zeros_like(st_ref)

    def chunk(q_ref, k_ref, v_ref, lg_ref, acc_ref, d, r0):
        rows = pl.ds(pl.multiple_of(r0, cs), cs)
        cum = lg_ref[rows, d * hk:(d + 1) * hk]
        tot = cum[cs - 1:cs] if d == 0 else cum[0:1]
        ref = 0.5 * tot
        q = q_ref[0, rows, :].astype(F32)
        k = k_ref[0, rows, :].astype(F32)
        v = v_ref[0, rows, :]
        qt = (q * jnp.exp(cum - ref)).astype(BF16)
        kt = (k * jnp.exp(ref - cum)).astype(BF16)
        qs = (q * jnp.exp(cum)).astype(BF16)
        kd = (k * jnp.exp(tot - cum)).astype(BF16)
        dec = jnp.exp(tot)
        for h in range(GLA_HEADS):
            ks = slice(h * GLA_DK, (h + 1) * GLA_DK)
            vs = slice(h * GLA_DV, (h + 1) * GLA_DV)
            a = lax.dot_general(qt[:, ks], kt[:, ks], NT_DIMS, preferred_element_type=F32)
            a = jnp.where(masks[d], a, 0.0).astype(BF16)
            st = st_ref[d, h]
            acc_ref[d, rows, vs] = _dot(a, v[:, vs]) + lax.dot_general(
                qs[:, ks], st.astype(BF16), NT_DIMS, preferred_element_type=F32)
            st_ref[d, h] = st * dec[:, ks] + lax.dot_general(
                v[:, vs], kd[:, ks], TN_DIMS, preferred_element_type=F32)

    def scan(q_ref, k_ref, v_ref, lg_ref, acc_ref):
        n_chunks = q_ref.shape[1] // cs

        def body(i, carry):
            chunk(q_ref, k_ref, v_ref, lg_ref, acc_ref, 0, i * cs)
            chunk(q_ref, k_ref, v_ref, lg_ref, acc_ref, 1, (n_chunks - 1 - i) * cs)
            return carry

        lax.fori_loop(0, n_chunks, body, 0)

    def finish(acc_ref, gr_ref, o_ref):
        n = acc_ref.shape[1]
        tr = _tile(n, 256)

        def body(i, carry):
            rows = pl.ds(pl.multiple_of(i * tr, tr), tr)
            gr = gr_ref[0, rows, :].astype(F32)
            gate = gr * _sigmoid(gr)
            for h in range(GLA_HEADS):
                vs = slice(h * GLA_DV, (h + 1) * GLA_DV)
                o = acc_ref[0, rows, vs] + acc_ref[1, rows, vs]
                o_ref[0, rows, vs] = (_rms(o, on_ref[...]) * gate[:, vs]).astype(o_ref.dtype)
            return carry

        lax.fori_loop(0, n // tr, body, 0)

    scan(qc_ref, kc_ref, vc_ref, lgc_ref, accc_ref)
    scan(ql_ref, kl_ref, vl_ref, lgl_ref, accl_ref)
    if want_ctx:
        finish(accc_ref, grc_ref, oc_ref)
    finish(accl_ref, grl_ref, ol_ref)


def _gla(p_c, p_l, w_gate, b_gate, o_norm, want_ctx, name="gla"):
    bsz, n_ctx, _ = p_c.shape
    n_lat = p_l.shape[1]
    hk, hv = GLA_HEADS * GLA_DK, GLA_HEADS * GLA_DV

    def specs(n):
        return [
            pl.BlockSpec((1, n, hk), lambda b: (b, 0, P_GQ // hk)),
            pl.BlockSpec((1, n, hk), lambda b: (b, 0, P_GK // hk)),
            pl.BlockSpec((1, n, 128), lambda b: (b, 0, P_TAIL // 128)),
            pl.BlockSpec((1, n, hv), lambda b: (b, 0, P_GV // hv)),
            pl.BlockSpec((1, n, hv), lambda b: (b, 0, P_GR // hv)),
        ]

    in_specs = specs(n_ctx) + specs(n_lat) + [
        pl.BlockSpec((128, 2 * hk), lambda b: (0, 0)),
        pl.BlockSpec((1, 2 * hk), lambda b: (0, 0)),
        pl.BlockSpec((1, GLA_DV), lambda b: (0, 0)),
    ]
    out_specs = [pl.BlockSpec((1, n_lat, hv), lambda b: (b, 0, 0))]
    out_shape = [jax.ShapeDtypeStruct((bsz, n_lat, hv), BF16)]
    if want_ctx:
        out_specs.insert(0, pl.BlockSpec((1, n_ctx, hv), lambda b: (b, 0, 0)))
        out_shape.insert(0, jax.ShapeDtypeStruct((bsz, n_ctx, hv), BF16))
    outs = pl.pallas_call(
        functools.partial(_gla_kernel, want_ctx=want_ctx),
        grid=(bsz,),
        in_specs=in_specs,
        out_specs=out_specs,
        out_shape=out_shape,
        scratch_shapes=[
            pltpu.VMEM((n_ctx, 2 * hk), F32), pltpu.VMEM((n_lat, 2 * hk), F32),
            pltpu.VMEM((2, n_ctx, hv), F32), pltpu.VMEM((2, n_lat, hv), F32),
            pltpu.VMEM((2, GLA_HEADS, GLA_DV, GLA_DK), F32),
        ],
        compiler_params=_params("parallel"),
        name=name,
    )(p_c, p_c, p_c, p_c, p_c, p_l, p_l, p_l, p_l, p_l, w_gate, b_gate, o_norm.reshape(1, GLA_DV))
    return (outs[0], outs[1]) if want_ctx else (None, outs[0])


def _conf_kernel(a_ref, gt_ref, w_ref, b_ref, lg_ref, lb_ref, o_ref, u_ref):
    n = a_ref.shape[1]
    pad = (CONV_W - 1) // 2
    lead = 16
    u_ref[0:lead] = jnp.zeros((lead, CONV_CH), F32)
    u_ref[lead + n:lead + n + lead] = jnp.zeros((lead, CONV_CH), F32)
    u_ref[lead:lead + n] = a_ref[0].astype(F32) * _sigmoid(gt_ref[0].astype(F32))
    tr = _tile(n, 256)

    def body(i, carry):
        r0 = pl.multiple_of(i * tr, tr)
        ext = u_ref[pl.ds(r0, tr + 2 * lead), :]
        y = jnp.zeros((tr, CONV_CH), F32) + b_ref[...]
        for k in range(CONV_W):
            off = lead - pad + k
            y = y + pltpu.roll(ext, tr + 2 * lead - off, axis=0)[0:tr] * w_ref[k:k + 1, :]
        mu = jnp.mean(y, axis=-1, keepdims=True)
        yc = y - mu
        z = yc * lax.rsqrt(jnp.mean(yc * yc, axis=-1, keepdims=True) + EPS) * lg_ref[...] + lb_ref[...]
        o_ref[0, pl.ds(r0, tr), :] = (z * _sigmoid(z)).astype(o_ref.dtype)
        return carry

    lax.fori_loop(0, n // tr, body, 0)


def _conformer(p, w, b, ln_g, ln_b, name="conformer"):
    bsz, n, _ = p.shape
    vec = pl.BlockSpec((1, CONV_CH), lambda i: (0, 0))
    return pl.pallas_call(
        _conf_kernel,
        grid=(bsz,),
        in_specs=[
            pl.BlockSpec((1, n, CONV_CH), lambda i: (i, 0, 0)),
            pl.BlockSpec((1, n, CONV_CH), lambda i: (i, 0, 1)),
            pl.BlockSpec((CONV_W + 1, CONV_CH), lambda i: (0, 0)),
            vec, vec, vec,
        ],
        out_specs=pl.BlockSpec((1, n, CONV_CH), lambda i: (i, 0, 0)),
        out_shape=jax.ShapeDtypeStruct((bsz, n, CONV_CH), BF16),
        scratch_shapes=[pltpu.VMEM((n + 32, CONV_CH), F32)],
        compiler_params=_params("parallel"),
        name=name,
    )(p, p, w, b.reshape(1, CONV_CH), ln_g.reshape(1, CONV_CH), ln_b.reshape(1, CONV_CH))


def _fnet_kernel(f_ref, cs_ref, wd_ref, o_ref, z_ref, *, scale):
    n = f_ref.shape[1]

    @pl.when(pl.program_id(1) == 0)
    def _channel_dft():
        zz = _dot(f_ref[0], cs_ref[...])
        z_ref[0:n] = zz[:, 0:FNET_CH].astype(BF16)
        z_ref[n:2 * n] = zz[:, FNET_CH:2 * FNET_CH].astype(BF16)

    o_ref[0] = (_dot(wd_ref[...], z_ref[...]) * scale).astype(o_ref.dtype)


def _dft_tables(n):
    k = np.arange(n, dtype=np.float64)
    ang = 2.0 * np.pi * np.outer(k, k) / n
    return np.cos(ang), np.sin(ang)


def _fnet(p, name="fnet"):
    bsz, n, width = p.shape
    cn, sn = _dft_tables(n)
    wd = jnp.asarray(np.concatenate([cn, -sn], axis=1), BF16)
    cg, sg = _dft_tables(FNET_GROUP_CH)
    eye = np.eye(FNET_GROUPS)
    cs = jnp.asarray(np.concatenate([np.kron(eye, cg), np.kron(eye, sg)], axis=1), BF16)
    tm = _tile(n, 512)
    scale = 1.0 / math.sqrt(n * FNET_GROUP_CH)
    return pl.pallas_call(
        functools.partial(_fnet_kernel, scale=scale),
        grid=(bsz, n // tm),
        in_specs=[
            pl.BlockSpec((1, n, FNET_CH), lambda b, i: (b, 0, (width - FNET_CH) // FNET_CH)),
            pl.BlockSpec((FNET_CH, 2 * FNET_CH), lambda b, i: (0, 0)),
            pl.BlockSpec((tm, 2 * n), lambda b, i: (i, 0)),
        ],
        out_specs=pl.BlockSpec((1, tm, FNET_CH), lambda b, i: (b, i, 0)),
        out_shape=jax.ShapeDtypeStruct((bsz, n, FNET_CH), BF16),
        scratch_shapes=[pltpu.VMEM((2 * n, FNET_CH), BF16)],
        compiler_params=_params("parallel", "arbitrary"),
        name=name,
    )(p, cs, wd)


def _rot_cols(w):
    a, b, c, d = jnp.split(w, 4, axis=-1)
    return jnp.concatenate([-b, a, -d, c], axis=-1)


def _even_weights(w_in, w_uq, w_ukv, w_gfw, b_gfw, w_gbw, b_gbw):
    d = w_in.shape[0]
    sizes = [MLA_Q_RANK, MLA_KV_RANK, MLA_ROPE, GLA_HEADS * GLA_DK, GLA_HEADS * GLA_DK, GLA_HEADS * GLA_DV,
             GLA_GATE_RANK, GLA_GATE_RANK, GLA_HEADS * GLA_DV]
    qc, kvc, kr, gq, gk, gv, glf, glb, gr = jnp.split(w_in, np.cumsum(sizes)[:-1].tolist(), axis=1)
    tail_pad = jnp.zeros((d, 128 - 2 * MLA_ROPE - 2 * GLA_GATE_RANK), w_in.dtype)
    w_p = jnp.concatenate([qc, gq * (GLA_DK ** -0.5), gk, kvc, kr, _rot_cols(kr), glf, glb, tail_pad, gv, gr],
                          axis=1).astype(BF16)
    uq = w_uq.reshape(MLA_Q_RANK, MLA_HEADS, MLA_NOPE + MLA_ROPE) * (MLA_SCALE * math.log2(math.e))
    qn = uq[:, :, :MLA_NOPE].reshape(MLA_Q_RANK, -1)
    qr = uq[:, :, MLA_NOPE:]
    w_q = jnp.concatenate([qn, qr.reshape(MLA_Q_RANK, -1), _rot_cols(qr).reshape(MLA_Q_RANK, -1)],
                          axis=1).astype(BF16)
    ukv = w_ukv.reshape(MLA_KV_RANK, MLA_HEADS, MLA_NOPE + MLA_V)
    w_kv = jnp.concatenate([ukv[:, :, :MLA_NOPE].reshape(MLA_KV_RANK, -1),
                            ukv[:, :, MLA_NOPE:].reshape(MLA_KV_RANK, -1)], axis=1).astype(BF16)
    hk = GLA_HEADS * GLA_DK
    w_gate = jnp.zeros((128, 2 * hk), F32)
    w_gate = w_gate.at[T_GF:T_GF + GLA_GATE_RANK, :hk].set(w_gfw)
    w_gate = w_gate.at[T_GB:T_GB + GLA_GATE_RANK, hk:].set(w_gbw).astype(BF16)
    b_gate = jnp.concatenate([b_gfw, b_gbw]).reshape(1, 2 * hk)
    return w_p, w_q, w_kv, w_gate, b_gate


def _ffn_weights(w_in, dw_w, dw_b, w_out):
    d, f2 = w_in.shape
    f = f2 // 2
    nf = f // FFN_CHUNK
    wg = w_in[:, :f].reshape(d, nf, FFN_CHUNK).transpose(1, 0, 2).astype(BF16)
    wv = w_in[:, f:].reshape(d, nf, FFN_CHUNK).transpose(1, 0, 2).astype(BF16)
    taps = jnp.concatenate([dw_w, dw_b[None], jnp.zeros((8 - dw_w.shape[0] - 1, f), F32)], axis=0)
    cwb = taps.reshape(8, nf, FFN_CHUNK).transpose(1, 0, 2)
    wo = w_out.reshape(nf, FFN_CHUNK, d).astype(BF16)
    return wg, wv, cwb, wo


def _rope_tables(n):
    rows = n // GRID_W
    row = jnp.repeat(jnp.arange(rows), GRID_W).astype(F32)
    col = jnp.tile(jnp.arange(GRID_W), rows).astype(F32)
    half = MLA_ROPE // 2
    inv = ROPE_BASE ** (-jnp.arange(0, half, 2, dtype=F32) / half)
    ar = row[:, None] * inv
    ac = col[:, None] * inv
    ang = jnp.concatenate([ar, ar, ac, ac], axis=-1)
    return jnp.cos(ang), jnp.sin(ang)


def kernel(x, c, ctx, c_ctx, mod_w, mod_b, pre_mix_g, post_mix_g, pre_ffn_g, post_ffn_g, ev_in_w, mla_q_norm, mla_kv_norm, mla_w_uq, mla_w_ukv, gla_w_gate_fw, gla_b_gate_fw, gla_w_gate_bw, gla_b_gate_bw, gla_o_norm, ev_out_w, od_in_w, conf_dw_w, conf_dw_b, conf_ln_g, conf_ln_b, od_out_w, ffn_in_w, ffn_dw_w, ffn_dw_b, ffn_out_w):
    bsz, n, d = x.shape
    n_ctx = ctx.shape[1]
    depth = mod_w.shape[0]
    last_ctx_reader = ((depth - 1) // 2) * 2

    rows = -(-(bsz + 1) // 8) * 8
    cvec = jnp.concatenate([c, c_ctx[None], jnp.zeros((rows - bsz - 1, d), F32)], axis=0)
    mod = _modulation(cvec, mod_w, mod_b)

    cos, sin = _rope_tables(n)
    cosq, sinq = jnp.tile(cos, (1, MLA_HEADS)), jnp.tile(sin, (1, MLA_HEADS))
    ones_q = jnp.ones((n_ctx, MLA_HEADS * MLA_ROPE), F32)
    zeros_q = jnp.zeros((n_ctx, MLA_HEADS * MLA_ROPE), F32)

    def flat(t):
        return t.reshape(1, bsz * n_ctx, t.shape[-1])

    def unflat(t):
        return t.reshape(bsz, n_ctx, t.shape[-1])

    x_lat, x_ctx = x, ctx
    for l in range(depth):
        need_ctx = l < last_ctx_reader
        use_ctx = need_ctx or (l % 2 == 0)
        i = l // 2
        m_lat = [t.reshape(bsz, 1, d) for t in jnp.split(mod[l, :bsz], 6, axis=-1)]
        m_ctx = [t.reshape(1, 1, d) for t in jnp.split(mod[l, bsz:bsz + 1], 6, axis=-1)]
        sh1, sc1, g1, sh2, sc2, g2 = m_lat
        csh1, csc1, cg1, csh2, csc2, cg2 = m_ctx

        if l % 2 == 0:
            w_p, w_q, w_kv, w_gate, b_gate = _even_weights(
                ev_in_w[i], mla_w_uq[i], mla_w_ukv[i], gla_w_gate_fw[i], gla_b_gate_fw[i],
                gla_w_gate_bw[i], gla_b_gate_bw[i])
            w_o = ev_out_w[i].astype(BF16)
            k1 = MLA_HEADS * MLA_V
            p_lat = _nmm(x_lat, 0, pre_mix_g[l], w_p, sc1, sh1, name="even_in_lat")
            p_ctx = unflat(_nmm(flat(x_ctx), 0, pre_mix_g[l], w_p, csc1, csh1, name="even_in_ctx"))
            q_lat = _nmm(p_lat, P_QC, mla_q_norm[i], w_q, name="mla_q_lat")
            kv_lat = _nmm(p_lat, P_KVC, mla_kv_norm[i], w_kv, name="mla_kv_lat")
            kv_ctx = unflat(_nmm(flat(p_ctx), P_KVC, mla_kv_norm[i], w_kv, name="mla_kv_ctx"))
            a_lat = _mla(q_lat, cosq, sinq, kv_ctx, p_ctx, kv_lat, p_lat, cos, sin, name="mla_lat")
            g_ctx, g_lat = _gla(p_ctx, p_lat, w_gate, b_gate, gla_o_norm[i], need_ctx)
            x_lat = _mnr(a_lat, g_lat, w_o[:k1], w_o[k1:], x_lat, g1, post_mix_g[l], name="even_out_lat")
            if need_ctx:
                q_ctx = unflat(_nmm(flat(p_ctx), P_QC, mla_q_norm[i], w_q, name="mla_q_ctx"))
                a_ctx = _mla(q_ctx, ones_q, zeros_q, kv_ctx, p_ctx, name="mla_ctx")
                x_ctx = unflat(_mnr(flat(a_ctx), flat(g_ctx), w_o[:k1], w_o[k1:], flat(x_ctx), cg1,
                                    post_mix_g[l], name="even_out_ctx"))
        else:
            w_p = od_in_w[i].astype(BF16)
            w_o = od_out_w[i].astype(BF16)
            w_dw = jnp.concatenate([conf_dw_w[i], jnp.zeros((1, CONV_CH), F32)], axis=0)
            streams = [(x_lat, sc1, sh1, g1, "lat")]
            if need_ctx:
                streams.append((x_ctx, csc1, csh1, cg1, "ctx"))
            outs = []
            for xs, sc, sh, gate, tag in streams:
                per_batch = sc.shape[0] > 1
                xin = xs if per_batch else flat(xs)
                p = _nmm(xin, 0, pre_mix_g[l], w_p, sc, sh, name="odd_in_" + tag)
                p = p if per_batch else unflat(p)
                u = _conformer(p, w_dw, conf_dw_b[i], conf_ln_g[i], conf_ln_b[i], name="conformer_" + tag)
                fm = _fnet(p, name="fnet_" + tag)
                if not per_batch:
                    u, fm = flat(u), flat(fm)
                y = _mnr(u, fm, w_o[:CONV_CH], w_o[CONV_CH:], xin, gate, post_mix_g[l], name="odd_out_" + tag)
                outs.append(y if per_batch else unflat(y))
            x_lat = outs[0]
            if need_ctx:
                x_ctx = outs[1]

        wg, wv, cwb, wo = _ffn_weights(ffn_in_w[l], ffn_dw_w[l], ffn_dw_b[l], ffn_out_w[l])
        x_lat = _ffn(x_lat, pre_ffn_g[l], sc2, sh2, g2, post_ffn_g[l], wg, wv, cwb, wo, n, name="ffn_lat")
        if need_ctx:
            x_ctx = unflat(_ffn(flat(x_ctx), pre_ffn_g[l], csc2, csh2, cg2, post_ffn_g[l], wg, wv, cwb, wo,
                                n_ctx, name="ffn_ctx"))
    return x_lat
```

```python
import functools
import math

import numpy as np
import jax
import jax.numpy as jnp
from jax import lax
from jax.experimental import pallas as pl
from jax.experimental.pallas import tpu as pltpu

F32 = jnp.float32
BF16 = jnp.bfloat16

GRID_W = 64
ROPE_BASE = 10000.0
MLA_HEADS = 8
MLA_Q_RANK = 256
MLA_KV_RANK = 128
MLA_NOPE = 64
MLA_ROPE = 32
MLA_V = 64
MLA_SCALE = (MLA_NOPE + MLA_ROPE) ** -0.5
GLA_HEADS = 4
GLA_DK = 64
GLA_DV = 128
GLA_GATE_RANK = 16
GLA_GATE_NORM = 16.0
GLA_CHUNK = 64
CONV_CH = 768
CONV_W = 31
FNET_GROUPS = 4
FNET_GROUP_CH = 64
FNET_CH = FNET_GROUPS * FNET_GROUP_CH
FFN_CHUNK = 256
EPS = 1e-6

P_QC, P_GQ, P_GK = 0, 256, 512
P_KVC, P_TAIL = 768, 896
P_GV, P_GR = 1024, 1536
P_WIDTH = 2048
T_KR, T_KRROT, T_GF, T_GB = 0, 32, 64, 80

NT_DIMS = (((1,), (1,)), ((), ()))
TN_DIMS = (((0,), (0,)), ((), ()))

VMEM_LIMIT = 56 * 1024 * 1024


def _params(*sem):
    return pltpu.CompilerParams(dimension_semantics=sem, vmem_limit_bytes=VMEM_LIMIT)


def _rms(xf, gain):
    return xf * lax.rsqrt(jnp.mean(xf * xf, axis=-1, keepdims=True) + EPS) * gain


def _sigmoid(x):
    return 1.0 / (1.0 + jnp.exp(-x))


def _dot(a, b):
    return jnp.dot(a, b, preferred_element_type=F32)


def _tile(n, want):
    t = min(n, want)
    assert n % t == 0, (n, want)
    return t


def _mod_kernel(c_ref, w_ref, b_ref, o_ref):
    c = c_ref[...]
    s = c * _sigmoid(c)
    o_ref[0] = _dot(s.astype(BF16), w_ref[0].astype(BF16)) + b_ref[0]


def _modulation(cvec, mod_w, mod_b):
    depth, d, n6 = mod_w.shape
    rows = cvec.shape[0]
    tn = _tile(n6, 1536)
    return pl.pallas_call(
        _mod_kernel,
        grid=(depth, n6 // tn),
        in_specs=[
            pl.BlockSpec((rows, d), lambda l, j: (0, 0)),
            pl.BlockSpec((1, d, tn), lambda l, j: (l, 0, j)),
            pl.BlockSpec((1, 1, tn), lambda l, j: (l, 0, j)),
        ],
        out_specs=pl.BlockSpec((1, rows, tn), lambda l, j: (l, 0, j)),
        out_shape=jax.ShapeDtypeStruct((depth, rows, n6), F32),
        compiler_params=_params("parallel", "parallel"),
        name="modulation",
    )(cvec, mod_w, mod_b.reshape(depth, 1, n6))


def _nmm_kernel(x_ref, g_ref, sc_ref, sh_ref, w_ref, o_ref, *, col_w):
    h = _rms(x_ref[0], g_ref[...]) * (1.0 + sc_ref[0]) + sh_ref[0]
    hb = h.astype(BF16)
    for j in range(o_ref.shape[2] // col_w):
        cols = slice(j * col_w, (j + 1) * col_w)
        o_ref[0, :, cols] = _dot(hb, w_ref[:, cols]).astype(o_ref.dtype)


def _nmm(x, gain, w, sc, sh, tm=512, name="nmm"):
    bsz, n, k = x.shape
    nout = w.shape[1]
    tm = _tile(n, tm)
    col_w = next(cw for cw in (512, 256, 128) if nout % cw == 0)
    per_batch = sc.shape[0] > 1
    mod_spec = pl.BlockSpec((1, 1, k), (lambda b, i: (b, 0, 0)) if per_batch else (lambda b, i: (0, 0, 0)))
    return pl.pallas_call(
        functools.partial(_nmm_kernel, col_w=col_w),
        grid=(bsz, n // tm),
        in_specs=[pl.BlockSpec((1, tm, k), lambda b, i: (b, i, 0)),
                  pl.BlockSpec((1, k), lambda b, i: (0, 0)),
                  mod_spec, mod_spec,
                  pl.BlockSpec((k, nout), lambda b, i: (0, 0))],
        out_specs=pl.BlockSpec((1, tm, nout), lambda b, i: (b, i, 0)),
        out_shape=jax.ShapeDtypeStruct((bsz, n, nout), BF16),
        compiler_params=_params("parallel", "parallel"),
        name=name,
    )(x, gain.reshape(1, k), sc, sh, w)


def _even_in_kernel(x_ref, g_ref, sc_ref, sh_ref, w_ref, qn_ref, wq_ref, kvn_ref, wkv_ref,
                    p_ref, q_ref, kv_ref):
    h = _rms(x_ref[0], g_ref[...]) * (1.0 + sc_ref[0]) + sh_ref[0]
    hb = h.astype(BF16)

    def up(lat, gain_ref, wu_ref, out_ref):
        lb = _rms(lat, gain_ref[...]).astype(BF16)
        for j in range(out_ref.shape[2] // 512):
            cols = slice(j * 512, (j + 1) * 512)
            out_ref[0, :, cols] = _dot(lb, wu_ref[:, cols]).astype(out_ref.dtype)

    for j in range(P_WIDTH // 512):
        c0 = j * 512
        pj = _dot(hb, w_ref[:, c0:c0 + 512])
        p_ref[0, :, c0:c0 + 512] = pj.astype(p_ref.dtype)
        if c0 <= P_QC < c0 + 512:
            up(pj[:, P_QC - c0:P_QC - c0 + MLA_Q_RANK], qn_ref, wq_ref, q_ref)
        if c0 <= P_KVC < c0 + 512:
            up(pj[:, P_KVC - c0:P_KVC - c0 + MLA_KV_RANK], kvn_ref, wkv_ref, kv_ref)


def _even_in(x, gain, sc, sh, w_p, q_norm, w_q, kv_norm, w_kv, tm=512, name="even_in"):
    bsz, n, d = x.shape
    tm = _tile(n, tm)
    per_batch = sc.shape[0] > 1
    mod_spec = pl.BlockSpec((1, 1, d), (lambda b, i: (b, 0, 0)) if per_batch else (lambda b, i: (0, 0, 0)))

    def whole(a):
        return pl.BlockSpec(a.shape, lambda b, i: (0, 0))

    args = [gain.reshape(1, d), w_p, q_norm.reshape(1, -1), w_q, kv_norm.reshape(1, -1), w_kv]
    widths = (w_p.shape[1], w_q.shape[1], w_kv.shape[1])
    return pl.pallas_call(
        _even_in_kernel,
        grid=(bsz, n // tm),
        in_specs=[pl.BlockSpec((1, tm, d), lambda b, i: (b, i, 0)), whole(args[0]), mod_spec, mod_spec]
        + [whole(a) for a in args[1:]],
        out_specs=[pl.BlockSpec((1, tm, w), lambda b, i: (b, i, 0)) for w in widths],
        out_shape=[jax.ShapeDtypeStruct((bsz, n, w), BF16) for w in widths],
        compiler_params=_params("parallel", "parallel"),
        name=name,
    )(x, args[0], sc, sh, *args[1:])


def _mnr_kernel(a1_ref, a2_ref, w1_ref, w2_ref, x_ref, gate_ref, pg_ref, o_ref):
    y = _dot(a1_ref[0], w1_ref[...]) + _dot(a2_ref[0], w2_ref[...])
    o_ref[0] = x_ref[0] + gate_ref[0] * _rms(y, pg_ref[...])


def _mnr(a1, a2, w1, w2, x, gate, post_g, tm=512, name="mnr"):
    bsz, n, d = x.shape
    k1, k2 = a1.shape[2], a2.shape[2]
    tm = _tile(n, tm)
    per_batch = gate.shape[0] > 1
    return pl.pallas_call(
        _mnr_kernel,
        grid=(bsz, n // tm),
        in_specs=[
            pl.BlockSpec((1, tm, k1), lambda b, i: (b, i, 0)),
            pl.BlockSpec((1, tm, k2), lambda b, i: (b, i, 0)),
            pl.BlockSpec((k1, d), lambda b, i: (0, 0)),
            pl.BlockSpec((k2, d), lambda b, i: (0, 0)),
            pl.BlockSpec((1, tm, d), lambda b, i: (b, i, 0)),
            pl.BlockSpec((1, 1, d), (lambda b, i: (b, 0, 0)) if per_batch else (lambda b, i: (0, 0, 0))),
            pl.BlockSpec((1, d), lambda b, i: (0, 0)),
        ],
        out_specs=pl.BlockSpec((1, tm, d), lambda b, i: (b, i, 0)),
        out_shape=jax.ShapeDtypeStruct((bsz, n, d), F32),
        compiler_params=_params("parallel", "parallel"),
        name=name,
    )(a1, a2, w1, w2, x, gate, post_g.reshape(1, d))


def _ffn_kernel(x_ref, xp_ref, xn_ref, g_ref, sc_ref, sh_ref, gate_ref, pg_ref,
                wg_ref, wv_ref, cw_ref, wo_ref, o_ref, xh_ref, acc_ref, g0_ref, g1_ref, v0_ref, v1_ref, *, seq_len):
    tm = x_ref.shape[1]
    n_chunks = wg_ref.shape[0]
    i = pl.program_id(1)
    gain = g_ref[...]
    scale = 1.0 + sc_ref[0]
    shift = sh_ref[0]

    def modulated(rows):
        return (_rms(rows, gain) * scale + shift).astype(BF16)

    xh_ref[0:tm] = modulated(x_ref[0])
    xh_ref[tm:tm + 16] = modulated(jnp.concatenate([xp_ref[0], xn_ref[0]], axis=0))

    local = lax.broadcasted_iota(jnp.int32, (tm, 1), 0)
    pos = (i * tm + local) % seq_len
    seq_first = pos == 0
    seq_last = pos == seq_len - 1
    tile_first = local == 0
    tile_last = local == tm - 1
    acc_ref[...] = jnp.zeros_like(acc_ref)
    slots = ((g0_ref, v0_ref), (g1_ref, v1_ref))

    def project(c, slot):
        gs_ref, vs_ref = slots[slot]
        gs_ref[...] = _dot(xh_ref[...], wg_ref[c])
        vs_ref[...] = _dot(xh_ref[0:tm], wv_ref[c])

    def mix(c, slot):
        gs_ref, vs_ref = slots[slot]
        gm = gs_ref[0:tm]
        g_prev = jnp.where(tile_first, gs_ref[tm + 7:tm + 8], pltpu.roll(gm, 1, axis=0))
        g_prev = jnp.where(seq_first, 0.0, g_prev)
        g_next = jnp.where(tile_last, gs_ref[tm + 8:tm + 9], pltpu.roll(gm, tm - 1, axis=0))
        g_next = jnp.where(seq_last, 0.0, g_next)
        cw = cw_ref[c]
        y = g_prev * cw[0:1] + gm * cw[1:2] + g_next * cw[2:3] + cw[3:4]
        u = 0.5 * y * (1.0 + lax.erf(y * (1.0 / math.sqrt(2.0)))) * vs_ref[...]
        acc_ref[...] += _dot(u.astype(BF16), wo_ref[c])

    project(0, 0)

    def body(j, carry):
        c = 2 * j
        project(c + 1, 1)
        mix(c, 0)
        project(c + 2, 0)
        mix(c + 1, 1)
        return carry

    lax.fori_loop(0, (n_chunks - 1) // 2, body, 0)
    if n_chunks % 2 == 0:
        project(n_chunks - 1, 1)
        mix(n_chunks - 2, 0)
        mix(n_chunks - 1, 1)
    else:
        mix(n_chunks - 1, 0)
    o_ref[0] = x_ref[0] + gate_ref[0] * _rms(acc_ref[...], pg_ref[...])


def _ffn(x, pre_g, sc, sh, gate, post_g, wg, wv, cwb, wo, seq_len, tm=512, name="ffn"):
    bsz, n, d = x.shape
    nf, _, fc = wg.shape
    tm = _tile(n, tm)
    nb8 = n // 8
    per_batch = sc.shape[0] > 1
    mod_spec = pl.BlockSpec((1, 1, d), (lambda b, i: (b, 0, 0)) if per_batch else (lambda b, i: (0, 0, 0)))
    vec_spec = pl.BlockSpec((1, d), lambda b, i: (0, 0))
    return pl.pallas_call(
        functools.partial(_ffn_kernel, seq_len=seq_len),
        grid=(bsz, n // tm),
        in_specs=[
            pl.BlockSpec((1, tm, d), lambda b, i: (b, i, 0)),
            pl.BlockSpec((1, 8, d), lambda b, i: (b, jnp.maximum(i * (tm // 8) - 1, 0), 0)),
            pl.BlockSpec((1, 8, d), lambda b, i: (b, jnp.minimum((i + 1) * (tm // 8), nb8 - 1), 0)),
            vec_spec, mod_spec, mod_spec, mod_spec, vec_spec,
            pl.BlockSpec((nf, d, fc), lambda b, i: (0, 0, 0), pipeline_mode=pl.Buffered(1)),
            pl.BlockSpec((nf, d, fc), lambda b, i: (0, 0, 0), pipeline_mode=pl.Buffered(1)),
            pl.BlockSpec((nf, 8, fc), lambda b, i: (0, 0, 0), pipeline_mode=pl.Buffered(1)),
            pl.BlockSpec((nf, fc, d), lambda b, i: (0, 0, 0), pipeline_mode=pl.Buffered(1)),
        ],
        out_specs=pl.BlockSpec((1, tm, d), lambda b, i: (b, i, 0)),
        out_shape=jax.ShapeDtypeStruct((bsz, n, d), F32),
        scratch_shapes=[pltpu.VMEM((tm + 16, d), BF16), pltpu.VMEM((tm, d), F32),
                        pltpu.VMEM((tm + 16, fc), F32), pltpu.VMEM((tm + 16, fc), F32),
                        pltpu.VMEM((tm, fc), F32), pltpu.VMEM((tm, fc), F32)],
        compiler_params=_params("parallel", "parallel"),
        name=name,
    )(x, x, x, pre_g.reshape(1, d), sc, sh, gate, post_g.reshape(1, d), wg, wv, cwb, wo)


def _mla_kernel(*refs, n_ctx, n_lat):
    if n_lat:
        (q_ref, cq_ref, sq_ref, kvc_ref, tc_ref, kvl_ref, tl_ref, ck_ref, sk_ref,
         o_ref, kc_ref, vt_ref, ot_ref, s0_ref, s1_ref, p0_ref, p1_ref, qh_ref, oe_ref) = refs
    else:
        (q_ref, cq_ref, sq_ref, kvc_ref, tc_ref,
         o_ref, kc_ref, vt_ref, ot_ref, s0_ref, s1_ref, p0_ref, p1_ref, qh_ref, oe_ref) = refs
    hv = MLA_HEADS * MLA_NOPE
    _, n_blocks, vrows, rb = vt_ref.shape

    @pl.when(pl.program_id(1) == 0)
    def _build_keys():
        r = lax.broadcasted_iota(jnp.int32, (vrows, MLA_V), 0)
        c = lax.broadcasted_iota(jnp.int32, (vrows, MLA_V), 1)
        sel = (r == c).astype(BF16)

        def fill(kv_ref, t_ref, off, n, roped):
            tail = t_ref[0].astype(F32)
            kr = tail[:, T_KR:T_KR + MLA_ROPE]
            if roped:
                kr = kr * ck_ref[...] + tail[:, T_KRROT:T_KRROT + MLA_ROPE] * sk_ref[...]
            kr = kr.astype(BF16)
            is_v = lax.broadcasted_iota(jnp.int32, (vrows, rb), 0) < MLA_V
            for h in range(MLA_HEADS):
                kc_ref[h, off:off + n, 0:MLA_NOPE] = kv_ref[0, :, h * MLA_NOPE:(h + 1) * MLA_NOPE]
                kc_ref[h, off:off + n, MLA_NOPE:MLA_NOPE + MLA_ROPE] = kr
                kc_ref[h, off:off + n, MLA_NOPE + MLA_ROPE:128] = jnp.zeros((n, 128 - MLA_NOPE - MLA_ROPE), BF16)
                for jb in range(n // rb):
                    vt = lax.dot_general(
                        sel, kv_ref[0, jb * rb:(jb + 1) * rb, hv + h * MLA_V:hv + (h + 1) * MLA_V], NT_DIMS,
                        preferred_element_type=F32)
                    vt_ref[h, off // rb + jb] = jnp.where(is_v, vt, 1.0).astype(BF16)

        fill(kvc_ref, tc_ref, 0, n_ctx, False)
        if n_lat:
            fill(kvl_ref, tl_ref, n_ctx, n_lat, True)

    tq = q_ref.shape[1]
    hr = MLA_HEADS * MLA_ROPE
    qr = (q_ref[0, :, hv:hv + hr].astype(F32) * cq_ref[...]
          + q_ref[0, :, hv + hr:hv + 2 * hr].astype(F32) * sq_ref[...]).astype(BF16)
    pad = jnp.zeros((tq, 128 - MLA_NOPE - MLA_ROPE), BF16)
    for h in range(MLA_HEADS):
        qh_ref[h] = jnp.concatenate([q_ref[0, :, h * MLA_NOPE:(h + 1) * MLA_NOPE],
                                     qr[:, h * MLA_ROPE:(h + 1) * MLA_ROPE], pad], axis=1)

    s_refs = (s0_ref, s1_ref)
    p_refs = (p0_ref, p1_ref)
    maxes = [None] * MLA_HEADS
    for t in range(MLA_HEADS + 2):
        h_s, h_e, h_v = t, t - 1, t - 2
        do_s, do_e, do_v = h_s < MLA_HEADS, 0 <= h_e < MLA_HEADS, 0 <= h_v
        if do_v:
            oe_ref[...] = jnp.zeros_like(oe_ref)

        def block(j, m, h_s=h_s, h_e=h_e, h_v=h_v, do_s=do_s, do_e=do_e, do_v=do_v):
            rows = pl.ds(pl.multiple_of(j * rb, rb), rb)
            if do_s:
                blk = lax.dot_general(kc_ref[h_s, rows, :], qh_ref[h_s], NT_DIMS, preferred_element_type=F32)
                s_refs[h_s % 2][rows, :] = blk
                m = jnp.maximum(m, jnp.max(blk, axis=0, keepdims=True))
            if do_e:
                p_refs[h_e % 2][rows, :] = jnp.exp2(s_refs[h_e % 2][rows, :] - maxes[h_e]).astype(BF16)
            if do_v:
                oe_ref[...] += _dot(vt_ref[h_v, j], p_refs[h_v % 2][rows, :])
            return m

        m = lax.fori_loop(0, n_blocks, block, jnp.full((1, tq), -jnp.inf, F32), unroll=True)
        if do_s:
            maxes[h_s] = m
        if do_v:
            ot_ref[h_v * MLA_V:(h_v + 1) * MLA_V, :] = oe_ref[0:MLA_V] / oe_ref[MLA_V:MLA_V + 1]
    o_ref[0] = ot_ref[...].T.astype(o_ref.dtype)


def _mla(q, cosq, sinq, kv_c, p_c, kv_l=None, p_l=None, cosk=None, sink=None, tq=512, name="mla"):
    bsz, nq, qw = q.shape
    n_ctx = kv_c.shape[1]
    n_lat = 0 if kv_l is None else kv_l.shape[1]
    nk = n_ctx + n_lat
    tq = _tile(nq, tq)
    rb = math.gcd(math.gcd(n_ctx, n_lat), 256)
    vrows = MLA_V + 16
    hr = MLA_HEADS * MLA_ROPE
    tail_blk = P_TAIL // 128
    in_specs = [
        pl.BlockSpec((1, tq, qw), lambda b, i: (b, i, 0)),
        pl.BlockSpec((tq, hr), lambda b, i: (i, 0)),
        pl.BlockSpec((tq, hr), lambda b, i: (i, 0)),
        pl.BlockSpec((1, n_ctx, kv_c.shape[2]), lambda b, i: (b, 0, 0)),
        pl.BlockSpec((1, n_ctx, 128), lambda b, i: (b, 0, tail_blk)),
    ]
    args = [q, cosq, sinq, kv_c, p_c]
    if n_lat:
        in_specs += [
            pl.BlockSpec((1, n_lat, kv_l.shape[2]), lambda b, i: (b, 0, 0)),
            pl.BlockSpec((1, n_lat, 128), lambda b, i: (b, 0, tail_blk)),
            pl.BlockSpec((n_lat, MLA_ROPE), lambda b, i: (0, 0)),
            pl.BlockSpec((n_lat, MLA_ROPE), lambda b, i: (0, 0)),
        ]
        args += [kv_l, p_l, cosk, sink]
    return pl.pallas_call(
        functools.partial(_mla_kernel, n_ctx=n_ctx, n_lat=n_lat),
        grid=(bsz, nq // tq),
        in_specs=in_specs,
        out_specs=pl.BlockSpec((1, tq, MLA_HEADS * MLA_V), lambda b, i: (b, i, 0)),
        out_shape=jax.ShapeDtypeStruct((bsz, nq, MLA_HEADS * MLA_V), BF16),
        scratch_shapes=[pltpu.VMEM((MLA_HEADS, nk, 128), BF16),
                        pltpu.VMEM((MLA_HEADS, nk // rb, vrows, rb), BF16),
                        pltpu.VMEM((MLA_HEADS * MLA_V, tq), F32),
                        pltpu.VMEM((nk, tq), F32), pltpu.VMEM((nk, tq), F32),
                        pltpu.VMEM((nk, tq), BF16), pltpu.VMEM((nk, tq), BF16),
                        pltpu.VMEM((MLA_HEADS, tq, 128), BF16),
                        pltpu.VMEM((vrows, tq), F32)],
        compiler_params=_params("parallel", "arbitrary"),
        name=name,
    )(*args)


def _gla_kernel(*refs, want_ctx):
    (qc_ref, kc_ref, tc_ref, vc_ref, grc_ref, ql_ref, kl_ref, tl_ref, vl_ref, grl_ref,
     wg_ref, bg_ref, on_ref) = refs[:13]
    if want_ctx:
        oc_ref, ol_ref = refs[13:15]
        scratch = refs[15:]
    else:
        oc_ref, ol_ref = None, refs[13]
        scratch = refs[14:]
    lgc_ref, lgl_ref, accc_ref, accl_ref, st_ref = scratch
    cs = GLA_CHUNK
    hk = GLA_HEADS * GLA_DK

    shift = cs.bit_length() - 1
    for t_ref, lg_ref in ((tc_ref, lgc_ref), (tl_ref, lgl_ref)):
        n = t_ref.shape[1]
        tr = _tile(n, 256)
        r = lax.broadcasted_iota(jnp.int32, (tr, tr), 0)
        c = lax.broadcasted_iota(jnp.int32, (tr, tr), 1)
        same_chunk = (r >> shift) == (c >> shift)
        tri_fw = (same_chunk & (r >= c)).astype(F32)
        tri_bw = (same_chunk & (r <= c)).astype(F32)

        def cum_body(i, carry, t_ref=t_ref, lg_ref=lg_ref, tr=tr, tri_fw=tri_fw, tri_bw=tri_bw):
            rows = pl.ds(pl.multiple_of(i * tr, tr), tr)
            z = _dot(t_ref[0, rows, :], wg_ref[...]) + bg_ref[...]
            lg = (jnp.minimum(z, 0.0) - jnp.log(1.0 + jnp.exp(-jnp.abs(z)))) * (1.0 / GLA_GATE_NORM)
            lg_ref[rows, 0:hk] = jnp.dot(tri_fw, lg[:, 0:hk], precision=lax.Precision.HIGHEST,
                                         preferred_element_type=F32)
            lg_ref[rows, hk:2 * hk] = jnp.dot(tri_bw, lg[:, hk:2 * hk], precision=lax.Precision.HIGHEST,
                                              preferred_element_type=F32)
            return carry

        lax.fori_loop(0, n // tr, cum_body, 0)

    row = lax.broadcasted_iota(jnp.int32, (cs, cs), 0)
    col = lax.broadcasted_iota(jnp.int32, (cs, cs), 1)
    masks = (row >= col, row <= col)
    st_ref[...] = jnp.zeros_like(st_ref)

    def chunk(q_ref, k_ref, v_ref, lg_ref, acc_ref, d, r0):
        rows = pl.ds(pl.multiple_of(r0, cs), cs)
        cum = lg_ref[rows, d * hk:(d + 1) * hk]
        tot = cum[cs - 1:cs] if d == 0 else cum[0:1]
        ref = 0.5 * tot
        q = q_ref[0, rows, :].astype(F32)
        k = k_ref[0, rows, :].astype(F32)
        v = v_ref[0, rows, :]
        qt = (q * jnp.exp(cum - ref)).astype(BF16)
        kt = (k * jnp.exp(ref - cum)).astype(BF16)
        qs = (q * jnp.exp(cum)).astype(BF16)
        kd = (k * jnp.exp(tot - cum)).astype(BF16)
        dec = jnp.exp(tot)
        for h in range(GLA_HEADS):
            ks = slice(h * GLA_DK, (h + 1) * GLA_DK)
            vs = slice(h * GLA_DV, (h + 1) * GLA_DV)
            a = lax.dot_general(qt[:, ks], kt[:, ks], NT_DIMS, preferred_element_type=F32)
            a = jnp.where(masks[d], a, 0.0).astype(BF16)
            st = st_ref[d, h]
            acc_ref[d, rows, vs] = _dot(a, v[:, vs]) + lax.dot_general(
                qs[:, ks], st.astype(BF16), NT_DIMS, preferred_element_type=F32)
            st_ref[d, h] = st * dec[:, ks] + lax.dot_general(
                v[:, vs], kd[:, ks], TN_DIMS, preferred_element_type=F32)

    def scan(q_ref, k_ref, v_ref, lg_ref, acc_ref):
        n_chunks = q_ref.shape[1] // cs

        def body(i, carry):
            chunk(q_ref, k_ref, v_ref, lg_ref, acc_ref, 0, i * cs)
            chunk(q_ref, k_ref, v_ref, lg_ref, acc_ref, 1, (n_chunks - 1 - i) * cs)
            return carry

        lax.fori_loop(0, n_chunks, body, 0)

    def finish(acc_ref, gr_ref, o_ref):
        n = acc_ref.shape[1]
        tr = _tile(n, 256)

        def body(i, carry):
            rows = pl.ds(pl.multiple_of(i * tr, tr), tr)
            gr = gr_ref[0, rows, :].astype(F32)
            gate = gr * _sigmoid(gr)
            for h in range(GLA_HEADS):
                vs = slice(h * GLA_DV, (h + 1) * GLA_DV)
                o = acc_ref[0, rows, vs] + acc_ref[1, rows, vs]
                o_ref[0, rows, vs] = (_rms(o, on_ref[...]) * gate[:, vs]).astype(o_ref.dtype)
            return carry

        lax.fori_loop(0, n // tr, body, 0)

    scan(qc_ref, kc_ref, vc_ref, lgc_ref, accc_ref)
    scan(ql_ref, kl_ref, vl_ref, lgl_ref, accl_ref)
    if want_ctx:
        finish(accc_ref, grc_ref, oc_ref)
    finish(accl_ref, grl_ref, ol_ref)


def _gla(p_c, p_l, w_gate, b_gate, o_norm, want_ctx, name="gla"):
    bsz, n_ctx, _ = p_c.shape
    n_lat = p_l.shape[1]
    hk, hv = GLA_HEADS * GLA_DK, GLA_HEADS * GLA_DV

    def specs(n):
        return [
            pl.BlockSpec((1, n, hk), lambda b: (b, 0, P_GQ // hk)),
            pl.BlockSpec((1, n, hk), lambda b: (b, 0, P_GK // hk)),
            pl.BlockSpec((1, n, 128), lambda b: (b, 0, P_TAIL // 128)),
            pl.BlockSpec((1, n, hv), lambda b: (b, 0, P_GV // hv)),
            pl.BlockSpec((1, n, hv), lambda b: (b, 0, P_GR // hv)),
        ]

    in_specs = specs(n_ctx) + specs(n_lat) + [
        pl.BlockSpec((128, 2 * hk), lambda b: (0, 0)),
        pl.BlockSpec((1, 2 * hk), lambda b: (0, 0)),
        pl.BlockSpec((1, GLA_DV), lambda b: (0, 0)),
    ]
    out_specs = [pl.BlockSpec((1, n_lat, hv), lambda b: (b, 0, 0))]
    out_shape = [jax.ShapeDtypeStruct((bsz, n_lat, hv), BF16)]
    if want_ctx:
        out_specs.insert(0, pl.BlockSpec((1, n_ctx, hv), lambda b: (b, 0, 0)))
        out_shape.insert(0, jax.ShapeDtypeStruct((bsz, n_ctx, hv), BF16))
    outs = pl.pallas_call(
        functools.partial(_gla_kernel, want_ctx=want_ctx),
        grid=(bsz,),
        in_specs=in_specs,
        out_specs=out_specs,
        out_shape=out_shape,
        scratch_shapes=[
            pltpu.VMEM((n_ctx, 2 * hk), F32), pltpu.VMEM((n_lat, 2 * hk), F32),
            pltpu.VMEM((2, n_ctx, hv), F32), pltpu.VMEM((2, n_lat, hv), F32),
            pltpu.VMEM((2, GLA_HEADS, GLA_DV, GLA_DK), F32),
        ],
        compiler_params=_params("parallel"),
        name=name,
    )(p_c, p_c, p_c, p_c, p_c, p_l, p_l, p_l, p_l, p_l, w_gate, b_gate, o_norm.reshape(1, GLA_DV))
    return (outs[0], outs[1]) if want_ctx else (None, outs[0])


def _conf_kernel(a_ref, gt_ref, w_ref, b_ref, lg_ref, lb_ref, o_ref, u_ref):
    n = a_ref.shape[1]
    pad = (CONV_W - 1) // 2
    lead = 16
    u_ref[0:lead] = jnp.zeros((lead, CONV_CH), F32)
    u_ref[lead + n:lead + n + lead] = jnp.zeros((lead, CONV_CH), F32)
    u_ref[lead:lead + n] = a_ref[0].astype(F32) * _sigmoid(gt_ref[0].astype(F32))
    tr = _tile(n, 256)

    def body(i, carry):
        r0 = pl.multiple_of(i * tr, tr)
        ext = u_ref[pl.ds(r0, tr + 2 * lead), :]
        y = jnp.zeros((tr, CONV_CH), F32) + b_ref[...]
        for k in range(CONV_W):
            off = lead - pad + k
            y = y + pltpu.roll(ext, tr + 2 * lead - off, axis=0)[0:tr] * w_ref[k:k + 1, :]
        mu = jnp.mean(y, axis=-1, keepdims=True)
        yc = y - mu
        z = yc * lax.rsqrt(jnp.mean(yc * yc, axis=-1, keepdims=True) + EPS) * lg_ref[...] + lb_ref[...]
        o_ref[0, pl.ds(r0, tr), :] = (z * _sigmoid(z)).astype(o_ref.dtype)
        return carry

    lax.fori_loop(0, n // tr, body, 0)


def _conformer(p, w, b, ln_g, ln_b, name="conformer"):
    bsz, n, _ = p.shape
    vec = pl.BlockSpec((1, CONV_CH), lambda i: (0, 0))
    return pl.pallas_call(
        _conf_kernel,
        grid=(bsz,),
        in_specs=[
            pl.BlockSpec((1, n, CONV_CH), lambda i: (i, 0, 0)),
            pl.BlockSpec((1, n, CONV_CH), lambda i: (i, 0, 1)),
            pl.BlockSpec((CONV_W + 1, CONV_CH), lambda i: (0, 0)),
            vec, vec, vec,
        ],
        out_specs=pl.BlockSpec((1, n, CONV_CH), lambda i: (i, 0, 0)),
        out_shape=jax.ShapeDtypeStruct((bsz, n, CONV_CH), BF16),
        scratch_shapes=[pltpu.VMEM((n + 32, CONV_CH), F32)],
        compiler_params=_params("parallel"),
        name=name,
    )(p, p, w, b.reshape(1, CONV_CH), ln_g.reshape(1, CONV_CH), ln_b.reshape(1, CONV_CH))


def _fnet_kernel(f_ref, cs_ref, wd_ref, o_ref, z_ref, *, scale):
    n = f_ref.shape[1]

    @pl.when(pl.program_id(1) == 0)
    def _channel_dft():
        zz = _dot(f_ref[0], cs_ref[...])
        z_ref[0:n] = zz[:, 0:FNET_CH].astype(BF16)
        z_ref[n:2 * n] = zz[:, FNET_CH:2 * FNET_CH].astype(BF16)

    o_ref[0] = (_dot(wd_ref[...], z_ref[...]) * scale).astype(o_ref.dtype)


def _dft_tables(n):
    k = np.arange(n, dtype=np.float64)
    ang = 2.0 * np.pi * np.outer(k, k) / n
    return np.cos(ang), np.sin(ang)


def _fnet(p, name="fnet"):
    bsz, n, width = p.shape
    cn, sn = _dft_tables(n)
    wd = jnp.asarray(np.concatenate([cn, -sn], axis=1), BF16)
    cg, sg = _dft_tables(FNET_GROUP_CH)
    eye = np.eye(FNET_GROUPS)
    cs = jnp.asarray(np.concatenate([np.kron(eye, cg), np.kron(eye, sg)], axis=1), BF16)
    tm = _tile(n, 512)
    scale = 1.0 / math.sqrt(n * FNET_GROUP_CH)
    return pl.pallas_call(
        functools.partial(_fnet_kernel, scale=scale),
        grid=(bsz, n // tm),
        in_specs=[
            pl.BlockSpec((1, n, FNET_CH), lambda b, i: (b, 0, (width - FNET_CH) // FNET_CH)),
            pl.BlockSpec((FNET_CH, 2 * FNET_CH), lambda b, i: (0, 0)),
            pl.BlockSpec((tm, 2 * n), lambda b, i: (i, 0)),
        ],
        out_specs=pl.BlockSpec((1, tm, FNET_CH), lambda b, i: (b, i, 0)),
        out_shape=jax.ShapeDtypeStruct((bsz, n, FNET_CH), BF16),
        scratch_shapes=[pltpu.VMEM((2 * n, FNET_CH), BF16)],
        compiler_params=_params("parallel", "arbitrary"),
        name=name,
    )(p, cs, wd)


def _rot_cols(w):
    a, b, c, d = jnp.split(w, 4, axis=-1)
    return jnp.concatenate([-b, a, -d, c], axis=-1)


def _even_weights(w_in, w_uq, w_ukv, w_gfw, b_gfw, w_gbw, b_gbw):
    d = w_in.shape[0]
    sizes = [MLA_Q_RANK, MLA_KV_RANK, MLA_ROPE, GLA_HEADS * GLA_DK, GLA_HEADS * GLA_DK, GLA_HEADS * GLA_DV,
             GLA_GATE_RANK, GLA_GATE_RANK, GLA_HEADS * GLA_DV]
    qc, kvc, kr, gq, gk, gv, glf, glb, gr = jnp.split(w_in, np.cumsum(sizes)[:-1].tolist(), axis=1)
    tail_pad = jnp.zeros((d, 128 - 2 * MLA_ROPE - 2 * GLA_GATE_RANK), w_in.dtype)
    w_p = jnp.concatenate([qc, gq * (GLA_DK ** -0.5), gk, kvc, kr, _rot_cols(kr), glf, glb, tail_pad, gv, gr],
                          axis=1).astype(BF16)
    uq = w_uq.reshape(MLA_Q_RANK, MLA_HEADS, MLA_NOPE + MLA_ROPE) * (MLA_SCALE * math.log2(math.e))
    qn = uq[:, :, :MLA_NOPE].reshape(MLA_Q_RANK, -1)
    qr = uq[:, :, MLA_NOPE:]
    w_q = jnp.concatenate([qn, qr.reshape(MLA_Q_RANK, -1), _rot_cols(qr).reshape(MLA_Q_RANK, -1)],
                          axis=1).astype(BF16)
    ukv = w_ukv.reshape(MLA_KV_RANK, MLA_HEADS, MLA_NOPE + MLA_V)
    w_kv = jnp.concatenate([ukv[:, :, :MLA_NOPE].reshape(MLA_KV_RANK, -1),
                            ukv[:, :, MLA_NOPE:].reshape(MLA_KV_RANK, -1)], axis=1).astype(BF16)
    hk = GLA_HEADS * GLA_DK
    w_gate = jnp.zeros((128, 2 * hk), F32)
    w_gate = w_gate.at[T_GF:T_GF + GLA_GATE_RANK, :hk].set(w_gfw)
    w_gate = w_gate.at[T_GB:T_GB + GLA_GATE_RANK, hk:].set(w_gbw).astype(BF16)
    b_gate = jnp.concatenate([b_gfw, b_gbw]).reshape(1, 2 * hk)
    return w_p, w_q, w_kv, w_gate, b_gate


def _ffn_weights(w_in, dw_w, dw_b, w_out):
    d, f2 = w_in.shape
    f = f2 // 2
    nf = f // FFN_CHUNK
    wg = w_in[:, :f].reshape(d, nf, FFN_CHUNK).transpose(1, 0, 2).astype(BF16)
    wv = w_in[:, f:].reshape(d, nf, FFN_CHUNK).transpose(1, 0, 2).astype(BF16)
    taps = jnp.concatenate([dw_w, dw_b[None], jnp.zeros((8 - dw_w.shape[0] - 1, f), F32)], axis=0)
    cwb = taps.reshape(8, nf, FFN_CHUNK).transpose(1, 0, 2)
    wo = w_out.reshape(nf, FFN_CHUNK, d).astype(BF16)
    return wg, wv, cwb, wo


def _rope_tables(n):
    rows = n // GRID_W
    row = jnp.repeat(jnp.arange(rows), GRID_W).astype(F32)
    col = jnp.tile(jnp.arange(GRID_W), rows).astype(F32)
    half = MLA_ROPE // 2
    inv = ROPE_BASE ** (-jnp.arange(0, half, 2, dtype=F32) / half)
    ar = row[:, None] * inv
    ac = col[:, None] * inv
    ang = jnp.concatenate([ar, ar, ac, ac], axis=-1)
    return jnp.cos(ang), jnp.sin(ang)


def kernel(x, c, ctx, c_ctx, mod_w, mod_b, pre_mix_g, post_mix_g, pre_ffn_g, post_ffn_g, ev_in_w, mla_q_norm, mla_kv_norm, mla_w_uq, mla_w_ukv, gla_w_gate_fw, gla_b_gate_fw, gla_w_gate_bw, gla_b_gate_bw, gla_o_norm, ev_out_w, od_in_w, conf_dw_w, conf_dw_b, conf_ln_g, conf_ln_b, od_out_w, ffn_in_w, ffn_dw_w, ffn_dw_b, ffn_out_w):
    bsz, n, d = x.shape
    n_ctx = ctx.shape[1]
    depth = mod_w.shape[0]
    last_ctx_reader = ((depth - 1) // 2) * 2

    rows = -(-(bsz + 1) // 8) * 8
    cvec = jnp.concatenate([c, c_ctx[None], jnp.zeros((rows - bsz - 1, d), F32)], axis=0)
    mod = _modulation(cvec, mod_w, mod_b)

    cos, sin = _rope_tables(n)
    cosq, sinq = jnp.tile(cos, (1, MLA_HEADS)), jnp.tile(sin, (1, MLA_HEADS))
    ones_q = jnp.ones((n_ctx, MLA_HEADS * MLA_ROPE), F32)
    zeros_q = jnp.zeros((n_ctx, MLA_HEADS * MLA_ROPE), F32)

    def flat(t):
        return t.reshape(1, bsz * n_ctx, t.shape[-1])

    def unflat(t):
        return t.reshape(bsz, n_ctx, t.shape[-1])

    x_lat, x_ctx = x, ctx
    for l in range(depth):
        need_ctx = l < last_ctx_reader
        use_ctx = need_ctx or (l % 2 == 0)
        i = l // 2
        m_lat = [t.reshape(bsz, 1, d) for t in jnp.split(mod[l, :bsz], 6, axis=-1)]
        m_ctx = [t.reshape(1, 1, d) for t in jnp.split(mod[l, bsz:bsz + 1], 6, axis=-1)]
        sh1, sc1, g1, sh2, sc2, g2 = m_lat
        csh1, csc1, cg1, csh2, csc2, cg2 = m_ctx

        if l % 2 == 0:
            w_p, w_q, w_kv, w_gate, b_gate = _even_weights(
                ev_in_w[i], mla_w_uq[i], mla_w_ukv[i], gla_w_gate_fw[i], gla_b_gate_fw[i],
                gla_w_gate_bw[i], gla_b_gate_bw[i])
            w_o = ev_out_w[i].astype(BF16)
            k1 = MLA_HEADS * MLA_V
            p_lat, q_lat, kv_lat = _even_in(x_lat, pre_mix_g[l], sc1, sh1, w_p, mla_q_norm[i], w_q,
                                            mla_kv_norm[i], w_kv, name="even_in_lat")
            p_ctx, q_ctx, kv_ctx = (unflat(t) for t in _even_in(
                flat(x_ctx), pre_mix_g[l], csc1, csh1, w_p, mla_q_norm[i], w_q, mla_kv_norm[i], w_kv,
                name="even_in_ctx"))
            a_lat = _mla(q_lat, cosq, sinq, kv_ctx, p_ctx, kv_lat, p_lat, cos, sin, name="mla_lat")
            g_ctx, g_lat = _gla(p_ctx, p_lat, w_gate, b_gate, gla_o_norm[i], need_ctx)
            x_lat = _mnr(a_lat, g_lat, w_o[:k1], w_o[k1:], x_lat, g1, post_mix_g[l], name="even_out_lat")
            if need_ctx:
                a_ctx = _mla(q_ctx, ones_q, zeros_q, kv_ctx, p_ctx, name="mla_ctx")
                x_ctx = unflat(_mnr(flat(a_ctx), flat(g_ctx), w_o[:k1], w_o[k1:], flat(x_ctx), cg1,
                                    post_mix_g[l], name="even_out_ctx"))
        else:
            w_p = od_in_w[i].astype(BF16)
            w_o = od_out_w[i].astype(BF16)
            w_dw = jnp.concatenate([conf_dw_w[i], jnp.zeros((1, CONV_CH), F32)], axis=0)
            streams = [(x_lat, sc1, sh1, g1, "lat")]
            if need_ctx:
                streams.append((x_ctx, csc1, csh1, cg1, "ctx"))
            outs = []
            for xs, sc, sh, gate, tag in streams:
                per_batch = sc.shape[0] > 1
                xin = xs if per_batch else flat(xs)
                p = _nmm(xin, pre_mix_g[l], w_p, sc, sh, name="odd_in_" + tag)
                p = p if per_batch else unflat(p)
                u = _conformer(p, w_dw, conf_dw_b[i], conf_ln_g[i], conf_ln_b[i], name="conformer_" + tag)
                fm = _fnet(p, name="fnet_" + tag)
                if not per_batch:
                    u, fm = flat(u), flat(fm)
                y = _mnr(u, fm, w_o[:CONV_CH], w_o[CONV_CH:], xin, gate, post_mix_g[l], name="odd_out_" + tag)
                outs.append(y if per_batch else unflat(y))
            x_lat = outs[0]
            if need_ctx:
                x_ctx = outs[1]

        wg, wv, cwb, wo = _ffn_weights(ffn_in_w[l], ffn_dw_w[l], ffn_dw_b[l], ffn_out_w[l])
        x_lat = _ffn(x_lat, pre_ffn_g[l], sc2, sh2, g2, post_ffn_g[l], wg, wv, cwb, wo, n, name="ffn_lat")
        if need_ctx:
            x_ctx = unflat(_ffn(flat(x_ctx), pre_ffn_g[l], csc2, csh2, cg2, post_ffn_g[l], wg, wv, cwb, wo,
                                n_ctx, name="ffn_ctx"))
    return x_lat
```

```python
import functools
import math

import numpy as np
import jax
import jax.numpy as jnp
from jax import lax
from jax.experimental import pallas as pl
from jax.experimental.pallas import tpu as pltpu

F32 = jnp.float32
BF16 = jnp.bfloat16

GRID_W = 64
ROPE_BASE = 10000.0
MLA_HEADS = 8
MLA_Q_RANK = 256
MLA_KV_RANK = 128
MLA_NOPE = 64
MLA_ROPE = 32
MLA_V = 64
MLA_SCALE = (MLA_NOPE + MLA_ROPE) ** -0.5
GLA_HEADS = 4
GLA_DK = 64
GLA_DV = 128
GLA_GATE_RANK = 16
GLA_GATE_NORM = 16.0
GLA_CHUNK = 64
CONV_CH = 768
CONV_W = 31
FNET_GROUPS = 4
FNET_GROUP_CH = 64
FNET_CH = FNET_GROUPS * FNET_GROUP_CH
FFN_CHUNK = 256
EPS = 1e-6

P_QC, P_GQ, P_GK = 0, 256, 512
P_KVC, P_TAIL = 768, 896
P_GV, P_GR = 1024, 1536
P_WIDTH = 2048
T_KR, T_KRROT, T_GF, T_GB = 0, 32, 64, 80

NT_DIMS = (((1,), (1,)), ((), ()))
TN_DIMS = (((0,), (0,)), ((), ()))

VMEM_LIMIT = 56 * 1024 * 1024


def _params(*sem):
    return pltpu.CompilerParams(dimension_semantics=sem, vmem_limit_bytes=VMEM_LIMIT)


def _rms(xf, gain):
    return xf * lax.rsqrt(jnp.mean(xf * xf, axis=-1, keepdims=True) + EPS) * gain


def _sigmoid(x):
    return 1.0 / (1.0 + jnp.exp(-x))


def _dot(a, b):
    return jnp.dot(a, b, preferred_element_type=F32)


def _tile(n, want):
    t = min(n, want)
    assert n % t == 0, (n, want)
    return t


def _mod_kernel(c_ref, w_ref, b_ref, o_ref):
    c = c_ref[...]
    s = c * _sigmoid(c)
    o_ref[0] = _dot(s.astype(BF16), w_ref[0].astype(BF16)) + b_ref[0]


def _modulation(cvec, mod_w, mod_b):
    depth, d, n6 = mod_w.shape
    rows = cvec.shape[0]
    tn = _tile(n6, 1536)
    return pl.pallas_call(
        _mod_kernel,
        grid=(depth, n6 // tn),
        in_specs=[
            pl.BlockSpec((rows, d), lambda l, j: (0, 0)),
            pl.BlockSpec((1, d, tn), lambda l, j: (l, 0, j)),
            pl.BlockSpec((1, 1, tn), lambda l, j: (l, 0, j)),
        ],
        out_specs=pl.BlockSpec((1, rows, tn), lambda l, j: (l, 0, j)),
        out_shape=jax.ShapeDtypeStruct((depth, rows, n6), F32),
        compiler_params=_params("parallel", "parallel"),
        name="modulation",
    )(cvec, mod_w, mod_b.reshape(depth, 1, n6))


def _nmm_kernel(x_ref, g_ref, sc_ref, sh_ref, w_ref, o_ref, *, col_w):
    h = _rms(x_ref[0], g_ref[...]) * (1.0 + sc_ref[0]) + sh_ref[0]
    hb = h.astype(BF16)
    for j in range(o_ref.shape[2] // col_w):
        cols = slice(j * col_w, (j + 1) * col_w)
        o_ref[0, :, cols] = _dot(hb, w_ref[:, cols]).astype(o_ref.dtype)


def _nmm(x, gain, w, sc, sh, tm=512, name="nmm"):
    bsz, n, k = x.shape
    nout = w.shape[1]
    tm = _tile(n, tm)
    col_w = next(cw for cw in (512, 256, 128) if nout % cw == 0)
    per_batch = sc.shape[0] > 1
    mod_spec = pl.BlockSpec((1, 1, k), (lambda b, i: (b, 0, 0)) if per_batch else (lambda b, i: (0, 0, 0)))
    return pl.pallas_call(
        functools.partial(_nmm_kernel, col_w=col_w),
        grid=(bsz, n // tm),
        in_specs=[pl.BlockSpec((1, tm, k), lambda b, i: (b, i, 0)),
                  pl.BlockSpec((1, k), lambda b, i: (0, 0)),
                  mod_spec, mod_spec,
                  pl.BlockSpec((k, nout), lambda b, i: (0, 0))],
        out_specs=pl.BlockSpec((1, tm, nout), lambda b, i: (b, i, 0)),
        out_shape=jax.ShapeDtypeStruct((bsz, n, nout), BF16),
        compiler_params=_params("parallel", "parallel"),
        name=name,
    )(x, gain.reshape(1, k), sc, sh, w)


def _even_in_kernel(x_ref, g_ref, sc_ref, sh_ref, w_ref, qn_ref, wq_ref, kvn_ref, wkv_ref,
                    p_ref, q_ref, kv_ref):
    h = _rms(x_ref[0], g_ref[...]) * (1.0 + sc_ref[0]) + sh_ref[0]
    hb = h.astype(BF16)

    def up(lat, gain_ref, wu_ref, out_ref):
        lb = _rms(lat, gain_ref[...]).astype(BF16)
        for j in range(out_ref.shape[2] // 512):
            cols = slice(j * 512, (j + 1) * 512)
            out_ref[0, :, cols] = _dot(lb, wu_ref[:, cols]).astype(out_ref.dtype)

    for j in range(P_WIDTH // 512):
        c0 = j * 512
        pj = _dot(hb, w_ref[:, c0:c0 + 512])
        p_ref[0, :, c0:c0 + 512] = pj.astype(p_ref.dtype)
        if c0 <= P_QC < c0 + 512:
            up(pj[:, P_QC - c0:P_QC - c0 + MLA_Q_RANK], qn_ref, wq_ref, q_ref)
        if c0 <= P_KVC < c0 + 512:
            up(pj[:, P_KVC - c0:P_KVC - c0 + MLA_KV_RANK], kvn_ref, wkv_ref, kv_ref)


def _even_in(x, gain, sc, sh, w_p, q_norm, w_q, kv_norm, w_kv, tm=512, name="even_in"):
    bsz, n, d = x.shape
    tm = _tile(n, tm)
    per_batch = sc.shape[0] > 1
    mod_spec = pl.BlockSpec((1, 1, d), (lambda b, i: (b, 0, 0)) if per_batch else (lambda b, i: (0, 0, 0)))

    def whole(a):
        return pl.BlockSpec(a.shape, lambda b, i: (0, 0))

    args = [gain.reshape(1, d), w_p, q_norm.reshape(1, -1), w_q, kv_norm.reshape(1, -1), w_kv]
    widths = (w_p.shape[1], w_q.shape[1], w_kv.shape[1])
    return pl.pallas_call(
        _even_in_kernel,
        grid=(bsz, n // tm),
        in_specs=[pl.BlockSpec((1, tm, d), lambda b, i: (b, i, 0)), whole(args[0]), mod_spec, mod_spec]
        + [whole(a) for a in args[1:]],
        out_specs=[pl.BlockSpec((1, tm, w), lambda b, i: (b, i, 0)) for w in widths],
        out_shape=[jax.ShapeDtypeStruct((bsz, n, w), BF16) for w in widths],
        compiler_params=_params("parallel", "parallel"),
        name=name,
    )(x, args[0], sc, sh, *args[1:])


def _mnr_kernel(a1_ref, a2_ref, w1_ref, w2_ref, x_ref, gate_ref, pg_ref, o_ref):
    y = _dot(a1_ref[0], w1_ref[...]) + _dot(a2_ref[0], w2_ref[...])
    o_ref[0] = x_ref[0] + gate_ref[0] * _rms(y, pg_ref[...])


def _mnr(a1, a2, w1, w2, x, gate, post_g, tm=512, name="mnr"):
    bsz, n, d = x.shape
    k1, k2 = a1.shape[2], a2.shape[2]
    tm = _tile(n, tm)
    per_batch = gate.shape[0] > 1
    return pl.pallas_call(
        _mnr_kernel,
        grid=(bsz, n // tm),
        in_specs=[
            pl.BlockSpec((1, tm, k1), lambda b, i: (b, i, 0)),
            pl.BlockSpec((1, tm, k2), lambda b, i: (b, i, 0)),
            pl.BlockSpec((k1, d), lambda b, i: (0, 0)),
            pl.BlockSpec((k2, d), lambda b, i: (0, 0)),
            pl.BlockSpec((1, tm, d), lambda b, i: (b, i, 0)),
            pl.BlockSpec((1, 1, d), (lambda b, i: (b, 0, 0)) if per_batch else (lambda b, i: (0, 0, 0))),
            pl.BlockSpec((1, d), lambda b, i: (0, 0)),
        ],
        out_specs=pl.BlockSpec((1, tm, d), lambda b, i: (b, i, 0)),
        out_shape=jax.ShapeDtypeStruct((bsz, n, d), F32),
        compiler_params=_params("parallel", "parallel"),
        name=name,
    )(a1, a2, w1, w2, x, gate, post_g.reshape(1, d))


def _ffn_kernel(x_ref, xp_ref, xn_ref, g_ref, sc_ref, sh_ref, gate_ref, pg_ref,
                wg_ref, wv_ref, cw_ref, wo_ref, o_ref, xh_ref, acc_ref, g0_ref, g1_ref, v0_ref, v1_ref, *, seq_len):
    tm = x_ref.shape[1]
    n_chunks = wg_ref.shape[0]
    i = pl.program_id(1)
    gain = g_ref[...]
    scale = 1.0 + sc_ref[0]
    shift = sh_ref[0]

    def modulated(rows):
        return (_rms(rows, gain) * scale + shift).astype(BF16)

    xh_ref[0:tm] = modulated(x_ref[0])
    xh_ref[tm:tm + 16] = modulated(jnp.concatenate([xp_ref[0], xn_ref[0]], axis=0))

    local = lax.broadcasted_iota(jnp.int32, (tm, 1), 0)
    pos = (i * tm + local) % seq_len
    seq_first = pos == 0
    seq_last = pos == seq_len - 1
    tile_first = local == 0
    tile_last = local == tm - 1
    acc_ref[...] = jnp.zeros_like(acc_ref)
    slots = ((g0_ref, v0_ref), (g1_ref, v1_ref))

    def project(c, slot):
        gs_ref, vs_ref = slots[slot]
        gs_ref[...] = _dot(xh_ref[...], wg_ref[c])
        vs_ref[...] = _dot(xh_ref[0:tm], wv_ref[c])

    def mix(c, slot):
        gs_ref, vs_ref = slots[slot]
        gm = gs_ref[0:tm]
        g_prev = jnp.where(tile_first, gs_ref[tm + 7:tm + 8], pltpu.roll(gm, 1, axis=0))
        g_prev = jnp.where(seq_first, 0.0, g_prev)
        g_next = jnp.where(tile_last, gs_ref[tm + 8:tm + 9], pltpu.roll(gm, tm - 1, axis=0))
        g_next = jnp.where(seq_last, 0.0, g_next)
        cw = cw_ref[c]
        y = g_prev * cw[0:1] + gm * cw[1:2] + g_next * cw[2:3] + cw[3:4]
        u = 0.5 * y * (1.0 + lax.erf(y * (1.0 / math.sqrt(2.0)))) * vs_ref[...]
        acc_ref[...] += _dot(u.astype(BF16), wo_ref[c])

    project(0, 0)

    def body(j, carry):
        c = 2 * j
        project(c + 1, 1)
        mix(c, 0)
        project(c + 2, 0)
        mix(c + 1, 1)
        return carry

    lax.fori_loop(0, (n_chunks - 1) // 2, body, 0)
    if n_chunks % 2 == 0:
        project(n_chunks - 1, 1)
        mix(n_chunks - 2, 0)
        mix(n_chunks - 1, 1)
    else:
        mix(n_chunks - 1, 0)
    o_ref[0] = x_ref[0] + gate_ref[0] * _rms(acc_ref[...], pg_ref[...])


def _ffn(x, pre_g, sc, sh, gate, post_g, wg, wv, cwb, wo, seq_len, tm=512, name="ffn"):
    bsz, n, d = x.shape
    nf, _, fc = wg.shape
    tm = _tile(n, tm)
    nb8 = n // 8
    per_batch = sc.shape[0] > 1
    mod_spec = pl.BlockSpec((1, 1, d), (lambda b, i: (b, 0, 0)) if per_batch else (lambda b, i: (0, 0, 0)))
    vec_spec = pl.BlockSpec((1, d), lambda b, i: (0, 0))
    return pl.pallas_call(
        functools.partial(_ffn_kernel, seq_len=seq_len),
        grid=(bsz, n // tm),
        in_specs=[
            pl.BlockSpec((1, tm, d), lambda b, i: (b, i, 0)),
            pl.BlockSpec((1, 8, d), lambda b, i: (b, jnp.maximum(i * (tm // 8) - 1, 0), 0)),
            pl.BlockSpec((1, 8, d), lambda b, i: (b, jnp.minimum((i + 1) * (tm // 8), nb8 - 1), 0)),
            vec_spec, mod_spec, mod_spec, mod_spec, vec_spec,
            pl.BlockSpec((nf, d, fc), lambda b, i: (0, 0, 0), pipeline_mode=pl.Buffered(1)),
            pl.BlockSpec((nf, d, fc), lambda b, i: (0, 0, 0), pipeline_mode=pl.Buffered(1)),
            pl.BlockSpec((nf, 8, fc), lambda b, i: (0, 0, 0), pipeline_mode=pl.Buffered(1)),
            pl.BlockSpec((nf, fc, d), lambda b, i: (0, 0, 0), pipeline_mode=pl.Buffered(1)),
        ],
        out_specs=pl.BlockSpec((1, tm, d), lambda b, i: (b, i, 0)),
        out_shape=jax.ShapeDtypeStruct((bsz, n, d), F32),
        scratch_shapes=[pltpu.VMEM((tm + 16, d), BF16), pltpu.VMEM((tm, d), F32),
                        pltpu.VMEM((tm + 16, fc), F32), pltpu.VMEM((tm + 16, fc), F32),
                        pltpu.VMEM((tm, fc), F32), pltpu.VMEM((tm, fc), F32)],
        compiler_params=_params("parallel", "parallel"),
        name=name,
    )(x, x, x, pre_g.reshape(1, d), sc, sh, gate, post_g.reshape(1, d), wg, wv, cwb, wo)


def _mla_kernel(*refs, n_ctx, n_lat):
    if n_lat:
        (q_ref, cq_ref, sq_ref, kvc_ref, tc_ref, kvl_ref, tl_ref, ck_ref, sk_ref,
         o_ref, kc_ref, vt_ref, ot_ref, s0_ref, s1_ref, p0_ref, p1_ref, qh_ref, oe_ref) = refs
    else:
        (q_ref, cq_ref, sq_ref, kvc_ref, tc_ref,
         o_ref, kc_ref, vt_ref, ot_ref, s0_ref, s1_ref, p0_ref, p1_ref, qh_ref, oe_ref) = refs
    hv = MLA_HEADS * MLA_NOPE
    _, n_blocks, vrows, rb = vt_ref.shape

    @pl.when(pl.program_id(1) == 0)
    def _build_keys():
        r = lax.broadcasted_iota(jnp.int32, (vrows, MLA_V), 0)
        c = lax.broadcasted_iota(jnp.int32, (vrows, MLA_V), 1)
        sel = (r == c).astype(BF16)

        def fill(kv_ref, t_ref, off, n, roped):
            tail = t_ref[0].astype(F32)
            kr = tail[:, T_KR:T_KR + MLA_ROPE]
            if roped:
                kr = kr * ck_ref[...] + tail[:, T_KRROT:T_KRROT + MLA_ROPE] * sk_ref[...]
            kr = kr.astype(BF16)
            is_v = lax.broadcasted_iota(jnp.int32, (vrows, rb), 0) < MLA_V
            for h in range(MLA_HEADS):
                kc_ref[h, off:off + n, 0:MLA_NOPE] = kv_ref[0, :, h * MLA_NOPE:(h + 1) * MLA_NOPE]
                kc_ref[h, off:off + n, MLA_NOPE:MLA_NOPE + MLA_ROPE] = kr
                kc_ref[h, off:off + n, MLA_NOPE + MLA_ROPE:128] = jnp.zeros((n, 128 - MLA_NOPE - MLA_ROPE), BF16)
                for jb in range(n // rb):
                    vt = lax.dot_general(
                        sel, kv_ref[0, jb * rb:(jb + 1) * rb, hv + h * MLA_V:hv + (h + 1) * MLA_V], NT_DIMS,
                        preferred_element_type=F32)
                    vt_ref[h, off // rb + jb] = jnp.where(is_v, vt, 1.0).astype(BF16)

        fill(kvc_ref, tc_ref, 0, n_ctx, False)
        if n_lat:
            fill(kvl_ref, tl_ref, n_ctx, n_lat, True)

    tq = q_ref.shape[1]
    hr = MLA_HEADS * MLA_ROPE
    qr = (q_ref[0, :, hv:hv + hr].astype(F32) * cq_ref[...]
          + q_ref[0, :, hv + hr:hv + 2 * hr].astype(F32) * sq_ref[...]).astype(BF16)
    pad = jnp.zeros((tq, 128 - MLA_NOPE - MLA_ROPE), BF16)
    for h in range(MLA_HEADS):
        qh_ref[h] = jnp.concatenate([q_ref[0, :, h * MLA_NOPE:(h + 1) * MLA_NOPE],
                                     qr[:, h * MLA_ROPE:(h + 1) * MLA_ROPE], pad], axis=1)

    s_refs = (s0_ref, s1_ref)
    p_refs = (p0_ref, p1_ref)
    maxes = [None] * MLA_HEADS
    for t in range(MLA_HEADS + 2):
        h_s, h_e, h_v = t, t - 1, t - 2
        do_s, do_e, do_v = h_s < MLA_HEADS, 0 <= h_e < MLA_HEADS, 0 <= h_v
        if do_v:
            oe_ref[...] = jnp.zeros_like(oe_ref)

        def block(j, m, h_s=h_s, h_e=h_e, h_v=h_v, do_s=do_s, do_e=do_e, do_v=do_v):
            rows = pl.ds(pl.multiple_of(j * rb, rb), rb)
            if do_s:
                blk = lax.dot_general(kc_ref[h_s, rows, :], qh_ref[h_s], NT_DIMS, preferred_element_type=F32)
                s_refs[h_s % 2][rows, :] = blk
                m = jnp.maximum(m, jnp.max(blk, axis=0, keepdims=True))
            if do_e:
                p_refs[h_e % 2][rows, :] = jnp.exp2(s_refs[h_e % 2][rows, :] - maxes[h_e]).astype(BF16)
            if do_v:
                oe_ref[...] += _dot(vt_ref[h_v, j], p_refs[h_v % 2][rows, :])
            return m

        m = lax.fori_loop(0, n_blocks, block, jnp.full((1, tq), -jnp.inf, F32), unroll=True)
        if do_s:
            maxes[h_s] = m
        if do_v:
            ot_ref[h_v * MLA_V:(h_v + 1) * MLA_V, :] = oe_ref[0:MLA_V] / oe_ref[MLA_V:MLA_V + 1]
    o_ref[0] = ot_ref[...].T.astype(o_ref.dtype)


def _mla(q, cosq, sinq, kv_c, p_c, kv_l=None, p_l=None, cosk=None, sink=None, tq=512, name="mla"):
    bsz, nq, qw = q.shape
    n_ctx = kv_c.shape[1]
    n_lat = 0 if kv_l is None else kv_l.shape[1]
    nk = n_ctx + n_lat
    tq = _tile(nq, tq)
    rb = math.gcd(math.gcd(n_ctx, n_lat), 256)
    vrows = MLA_V + 16
    hr = MLA_HEADS * MLA_ROPE
    tail_blk = P_TAIL // 128
    in_specs = [
        pl.BlockSpec((1, tq, qw), lambda b, i: (b, i, 0)),
        pl.BlockSpec((tq, hr), lambda b, i: (i, 0)),
        pl.BlockSpec((tq, hr), lambda b, i: (i, 0)),
        pl.BlockSpec((1, n_ctx, kv_c.shape[2]), lambda b, i: (b, 0, 0)),
        pl.BlockSpec((1, n_ctx, 128), lambda b, i: (b, 0, tail_blk)),
    ]
    args = [q, cosq, sinq, kv_c, p_c]
    if n_lat:
        in_specs += [
            pl.BlockSpec((1, n_lat, kv_l.shape[2]), lambda b, i: (b, 0, 0)),
            pl.BlockSpec((1, n_lat, 128), lambda b, i: (b, 0, tail_blk)),
            pl.BlockSpec((n_lat, MLA_ROPE), lambda b, i: (0, 0)),
            pl.BlockSpec((n_lat, MLA_ROPE), lambda b, i: (0, 0)),
        ]
        args += [kv_l, p_l, cosk, sink]
    return pl.pallas_call(
        functools.partial(_mla_kernel, n_ctx=n_ctx, n_lat=n_lat),
        grid=(bsz, nq // tq),
        in_specs=in_specs,
        out_specs=pl.BlockSpec((1, tq, MLA_HEADS * MLA_V), lambda b, i: (b, i, 0)),
        out_shape=jax.ShapeDtypeStruct((bsz, nq, MLA_HEADS * MLA_V), BF16),
        scratch_shapes=[pltpu.VMEM((MLA_HEADS, nk, 128), BF16),
                        pltpu.VMEM((MLA_HEADS, nk // rb, vrows, rb), BF16),
                        pltpu.VMEM((MLA_HEADS * MLA_V, tq), F32),
                        pltpu.VMEM((nk, tq), F32), pltpu.VMEM((nk, tq), F32),
                        pltpu.VMEM((nk, tq), BF16), pltpu.VMEM((nk, tq), BF16),
                        pltpu.VMEM((MLA_HEADS, tq, 128), BF16),
                        pltpu.VMEM((vrows, tq), F32)],
        compiler_params=_params("parallel", "arbitrary"),
        name=name,
    )(*args)


def _gla_kernel(*refs, want_ctx):
    (qc_ref, kc_ref, tc_ref, vc_ref, grc_ref, ql_ref, kl_ref, tl_ref, vl_ref, grl_ref,
     wg_ref, bg_ref, on_ref) = refs[:13]
    if want_ctx:
        oc_ref, ol_ref = refs[13:15]
        scratch = refs[15:]
    else:
        oc_ref, ol_ref = None, refs[13]
        scratch = refs[14:]
    lgc_ref, lgl_ref, accc_ref, accl_ref, st_ref = scratch
    cs = GLA_CHUNK
    hk = GLA_HEADS * GLA_DK

    shift = cs.bit_length() - 1
    for t_ref, lg_ref in ((tc_ref, lgc_ref), (tl_ref, lgl_ref)):
        n = t_ref.shape[1]
        tr = _tile(n, 256)
        r = lax.broadcasted_iota(jnp.int32, (tr, tr), 0)
        c = lax.broadcasted_iota(jnp.int32, (tr, tr), 1)
        same_chunk = (r >> shift) == (c >> shift)
        tri_fw = (same_chunk & (r >= c)).astype(BF16)
        tri_bw = (same_chunk & (r <= c)).astype(BF16)

        def cum_body(i, carry, t_ref=t_ref, lg_ref=lg_ref, tr=tr, tri_fw=tri_fw, tri_bw=tri_bw):
            rows = pl.ds(pl.multiple_of(i * tr, tr), tr)
            z = _dot(t_ref[0, rows, :], wg_ref[...]) + bg_ref[...]
            lg = (jnp.minimum(z, 0.0) - jnp.log(1.0 + jnp.exp(-jnp.abs(z)))) * (1.0 / GLA_GATE_NORM)
            hi = lg.astype(BF16)
            rest = lg - hi.astype(F32)
            mid = rest.astype(BF16)
            lo = (rest - mid.astype(F32)).astype(BF16)
            for tri, cols in ((tri_fw, slice(0, hk)), (tri_bw, slice(hk, 2 * hk))):
                lg_ref[rows, cols] = (_dot(tri, lo[:, cols]) + _dot(tri, mid[:, cols])) + _dot(tri, hi[:, cols])
            return carry

        lax.fori_loop(0, n // tr, cum_body, 0)

    br = math.gcd(math.gcd(qc_ref.shape[1], ql_ref.shape[1]), 256)
    cpb = br // cs
    row = lax.broadcasted_iota(jnp.int32, (br, br), 0)
    col = lax.broadcasted_iota(jnp.int32, (br, br), 1)
    same = (row >> shift) == (col >> shift)
    causal = (same & (row >= col), same & (row <= col))
    erow = lax.broadcasted_iota(jnp.int32, (cs, cs), 0)
    ecol = lax.broadcasted_iota(jnp.int32, (cs, cs), 1)
    eye = (erow == ecol).astype(F32)
    st_ref[...] = jnp.zeros_like(st_ref)

    def block_diag(x):
        wide = jnp.concatenate([x] * cpb, axis=1)
        return jnp.where(same, wide, jnp.zeros_like(wide))

    def block(q_ref, k_ref, v_ref, lg_ref, acc_ref, d, r0):
        rows = pl.ds(pl.multiple_of(r0, br), br)
        cum = lg_ref[rows, d * hk:(d + 1) * hk]
        cum3 = cum.reshape(cpb, cs, hk)
        edge = cum3[:, cs - 1:cs, :] if d == 0 else cum3[:, 0:1, :]
        tot = jnp.broadcast_to(edge, cum3.shape).reshape(br, hk)
        half = 0.5 * tot
        q = q_ref[0, rows, :].astype(F32)
        k = k_ref[0, rows, :].astype(F32)
        v = v_ref[0, rows, :]
        qt = (q * jnp.exp(cum - half)).astype(BF16)
        kt = (k * jnp.exp(half - cum)).astype(BF16)
        qs = (q * jnp.exp(cum)).astype(BF16)
        kd = (k * jnp.exp(tot - cum)).astype(BF16)
        order = range(cpb) if d == 0 else range(cpb - 1, -1, -1)
        for h in range(GLA_HEADS):
            ks = slice(h * GLA_DK, (h + 1) * GLA_DK)
            vs = slice(h * GLA_DV, (h + 1) * GLA_DV)
            a = lax.dot_general(qt[:, ks], kt[:, ks], NT_DIMS, preferred_element_type=F32)
            o = _dot(jnp.where(causal[d], a, 0.0).astype(BF16), v[:, vs])
            u = lax.dot_general(block_diag(kd[:, ks]), v[:, vs], TN_DIMS, preferred_element_type=F32)
            st = st_ref[d, h]
            starts = [None] * cpb
            for c in order:
                starts[c] = st.astype(BF16)
                dec = jnp.exp(jnp.sum(eye * edge[c, :, ks], axis=1, keepdims=True))
                st = st * dec + u[c * cs:(c + 1) * cs]
            st_ref[d, h] = st
            acc_ref[d, rows, vs] = o + _dot(block_diag(qs[:, ks]), jnp.concatenate(starts, axis=0))

    def scan(q_ref, k_ref, v_ref, lg_ref, acc_ref):
        n_blocks = q_ref.shape[1] // br

        def body(i, carry):
            block(q_ref, k_ref, v_ref, lg_ref, acc_ref, 0, i * br)
            block(q_ref, k_ref, v_ref, lg_ref, acc_ref, 1, (n_blocks - 1 - i) * br)
            return carry

        lax.fori_loop(0, n_blocks, body, 0)

    def finish(acc_ref, gr_ref, o_ref):
        n = acc_ref.shape[1]
        tr = _tile(n, 256)

        def body(i, carry):
            rows = pl.ds(pl.multiple_of(i * tr, tr), tr)
            gr = gr_ref[0, rows, :].astype(F32)
            gate = gr * _sigmoid(gr)
            for h in range(GLA_HEADS):
                vs = slice(h * GLA_DV, (h + 1) * GLA_DV)
                o = acc_ref[0, rows, vs] + acc_ref[1, rows, vs]
                o_ref[0, rows, vs] = (_rms(o, on_ref[...]) * gate[:, vs]).astype(o_ref.dtype)
            return carry

        lax.fori_loop(0, n // tr, body, 0)

    scan(qc_ref, kc_ref, vc_ref, lgc_ref, accc_ref)
    scan(ql_ref, kl_ref, vl_ref, lgl_ref, accl_ref)
    if want_ctx:
        finish(accc_ref, grc_ref, oc_ref)
    finish(accl_ref, grl_ref, ol_ref)


def _gla(p_c, p_l, w_gate, b_gate, o_norm, want_ctx, name="gla"):
    bsz, n_ctx, _ = p_c.shape
    n_lat = p_l.shape[1]
    hk, hv = GLA_HEADS * GLA_DK, GLA_HEADS * GLA_DV

    def specs(n):
        return [
            pl.BlockSpec((1, n, hk), lambda b: (b, 0, P_GQ // hk)),
            pl.BlockSpec((1, n, hk), lambda b: (b, 0, P_GK // hk)),
            pl.BlockSpec((1, n, 128), lambda b: (b, 0, P_TAIL // 128)),
            pl.BlockSpec((1, n, hv), lambda b: (b, 0, P_GV // hv)),
            pl.BlockSpec((1, n, hv), lambda b: (b, 0, P_GR // hv)),
        ]

    in_specs = specs(n_ctx) + specs(n_lat) + [
        pl.BlockSpec((128, 2 * hk), lambda b: (0, 0)),
        pl.BlockSpec((1, 2 * hk), lambda b: (0, 0)),
        pl.BlockSpec((1, GLA_DV), lambda b: (0, 0)),
    ]
    out_specs = [pl.BlockSpec((1, n_lat, hv), lambda b: (b, 0, 0))]
    out_shape = [jax.ShapeDtypeStruct((bsz, n_lat, hv), BF16)]
    if want_ctx:
        out_specs.insert(0, pl.BlockSpec((1, n_ctx, hv), lambda b: (b, 0, 0)))
        out_shape.insert(0, jax.ShapeDtypeStruct((bsz, n_ctx, hv), BF16))
    outs = pl.pallas_call(
        functools.partial(_gla_kernel, want_ctx=want_ctx),
        grid=(bsz,),
        in_specs=in_specs,
        out_specs=out_specs,
        out_shape=out_shape,
        scratch_shapes=[
            pltpu.VMEM((n_ctx, 2 * hk), F32), pltpu.VMEM((n_lat, 2 * hk), F32),
            pltpu.VMEM((2, n_ctx, hv), F32), pltpu.VMEM((2, n_lat, hv), F32),
            pltpu.VMEM((2, GLA_HEADS, GLA_DK, GLA_DV), F32),
        ],
        compiler_params=_params("parallel"),
        name=name,
    )(p_c, p_c, p_c, p_c, p_c, p_l, p_l, p_l, p_l, p_l, w_gate, b_gate, o_norm.reshape(1, GLA_DV))
    return (outs[0], outs[1]) if want_ctx else (None, outs[0])


def _conf_kernel(a_ref, gt_ref, w_ref, b_ref, lg_ref, lb_ref, o_ref, u_ref):
    n = a_ref.shape[1]
    pad = (CONV_W - 1) // 2
    lead = 16
    u_ref[0:lead] = jnp.zeros((lead, CONV_CH), F32)
    u_ref[lead + n:lead + n + lead] = jnp.zeros((lead, CONV_CH), F32)
    u_ref[lead:lead + n] = a_ref[0].astype(F32) * _sigmoid(gt_ref[0].astype(F32))
    tr = _tile(n, 256)

    def body(i, carry):
        r0 = pl.multiple_of(i * tr, tr)
        ext = u_ref[pl.ds(r0, tr + 2 * lead), :]
        y = jnp.zeros((tr, CONV_CH), F32) + b_ref[...]
        for k in range(CONV_W):
            off = lead - pad + k
            y = y + pltpu.roll(ext, tr + 2 * lead - off, axis=0)[0:tr] * w_ref[k:k + 1, :]
        mu = jnp.mean(y, axis=-1, keepdims=True)
        yc = y - mu
        z = yc * lax.rsqrt(jnp.mean(yc * yc, axis=-1, keepdims=True) + EPS) * lg_ref[...] + lb_ref[...]
        o_ref[0, pl.ds(r0, tr), :] = (z * _sigmoid(z)).astype(o_ref.dtype)
        return carry

    lax.fori_loop(0, n // tr, body, 0)


def _conformer(p, w, b, ln_g, ln_b, name="conformer"):
    bsz, n, _ = p.shape
    vec = pl.BlockSpec((1, CONV_CH), lambda i: (0, 0))
    return pl.pallas_call(
        _conf_kernel,
        grid=(bsz,),
        in_specs=[
            pl.BlockSpec((1, n, CONV_CH), lambda i: (i, 0, 0)),
            pl.BlockSpec((1, n, CONV_CH), lambda i: (i, 0, 1)),
            pl.BlockSpec((CONV_W + 1, CONV_CH), lambda i: (0, 0)),
            vec, vec, vec,
        ],
        out_specs=pl.BlockSpec((1, n, CONV_CH), lambda i: (i, 0, 0)),
        out_shape=jax.ShapeDtypeStruct((bsz, n, CONV_CH), BF16),
        scratch_shapes=[pltpu.VMEM((n + 32, CONV_CH), F32)],
        compiler_params=_params("parallel"),
        name=name,
    )(p, p, w, b.reshape(1, CONV_CH), ln_g.reshape(1, CONV_CH), ln_b.reshape(1, CONV_CH))


def _fnet_kernel(f_ref, cs_ref, wd_ref, o_ref, z_ref, *, scale):
    n = f_ref.shape[1]
    b = pl.program_id(1)

    @pl.when(pl.program_id(0) == 0)
    def _channel_dft():
        zz = _dot(f_ref[0], cs_ref[...])
        z_ref[b, 0:n] = zz[:, 0:FNET_CH].astype(BF16)
        z_ref[b, n:2 * n] = zz[:, FNET_CH:2 * FNET_CH].astype(BF16)

    o_ref[0] = (_dot(wd_ref[...], z_ref[b]) * scale).astype(o_ref.dtype)


def _dft_tables(n):
    k = np.arange(n, dtype=np.float64)
    ang = 2.0 * np.pi * np.outer(k, k) / n
    return np.cos(ang), np.sin(ang)


def _fnet(p, name="fnet"):
    bsz, n, width = p.shape
    cn, sn = _dft_tables(n)
    wd = jnp.asarray(np.concatenate([cn, -sn], axis=1), BF16)
    cg, sg = _dft_tables(FNET_GROUP_CH)
    eye = np.eye(FNET_GROUPS)
    cs = jnp.asarray(np.concatenate([np.kron(eye, cg), np.kron(eye, sg)], axis=1), BF16)
    tm = _tile(n, 512)
    scale = 1.0 / math.sqrt(n * FNET_GROUP_CH)
    return pl.pallas_call(
        functools.partial(_fnet_kernel, scale=scale),
        grid=(n // tm, bsz),
        in_specs=[
            pl.BlockSpec((1, n, FNET_CH), lambda i, b: (b, 0, (width - FNET_CH) // FNET_CH)),
            pl.BlockSpec((FNET_CH, 2 * FNET_CH), lambda i, b: (0, 0)),
            pl.BlockSpec((tm, 2 * n), lambda i, b: (i, 0)),
        ],
        out_specs=pl.BlockSpec((1, tm, FNET_CH), lambda i, b: (b, i, 0)),
        out_shape=jax.ShapeDtypeStruct((bsz, n, FNET_CH), BF16),
        scratch_shapes=[pltpu.VMEM((bsz, 2 * n, FNET_CH), BF16)],
        compiler_params=_params("arbitrary", "arbitrary"),
        name=name,
    )(p, cs, wd)


def _rot_cols(w):
    a, b, c, d = jnp.split(w, 4, axis=-1)
    return jnp.concatenate([-b, a, -d, c], axis=-1)


def _even_weights(w_in, w_uq, w_ukv, w_gfw, b_gfw, w_gbw, b_gbw):
    d = w_in.shape[0]
    sizes = [MLA_Q_RANK, MLA_KV_RANK, MLA_ROPE, GLA_HEADS * GLA_DK, GLA_HEADS * GLA_DK, GLA_HEADS * GLA_DV,
             GLA_GATE_RANK, GLA_GATE_RANK, GLA_HEADS * GLA_DV]
    qc, kvc, kr, gq, gk, gv, glf, glb, gr = jnp.split(w_in, np.cumsum(sizes)[:-1].tolist(), axis=1)
    tail_pad = jnp.zeros((d, 128 - 2 * MLA_ROPE - 2 * GLA_GATE_RANK), w_in.dtype)
    w_p = jnp.concatenate([qc, gq * (GLA_DK ** -0.5), gk, kvc, kr, _rot_cols(kr), glf, glb, tail_pad, gv, gr],
                          axis=1).astype(BF16)
    uq = w_uq.reshape(MLA_Q_RANK, MLA_HEADS, MLA_NOPE + MLA_ROPE) * (MLA_SCALE * math.log2(math.e))
    qn = uq[:, :, :MLA_NOPE].reshape(MLA_Q_RANK, -1)
    qr = uq[:, :, MLA_NOPE:]
    w_q = jnp.concatenate([qn, qr.reshape(MLA_Q_RANK, -1), _rot_cols(qr).reshape(MLA_Q_RANK, -1)],
                          axis=1).astype(BF16)
    ukv = w_ukv.reshape(MLA_KV_RANK, MLA_HEADS, MLA_NOPE + MLA_V)
    w_kv = jnp.concatenate([ukv[:, :, :MLA_NOPE].reshape(MLA_KV_RANK, -1),
                            ukv[:, :, MLA_NOPE:].reshape(MLA_KV_RANK, -1)], axis=1).astype(BF16)
    hk = GLA_HEADS * GLA_DK
    w_gate = jnp.zeros((128, 2 * hk), F32)
    w_gate = w_gate.at[T_GF:T_GF + GLA_GATE_RANK, :hk].set(w_gfw)
    w_gate = w_gate.at[T_GB:T_GB + GLA_GATE_RANK, hk:].set(w_gbw).astype(BF16)
    b_gate = jnp.concatenate([b_gfw, b_gbw]).reshape(1, 2 * hk)
    return w_p, w_q, w_kv, w_gate, b_gate


def _ffn_weights(w_in, dw_w, dw_b, w_out):
    d, f2 = w_in.shape
    f = f2 // 2
    nf = f // FFN_CHUNK
    wg = w_in[:, :f].reshape(d, nf, FFN_CHUNK).transpose(1, 0, 2).astype(BF16)
    wv = w_in[:, f:].reshape(d, nf, FFN_CHUNK).transpose(1, 0, 2).astype(BF16)
    taps = jnp.concatenate([dw_w, dw_b[None], jnp.zeros((8 - dw_w.shape[0] - 1, f), F32)], axis=0)
    cwb = taps.reshape(8, nf, FFN_CHUNK).transpose(1, 0, 2)
    wo = w_out.reshape(nf, FFN_CHUNK, d).astype(BF16)
    return wg, wv, cwb, wo


def _rope_tables(n):
    rows = n // GRID_W
    row = jnp.repeat(jnp.arange(rows), GRID_W).astype(F32)
    col = jnp.tile(jnp.arange(GRID_W), rows).astype(F32)
    half = MLA_ROPE // 2
    inv = ROPE_BASE ** (-jnp.arange(0, half, 2, dtype=F32) / half)
    ar = row[:, None] * inv
    ac = col[:, None] * inv
    ang = jnp.concatenate([ar, ar, ac, ac], axis=-1)
    return jnp.cos(ang), jnp.sin(ang)


def kernel(x, c, ctx, c_ctx, mod_w, mod_b, pre_mix_g, post_mix_g, pre_ffn_g, post_ffn_g, ev_in_w, mla_q_norm, mla_kv_norm, mla_w_uq, mla_w_ukv, gla_w_gate_fw, gla_b_gate_fw, gla_w_gate_bw, gla_b_gate_bw, gla_o_norm, ev_out_w, od_in_w, conf_dw_w, conf_dw_b, conf_ln_g, conf_ln_b, od_out_w, ffn_in_w, ffn_dw_w, ffn_dw_b, ffn_out_w):
    bsz, n, d = x.shape
    n_ctx = ctx.shape[1]
    depth = mod_w.shape[0]
    last_ctx_reader = ((depth - 1) // 2) * 2

    rows = -(-(bsz + 1) // 8) * 8
    cvec = jnp.concatenate([c, c_ctx[None], jnp.zeros((rows - bsz - 1, d), F32)], axis=0)
    mod = _modulation(cvec, mod_w, mod_b)

    cos, sin = _rope_tables(n)
    cosq, sinq = jnp.tile(cos, (1, MLA_HEADS)), jnp.tile(sin, (1, MLA_HEADS))
    ones_q = jnp.ones((n_ctx, MLA_HEADS * MLA_ROPE), F32)
    zeros_q = jnp.zeros((n_ctx, MLA_HEADS * MLA_ROPE), F32)

    def flat(t):
        return t.reshape(1, bsz * n_ctx, t.shape[-1])

    def unflat(t):
        return t.reshape(bsz, n_ctx, t.shape[-1])

    x_lat, x_ctx = x, ctx
    for l in range(depth):
        need_ctx = l < last_ctx_reader
        use_ctx = need_ctx or (l % 2 == 0)
        i = l // 2
        m_lat = [t.reshape(bsz, 1, d) for t in jnp.split(mod[l, :bsz], 6, axis=-1)]
        m_ctx = [t.reshape(1, 1, d) for t in jnp.split(mod[l, bsz:bsz + 1], 6, axis=-1)]
        sh1, sc1, g1, sh2, sc2, g2 = m_lat
        csh1, csc1, cg1, csh2, csc2, cg2 = m_ctx

        if l % 2 == 0:
            w_p, w_q, w_kv, w_gate, b_gate = _even_weights(
                ev_in_w[i], mla_w_uq[i], mla_w_ukv[i], gla_w_gate_fw[i], gla_b_gate_fw[i],
                gla_w_gate_bw[i], gla_b_gate_bw[i])
            w_o = ev_out_w[i].astype(BF16)
            k1 = MLA_HEADS * MLA_V
            p_lat, q_lat, kv_lat = _even_in(x_lat, pre_mix_g[l], sc1, sh1, w_p, mla_q_norm[i], w_q,
                                            mla_kv_norm[i], w_kv, name="even_in_lat")
            p_ctx, q_ctx, kv_ctx = (unflat(t) for t in _even_in(
                flat(x_ctx), pre_mix_g[l], csc1, csh1, w_p, mla_q_norm[i], w_q, mla_kv_norm[i], w_kv,
                name="even_in_ctx"))
            a_lat = _mla(q_lat, cosq, sinq, kv_ctx, p_ctx, kv_lat, p_lat, cos, sin, name="mla_lat")
            g_ctx, g_lat = _gla(p_ctx, p_lat, w_gate, b_gate, gla_o_norm[i], need_ctx)
            x_lat = _mnr(a_lat, g_lat, w_o[:k1], w_o[k1:], x_lat, g1, post_mix_g[l], name="even_out_lat")
            if need_ctx:
                a_ctx = _mla(q_ctx, ones_q, zeros_q, kv_ctx, p_ctx, name="mla_ctx")
                x_ctx = unflat(_mnr(flat(a_ctx), flat(g_ctx), w_o[:k1], w_o[k1:], flat(x_ctx), cg1,
                                    post_mix_g[l], name="even_out_ctx"))
        else:
            w_p = od_in_w[i].astype(BF16)
            w_o = od_out_w[i].astype(BF16)
            w_dw = jnp.concatenate([conf_dw_w[i], jnp.zeros((1, CONV_CH), F32)], axis=0)
            streams = [(x_lat, sc1, sh1, g1, "lat")]
            if need_ctx:
                streams.append((x_ctx, csc1, csh1, cg1, "ctx"))
            outs = []
            for xs, sc, sh, gate, tag in streams:
                per_batch = sc.shape[0] > 1
                xin = xs if per_batch else flat(xs)
                p = _nmm(xin, pre_mix_g[l], w_p, sc, sh, name="odd_in_" + tag)
                p = p if per_batch else unflat(p)
                u = _conformer(p, w_dw, conf_dw_b[i], conf_ln_g[i], conf_ln_b[i], name="conformer_" + tag)
                fm = _fnet(p, name="fnet_" + tag)
                if not per_batch:
                    u, fm = flat(u), flat(fm)
                y = _mnr(u, fm, w_o[:CONV_CH], w_o[CONV_CH:], xin, gate, post_mix_g[l], name="odd_out_" + tag)
                outs.append(y if per_batch else unflat(y))
            x_lat = outs[0]
            if need_ctx:
                x_ctx = outs[1]

        wg, wv, cwb, wo = _ffn_weights(ffn_in_w[l], ffn_dw_w[l], ffn_dw_b[l], ffn_out_w[l])
        x_lat = _ffn(x_lat, pre_ffn_g[l], sc2, sh2, g2, post_ffn_g[l], wg, wv, cwb, wo, n, name="ffn_lat")
        if need_ctx:
            x_ctx = unflat(_ffn(flat(x_ctx), pre_ffn_g[l], csc2, csh2, cg2, post_ffn_g[l], wg, wv, cwb, wo,
                                n_ctx, name="ffn_ctx"))
    return x_lat
```

```python
import functools
import math
from typing import NamedTuple

import numpy as np
import jax
import jax.numpy as jnp
from jax import lax
from jax.experimental import pallas as pl
from jax.experimental.pallas import tpu as pltpu

F32 = jnp.float32
BF16 = jnp.bfloat16

GRID_W = 64
ROPE_BASE = 10000.0
MLA_HEADS = 8
MLA_Q_RANK = 256
MLA_KV_RANK = 128
MLA_NOPE = 64
MLA_ROPE = 32
MLA_V = 64
MLA_SCALE = (MLA_NOPE + MLA_ROPE) ** -0.5
GLA_HEADS = 4
GLA_DK = 64
GLA_DV = 128
GLA_GATE_RANK = 16
GLA_GATE_NORM = 16.0
GLA_CHUNK = 64
CONV_CH = 768
CONV_W = 31
FNET_GROUPS = 4
FNET_GROUP_CH = 64
FNET_CH = FNET_GROUPS * FNET_GROUP_CH
FFN_CHUNK = 256
EPS = 1e-6

P_QC, P_GQ, P_GK = 0, 256, 512
P_KVC, P_TAIL = 768, 896
P_GV, P_GR = 1024, 1536
P_WIDTH = 2048
T_KR, T_KRROT, T_GF, T_GB = 0, 32, 64, 80

NT_DIMS = (((1,), (1,)), ((), ()))
TN_DIMS = (((0,), (0,)), ((), ()))

VMEM_LIMIT = 56 * 1024 * 1024


def _params(*sem):
    return pltpu.CompilerParams(dimension_semantics=sem, vmem_limit_bytes=VMEM_LIMIT)


def _rms(xf, gain):
    return xf * lax.rsqrt(jnp.mean(xf * xf, axis=-1, keepdims=True) + EPS) * gain


def _sigmoid(x):
    return 1.0 / (1.0 + jnp.exp(-x))


def _dot(a, b):
    return jnp.dot(a, b, preferred_element_type=F32)


def _tile(n, want):
    t = min(n, want)
    assert n % t == 0, (n, want)
    return t


class _Mod(NamedTuple):
    table: jax.Array
    first: int
    stride: int


def _mod_spec(m):
    return pl.BlockSpec((1, 1, m.table.shape[2]), lambda b, i: (m.first + b * m.stride, 0, 0))


def _mod_kernel(c_ref, w_ref, b_ref, o_ref):
    c = c_ref[...]
    s = c * _sigmoid(c)
    o_ref[0] = _dot(s.astype(BF16), w_ref[0].astype(BF16)) + b_ref[0]


def _modulation(cvec, mod_w, mod_b):
    depth, d, n6 = mod_w.shape
    rows = cvec.shape[0]
    tn = _tile(n6, 1536)
    return pl.pallas_call(
        _mod_kernel,
        grid=(depth, n6 // tn),
        in_specs=[
            pl.BlockSpec((rows, d), lambda l, j: (0, 0)),
            pl.BlockSpec((1, d, tn), lambda l, j: (l, 0, j)),
            pl.BlockSpec((1, 1, tn), lambda l, j: (l, 0, j)),
        ],
        out_specs=pl.BlockSpec((1, rows, tn), lambda l, j: (l, 0, j)),
        out_shape=jax.ShapeDtypeStruct((depth, rows, n6), F32),
        compiler_params=_params("parallel", "parallel"),
        name="modulation",
    )(cvec, mod_w, mod_b.reshape(depth, 1, n6))


def _nmm_kernel(x_ref, g_ref, sc_ref, sh_ref, w_ref, o_ref, *, col_w):
    h = _rms(x_ref[0], g_ref[...]) * (1.0 + sc_ref[0]) + sh_ref[0]
    hb = h.astype(BF16)
    for j in range(o_ref.shape[2] // col_w):
        cols = slice(j * col_w, (j + 1) * col_w)
        o_ref[0, :, cols] = _dot(hb, w_ref[:, cols]).astype(o_ref.dtype)


def _nmm(x, gain, w, sc, sh, tm=512, name="nmm"):
    bsz, n, k = x.shape
    nout = w.shape[1]
    tm = _tile(n, tm)
    col_w = next(cw for cw in (512, 256, 128) if nout % cw == 0)
    return pl.pallas_call(
        functools.partial(_nmm_kernel, col_w=col_w),
        grid=(bsz, n // tm),
        in_specs=[pl.BlockSpec((1, tm, k), lambda b, i: (b, i, 0)),
                  pl.BlockSpec((1, k), lambda b, i: (0, 0)),
                  _mod_spec(sc), _mod_spec(sh),
                  pl.BlockSpec((k, nout), lambda b, i: (0, 0))],
        out_specs=pl.BlockSpec((1, tm, nout), lambda b, i: (b, i, 0)),
        out_shape=jax.ShapeDtypeStruct((bsz, n, nout), BF16),
        compiler_params=_params("parallel", "parallel"),
        name=name,
    )(x, gain.reshape(1, k), sc.table, sh.table, w)


def _even_in_kernel(x_ref, g_ref, sc_ref, sh_ref, w_ref, qn_ref, wq_ref, kvn_ref, wkv_ref,
                    p_ref, q_ref, kv_ref):
    h = _rms(x_ref[0], g_ref[...]) * (1.0 + sc_ref[0]) + sh_ref[0]
    hb = h.astype(BF16)

    def up(lat, gain_ref, wu_ref, out_ref):
        lb = _rms(lat, gain_ref[...]).astype(BF16)
        for j in range(out_ref.shape[2] // 512):
            cols = slice(j * 512, (j + 1) * 512)
            out_ref[0, :, cols] = _dot(lb, wu_ref[:, cols]).astype(out_ref.dtype)

    for j in range(P_WIDTH // 512):
        c0 = j * 512
        pj = _dot(hb, w_ref[:, c0:c0 + 512])
        p_ref[0, :, c0:c0 + 512] = pj.astype(p_ref.dtype)
        if c0 <= P_QC < c0 + 512:
            up(pj[:, P_QC - c0:P_QC - c0 + MLA_Q_RANK], qn_ref, wq_ref, q_ref)
        if c0 <= P_KVC < c0 + 512:
            up(pj[:, P_KVC - c0:P_KVC - c0 + MLA_KV_RANK], kvn_ref, wkv_ref, kv_ref)


def _even_in(x, gain, sc, sh, w_p, q_norm, w_q, kv_norm, w_kv, tm=512, name="even_in"):
    bsz, n, d = x.shape
    tm = _tile(n, tm)

    def whole(a):
        return pl.BlockSpec(a.shape, lambda b, i: (0, 0))

    args = [gain.reshape(1, d), w_p, q_norm.reshape(1, -1), w_q, kv_norm.reshape(1, -1), w_kv]
    widths = (w_p.shape[1], w_q.shape[1], w_kv.shape[1])
    return pl.pallas_call(
        _even_in_kernel,
        grid=(bsz, n // tm),
        in_specs=[pl.BlockSpec((1, tm, d), lambda b, i: (b, i, 0)), whole(args[0]), _mod_spec(sc), _mod_spec(sh)]
        + [whole(a) for a in args[1:]],
        out_specs=[pl.BlockSpec((1, tm, w), lambda b, i: (b, i, 0)) for w in widths],
        out_shape=[jax.ShapeDtypeStruct((bsz, n, w), BF16) for w in widths],
        compiler_params=_params("parallel", "parallel"),
        name=name,
    )(x, args[0], sc.table, sh.table, *args[1:])


def _mnr_kernel(a1_ref, a2_ref, w1_ref, w2_ref, x_ref, gate_ref, pg_ref, o_ref):
    y = _dot(a1_ref[0], w1_ref[...]) + _dot(a2_ref[0], w2_ref[...])
    o_ref[0] = x_ref[0] + gate_ref[0] * _rms(y, pg_ref[...])


def _mnr(a1, a2, w1, w2, x, gate, post_g, tm=1024, name="mnr"):
    bsz, n, d = x.shape
    k1, k2 = a1.shape[2], a2.shape[2]
    tm = _tile(n, tm)
    return pl.pallas_call(
        _mnr_kernel,
        grid=(bsz, n // tm),
        in_specs=[
            pl.BlockSpec((1, tm, k1), lambda b, i: (b, i, 0)),
            pl.BlockSpec((1, tm, k2), lambda b, i: (b, i, 0)),
            pl.BlockSpec((k1, d), lambda b, i: (0, 0)),
            pl.BlockSpec((k2, d), lambda b, i: (0, 0)),
            pl.BlockSpec((1, tm, d), lambda b, i: (b, i, 0)),
            _mod_spec(gate),
            pl.BlockSpec((1, d), lambda b, i: (0, 0)),
        ],
        out_specs=pl.BlockSpec((1, tm, d), lambda b, i: (b, i, 0)),
        out_shape=jax.ShapeDtypeStruct((bsz, n, d), F32),
        compiler_params=_params("parallel", "parallel"),
        name=name,
    )(a1, a2, w1, w2, x, gate.table, post_g.reshape(1, d))


def _ffn_kernel(x_ref, xp_ref, xn_ref, g_ref, sc_ref, sh_ref, gate_ref, pg_ref,
                wg_ref, wv_ref, cw_ref, wo_ref, o_ref, xh_ref, acc_ref, g0_ref, g1_ref, v0_ref, v1_ref, *, seq_len):
    tm = x_ref.shape[1]
    n_chunks, fc, _ = wo_ref.shape
    i = pl.program_id(1)
    gain = g_ref[...]
    scale = 1.0 + sc_ref[0]
    shift = sh_ref[0]

    def modulated(rows):
        return (_rms(rows, gain) * scale + shift).astype(BF16)

    xh_ref[0:tm] = modulated(x_ref[0])
    xh_ref[tm:tm + 16] = modulated(jnp.concatenate([xp_ref[0], xn_ref[0]], axis=0))

    local = lax.broadcasted_iota(jnp.int32, (tm, 1), 0)
    pos = (i * tm + local) % seq_len
    seq_first = pos == 0
    seq_last = pos == seq_len - 1
    tile_first = local == 0
    tile_last = local == tm - 1
    acc_ref[...] = jnp.zeros_like(acc_ref)
    slots = ((g0_ref, v0_ref), (g1_ref, v1_ref))

    def project(c, slot):
        gs_ref, vs_ref = slots[slot]
        cols = pl.ds(pl.multiple_of(c * fc, fc), fc)
        gs_ref[...] = _dot(xh_ref[...], wg_ref[:, cols])
        vs_ref[...] = _dot(xh_ref[0:tm], wv_ref[:, cols])

    def mix(c, slot):
        gs_ref, vs_ref = slots[slot]
        gm = gs_ref[0:tm]
        g_prev = jnp.where(tile_first, gs_ref[tm + 7:tm + 8], pltpu.roll(gm, 1, axis=0))
        g_prev = jnp.where(seq_first, 0.0, g_prev)
        g_next = jnp.where(tile_last, gs_ref[tm + 8:tm + 9], pltpu.roll(gm, tm - 1, axis=0))
        g_next = jnp.where(seq_last, 0.0, g_next)
        cw = cw_ref[c]
        y = g_prev * cw[0:1] + gm * cw[1:2] + g_next * cw[2:3] + cw[3:4]
        u = 0.5 * y * (1.0 + lax.erf(y * (1.0 / math.sqrt(2.0)))) * vs_ref[...]
        acc_ref[...] += _dot(u.astype(BF16), wo_ref[c])

    project(0, 0)

    def body(j, carry):
        c = 2 * j
        project(c + 1, 1)
        mix(c, 0)
        project(c + 2, 0)
        mix(c + 1, 1)
        return carry

    lax.fori_loop(0, (n_chunks - 1) // 2, body, 0)
    if n_chunks % 2 == 0:
        project(n_chunks - 1, 1)
        mix(n_chunks - 2, 0)
        mix(n_chunks - 1, 1)
    else:
        mix(n_chunks - 1, 0)
    o_ref[0] = x_ref[0] + gate_ref[0] * _rms(acc_ref[...], pg_ref[...])


def _ffn(x, pre_g, sc, sh, gate, post_g, wg, wv, cwb, wo, seq_len, tm=512, name="ffn"):
    bsz, n, d = x.shape
    nf, fc, _ = wo.shape
    f = nf * fc
    tm = _tile(n, tm)
    nb8 = n // 8
    vec_spec = pl.BlockSpec((1, d), lambda b, i: (0, 0))
    return pl.pallas_call(
        functools.partial(_ffn_kernel, seq_len=seq_len),
        grid=(bsz, n // tm),
        in_specs=[
            pl.BlockSpec((1, tm, d), lambda b, i: (b, i, 0)),
            pl.BlockSpec((1, 8, d), lambda b, i: (b, jnp.maximum(i * (tm // 8) - 1, 0), 0)),
            pl.BlockSpec((1, 8, d), lambda b, i: (b, jnp.minimum((i + 1) * (tm // 8), nb8 - 1), 0)),
            vec_spec, _mod_spec(sc), _mod_spec(sh), _mod_spec(gate), vec_spec,
            pl.BlockSpec((d, f), lambda b, i: (0, 0), pipeline_mode=pl.Buffered(1)),
            pl.BlockSpec((d, f), lambda b, i: (0, 0), pipeline_mode=pl.Buffered(1)),
            pl.BlockSpec((nf, 8, fc), lambda b, i: (0, 0, 0), pipeline_mode=pl.Buffered(1)),
            pl.BlockSpec((nf, fc, d), lambda b, i: (0, 0, 0), pipeline_mode=pl.Buffered(1)),
        ],
        out_specs=pl.BlockSpec((1, tm, d), lambda b, i: (b, i, 0)),
        out_shape=jax.ShapeDtypeStruct((bsz, n, d), F32),
        scratch_shapes=[pltpu.VMEM((tm + 16, d), BF16), pltpu.VMEM((tm, d), F32),
                        pltpu.VMEM((tm + 16, fc), F32), pltpu.VMEM((tm + 16, fc), F32),
                        pltpu.VMEM((tm, fc), F32), pltpu.VMEM((tm, fc), F32)],
        compiler_params=_params("parallel", "parallel"),
        name=name,
    )(x, x, x, pre_g.reshape(1, d), sc.table, sh.table, gate.table, post_g.reshape(1, d), wg, wv, cwb, wo)


def _mla_kernel(*refs, n_ctx, n_lat):
    if n_lat:
        (q_ref, cq_ref, sq_ref, kvc_ref, tc_ref, kvl_ref, tl_ref, ck_ref, sk_ref,
         o_ref, kc_ref, vt_ref, ot_ref, s0_ref, s1_ref, p0_ref, p1_ref, qh_ref, oe_ref) = refs
    else:
        (q_ref, cq_ref, sq_ref, kvc_ref, tc_ref,
         o_ref, kc_ref, vt_ref, ot_ref, s0_ref, s1_ref, p0_ref, p1_ref, qh_ref, oe_ref) = refs
    hv = MLA_HEADS * MLA_NOPE
    _, n_blocks, vrows, rb = vt_ref.shape

    @pl.when(pl.program_id(1) == 0)
    def _build_keys():
        r = lax.broadcasted_iota(jnp.int32, (vrows, MLA_V), 0)
        c = lax.broadcasted_iota(jnp.int32, (vrows, MLA_V), 1)
        sel = (r == c).astype(BF16)

        def fill(kv_ref, t_ref, off, n, roped):
            tail = t_ref[0].astype(F32)
            kr = tail[:, T_KR:T_KR + MLA_ROPE]
            if roped:
                kr = kr * ck_ref[...] + tail[:, T_KRROT:T_KRROT + MLA_ROPE] * sk_ref[...]
            kr = kr.astype(BF16)
            is_v = lax.broadcasted_iota(jnp.int32, (vrows, rb), 0) < MLA_V
            for h in range(MLA_HEADS):
                kc_ref[h, off:off + n, 0:MLA_NOPE] = kv_ref[0, :, h * MLA_NOPE:(h + 1) * MLA_NOPE]
                kc_ref[h, off:off + n, MLA_NOPE:MLA_NOPE + MLA_ROPE] = kr
                kc_ref[h, off:off + n, MLA_NOPE + MLA_ROPE:128] = jnp.zeros((n, 128 - MLA_NOPE - MLA_ROPE), BF16)
                for jb in range(n // rb):
                    vt = lax.dot_general(
                        sel, kv_ref[0, jb * rb:(jb + 1) * rb, hv + h * MLA_V:hv + (h + 1) * MLA_V], NT_DIMS,
                        preferred_element_type=F32)
                    vt_ref[h, off // rb + jb] = jnp.where(is_v, vt, 1.0).astype(BF16)

        fill(kvc_ref, tc_ref, 0, n_ctx, False)
        if n_lat:
            fill(kvl_ref, tl_ref, n_ctx, n_lat, True)

    tq = q_ref.shape[1]
    hr = MLA_HEADS * MLA_ROPE
    qr = (q_ref[0, :, hv:hv + hr].astype(F32) * cq_ref[...]
          + q_ref[0, :, hv + hr:hv + 2 * hr].astype(F32) * sq_ref[...]).astype(BF16)
    pad = jnp.zeros((tq, 128 - MLA_NOPE - MLA_ROPE), BF16)
    for h in range(MLA_HEADS):
        qh_ref[h] = jnp.concatenate([q_ref[0, :, h * MLA_NOPE:(h + 1) * MLA_NOPE],
                                     qr[:, h * MLA_ROPE:(h + 1) * MLA_ROPE], pad], axis=1)

    s_refs = (s0_ref, s1_ref)
    p_refs = (p0_ref, p1_ref)
    maxes = [None] * MLA_HEADS
    for t in range(MLA_HEADS + 2):
        h_s, h_e, h_v = t, t - 1, t - 2
        do_s, do_e, do_v = h_s < MLA_HEADS, 0 <= h_e < MLA_HEADS, 0 <= h_v
        if do_v:
            oe_ref[...] = jnp.zeros_like(oe_ref)

        def block(j, m, h_s=h_s, h_e=h_e, h_v=h_v, do_s=do_s, do_e=do_e, do_v=do_v):
            rows = pl.ds(pl.multiple_of(j * rb, rb), rb)
            if do_s:
                blk = lax.dot_general(kc_ref[h_s, rows, :], qh_ref[h_s], NT_DIMS, preferred_element_type=F32)
                s_refs[h_s % 2][rows, :] = blk
                m = jnp.maximum(m, jnp.max(blk, axis=0, keepdims=True))
            if do_e:
                p_refs[h_e % 2][rows, :] = jnp.exp2(s_refs[h_e % 2][rows, :] - maxes[h_e]).astype(BF16)
            if do_v:
                oe_ref[...] += _dot(vt_ref[h_v, j], p_refs[h_v % 2][rows, :])
            return m

        m = lax.fori_loop(0, n_blocks, block, jnp.full((1, tq), -jnp.inf, F32), unroll=True)
        if do_s:
            maxes[h_s] = m
        if do_v:
            ot_ref[h_v * MLA_V:(h_v + 1) * MLA_V, :] = oe_ref[0:MLA_V] / oe_ref[MLA_V:MLA_V + 1]
    o_ref[0] = ot_ref[...].T.astype(o_ref.dtype)


def _mla(q, cosq, sinq, kv_c, p_c, kv_l=None, p_l=None, cosk=None, sink=None, tq=512, name="mla"):
    bsz, nq, qw = q.shape
    n_ctx = kv_c.shape[1]
    n_lat = 0 if kv_l is None else kv_l.shape[1]
    nk = n_ctx + n_lat
    tq = _tile(nq, tq)
    rb = math.gcd(math.gcd(n_ctx, n_lat), 256)
    vrows = MLA_V + 16
    hr = MLA_HEADS * MLA_ROPE
    tail_blk = P_TAIL // 128
    in_specs = [
        pl.BlockSpec((1, tq, qw), lambda b, i: (b, i, 0)),
        pl.BlockSpec((tq, hr), lambda b, i: (i, 0)),
        pl.BlockSpec((tq, hr), lambda b, i: (i, 0)),
        pl.BlockSpec((1, n_ctx, kv_c.shape[2]), lambda b, i: (b, 0, 0)),
        pl.BlockSpec((1, n_ctx, 128), lambda b, i: (b, 0, tail_blk)),
    ]
    args = [q, cosq, sinq, kv_c, p_c]
    if n_lat:
        in_specs += [
            pl.BlockSpec((1, n_lat, kv_l.shape[2]), lambda b, i: (b, 0, 0)),
            pl.BlockSpec((1, n_lat, 128), lambda b, i: (b, 0, tail_blk)),
            pl.BlockSpec((n_lat, MLA_ROPE), lambda b, i: (0, 0)),
            pl.BlockSpec((n_lat, MLA_ROPE), lambda b, i: (0, 0)),
        ]
        args += [kv_l, p_l, cosk, sink]
    return pl.pallas_call(
        functools.partial(_mla_kernel, n_ctx=n_ctx, n_lat=n_lat),
        grid=(bsz, nq // tq),
        in_specs=in_specs,
        out_specs=pl.BlockSpec((1, tq, MLA_HEADS * MLA_V), lambda b, i: (b, i, 0)),
        out_shape=jax.ShapeDtypeStruct((bsz, nq, MLA_HEADS * MLA_V), BF16),
        scratch_shapes=[pltpu.VMEM((MLA_HEADS, nk, 128), BF16),
                        pltpu.VMEM((MLA_HEADS, nk // rb, vrows, rb), BF16),
                        pltpu.VMEM((MLA_HEADS * MLA_V, tq), F32),
                        pltpu.VMEM((nk, tq), F32), pltpu.VMEM((nk, tq), F32),
                        pltpu.VMEM((nk, tq), BF16), pltpu.VMEM((nk, tq), BF16),
                        pltpu.VMEM((MLA_HEADS, tq, 128), BF16),
                        pltpu.VMEM((vrows, tq), F32)],
        compiler_params=_params("parallel", "arbitrary"),
        name=name,
    )(*args)


def _gla_kernel(*refs, want_ctx):
    (qc_ref, kc_ref, tc_ref, vc_ref, grc_ref, ql_ref, kl_ref, tl_ref, vl_ref, grl_ref,
     wg_ref, bg_ref, on_ref) = refs[:13]
    if want_ctx:
        oc_ref, ol_ref = refs[13:15]
        scratch = refs[15:]
    else:
        oc_ref, ol_ref = None, refs[13]
        scratch = refs[14:]
    lgc_ref, lgl_ref, accc_ref, accl_ref, st_ref = scratch
    cs = GLA_CHUNK
    hk = GLA_HEADS * GLA_DK

    shift = cs.bit_length() - 1
    for t_ref, lg_ref in ((tc_ref, lgc_ref), (tl_ref, lgl_ref)):
        n = t_ref.shape[1]
        tr = _tile(n, 256)
        r = lax.broadcasted_iota(jnp.int32, (tr, tr), 0)
        c = lax.broadcasted_iota(jnp.int32, (tr, tr), 1)
        same_chunk = (r >> shift) == (c >> shift)
        tri_fw = (same_chunk & (r >= c)).astype(BF16)
        tri_bw = (same_chunk & (r <= c)).astype(BF16)

        def cum_body(i, carry, t_ref=t_ref, lg_ref=lg_ref, tr=tr, tri_fw=tri_fw, tri_bw=tri_bw):
            rows = pl.ds(pl.multiple_of(i * tr, tr), tr)
            z = _dot(t_ref[0, rows, :], wg_ref[...]) + bg_ref[...]
            lg = (jnp.minimum(z, 0.0) - jnp.log(1.0 + jnp.exp(-jnp.abs(z)))) * (1.0 / GLA_GATE_NORM)
            hi = lg.astype(BF16)
            rest = lg - hi.astype(F32)
            mid = rest.astype(BF16)
            lo = (rest - mid.astype(F32)).astype(BF16)
            for tri, cols in ((tri_fw, slice(0, hk)), (tri_bw, slice(hk, 2 * hk))):
                lg_ref[rows, cols] = (_dot(tri, lo[:, cols]) + _dot(tri, mid[:, cols])) + _dot(tri, hi[:, cols])
            return carry

        lax.fori_loop(0, n // tr, cum_body, 0)

    br = math.gcd(math.gcd(qc_ref.shape[1], ql_ref.shape[1]), 256)
    cpb = br // cs
    row = lax.broadcasted_iota(jnp.int32, (br, br), 0)
    col = lax.broadcasted_iota(jnp.int32, (br, br), 1)
    same = (row >> shift) == (col >> shift)
    causal = (same & (row >= col), same & (row <= col))
    erow = lax.broadcasted_iota(jnp.int32, (cs, cs), 0)
    ecol = lax.broadcasted_iota(jnp.int32, (cs, cs), 1)
    eye = (erow == ecol).astype(F32)
    st_ref[...] = jnp.zeros_like(st_ref)

    def block_diag(x):
        wide = jnp.concatenate([x] * cpb, axis=1)
        return jnp.where(same, wide, jnp.zeros_like(wide))

    def block(q_ref, k_ref, v_ref, lg_ref, acc_ref, d, r0):
        rows = pl.ds(pl.multiple_of(r0, br), br)
        cum = lg_ref[rows, d * hk:(d + 1) * hk]
        cum3 = cum.reshape(cpb, cs, hk)
        edge = cum3[:, cs - 1:cs, :] if d == 0 else cum3[:, 0:1, :]
        tot = jnp.broadcast_to(edge, cum3.shape).reshape(br, hk)
        half = 0.5 * tot
        q = q_ref[0, rows, :].astype(F32)
        k = k_ref[0, rows, :].astype(F32)
        v = v_ref[0, rows, :]
        qt = (q * jnp.exp(cum - half)).astype(BF16)
        kt = (k * jnp.exp(half - cum)).astype(BF16)
        qs = (q * jnp.exp(cum)).astype(BF16)
        kd = (k * jnp.exp(tot - cum)).astype(BF16)
        order = range(cpb) if d == 0 else range(cpb - 1, -1, -1)
        for h in range(GLA_HEADS):
            ks = slice(h * GLA_DK, (h + 1) * GLA_DK)
            vs = slice(h * GLA_DV, (h + 1) * GLA_DV)
            a = lax.dot_general(qt[:, ks], kt[:, ks], NT_DIMS, preferred_element_type=F32)
            o = _dot(jnp.where(causal[d], a, 0.0).astype(BF16), v[:, vs])
            u = lax.dot_general(block_diag(kd[:, ks]), v[:, vs], TN_DIMS, preferred_element_type=F32)
            st = st_ref[d, h]
            starts = [None] * cpb
            for c in order:
                starts[c] = st.astype(BF16)
                dec = jnp.exp(jnp.sum(eye * edge[c, :, ks], axis=1, keepdims=True))
                st = st * dec + u[c * cs:(c + 1) * cs]
            st_ref[d, h] = st
            acc_ref[d, rows, vs] = o + _dot(block_diag(qs[:, ks]), jnp.concatenate(starts, axis=0))

    def scan(q_ref, k_ref, v_ref, lg_ref, acc_ref):
        n_blocks = q_ref.shape[1] // br

        def body(i, carry):
            block(q_ref, k_ref, v_ref, lg_ref, acc_ref, 0, i * br)
            block(q_ref, k_ref, v_ref, lg_ref, acc_ref, 1, (n_blocks - 1 - i) * br)
            return carry

        lax.fori_loop(0, n_blocks, body, 0)

    def finish(acc_ref, gr_ref, o_ref):
        n = acc_ref.shape[1]
        tr = _tile(n, 256)

        def body(i, carry):
            rows = pl.ds(pl.multiple_of(i * tr, tr), tr)
            gr = gr_ref[0, rows, :].astype(F32)
            gate = gr * _sigmoid(gr)
            for h in range(GLA_HEADS):
                vs = slice(h * GLA_DV, (h + 1) * GLA_DV)
                o = acc_ref[0, rows, vs] + acc_ref[1, rows, vs]
                o_ref[0, rows, vs] = (_rms(o, on_ref[...]) * gate[:, vs]).astype(o_ref.dtype)
            return carry

        lax.fori_loop(0, n // tr, body, 0)

    scan(qc_ref, kc_ref, vc_ref, lgc_ref, accc_ref)
    scan(ql_ref, kl_ref, vl_ref, lgl_ref, accl_ref)
    if want_ctx:
        finish(accc_ref, grc_ref, oc_ref)
    finish(accl_ref, grl_ref, ol_ref)


def _gla(p_c, p_l, w_gate, b_gate, o_norm, want_ctx, name="gla"):
    bsz, n_ctx, _ = p_c.shape
    n_lat = p_l.shape[1]
    hk, hv = GLA_HEADS * GLA_DK, GLA_HEADS * GLA_DV

    def specs(n):
        return [
            pl.BlockSpec((1, n, hk), lambda b: (b, 0, P_GQ // hk)),
            pl.BlockSpec((1, n, hk), lambda b: (b, 0, P_GK // hk)),
            pl.BlockSpec((1, n, 128), lambda b: (b, 0, P_TAIL // 128)),
            pl.BlockSpec((1, n, hv), lambda b: (b, 0, P_GV // hv)),
            pl.BlockSpec((1, n, hv), lambda b: (b, 0, P_GR // hv)),
        ]

    in_specs = specs(n_ctx) + specs(n_lat) + [
        pl.BlockSpec((128, 2 * hk), lambda b: (0, 0)),
        pl.BlockSpec((1, 2 * hk), lambda b: (0, 0)),
        pl.BlockSpec((1, GLA_DV), lambda b: (0, 0)),
    ]
    out_specs = [pl.BlockSpec((1, n_lat, hv), lambda b: (b, 0, 0))]
    out_shape = [jax.ShapeDtypeStruct((bsz, n_lat, hv), BF16)]
    if want_ctx:
        out_specs.insert(0, pl.BlockSpec((1, n_ctx, hv), lambda b: (b, 0, 0)))
        out_shape.insert(0, jax.ShapeDtypeStruct((bsz, n_ctx, hv), BF16))
    outs = pl.pallas_call(
        functools.partial(_gla_kernel, want_ctx=want_ctx),
        grid=(bsz,),
        in_specs=in_specs,
        out_specs=out_specs,
        out_shape=out_shape,
        scratch_shapes=[
            pltpu.VMEM((n_ctx, 2 * hk), F32), pltpu.VMEM((n_lat, 2 * hk), F32),
            pltpu.VMEM((2, n_ctx, hv), F32), pltpu.VMEM((2, n_lat, hv), F32),
            pltpu.VMEM((2, GLA_HEADS, GLA_DK, GLA_DV), F32),
        ],
        compiler_params=_params("parallel"),
        name=name,
    )(p_c, p_c, p_c, p_c, p_c, p_l, p_l, p_l, p_l, p_l, w_gate, b_gate, o_norm.reshape(1, GLA_DV))
    return (outs[0], outs[1]) if want_ctx else (None, outs[0])


def _conf_kernel(a_ref, gt_ref, w_ref, b_ref, lg_ref, lb_ref, o_ref, u_ref):
    n = a_ref.shape[1]
    pad = (CONV_W - 1) // 2
    lead = 16
    u_ref[0:lead] = jnp.zeros((lead, CONV_CH), F32)
    u_ref[lead + n:lead + n + lead] = jnp.zeros((lead, CONV_CH), F32)
    u_ref[lead:lead + n] = a_ref[0].astype(F32) * _sigmoid(gt_ref[0].astype(F32))
    tr = _tile(n, 256)

    def body(i, carry):
        r0 = pl.multiple_of(i * tr, tr)
        ext = u_ref[pl.ds(r0, tr + 2 * lead), :]
        y = jnp.zeros((tr, CONV_CH), F32) + b_ref[...]
        for k in range(CONV_W):
            off = lead - pad + k
            y = y + pltpu.roll(ext, tr + 2 * lead - off, axis=0)[0:tr] * w_ref[k:k + 1, :]
        mu = jnp.mean(y, axis=-1, keepdims=True)
        yc = y - mu
        z = yc * lax.rsqrt(jnp.mean(yc * yc, axis=-1, keepdims=True) + EPS) * lg_ref[...] + lb_ref[...]
        o_ref[0, pl.ds(r0, tr), :] = (z * _sigmoid(z)).astype(o_ref.dtype)
        return carry

    lax.fori_loop(0, n // tr, body, 0)


def _conformer(p, w, b, ln_g, ln_b, name="conformer"):
    bsz, n, _ = p.shape
    vec = pl.BlockSpec((1, CONV_CH), lambda i: (0, 0))
    return pl.pallas_call(
        _conf_kernel,
        grid=(bsz,),
        in_specs=[
            pl.BlockSpec((1, n, CONV_CH), lambda i: (i, 0, 0)),
            pl.BlockSpec((1, n, CONV_CH), lambda i: (i, 0, 1)),
            pl.BlockSpec((CONV_W + 1, CONV_CH), lambda i: (0, 0)),
            vec, vec, vec,
        ],
        out_specs=pl.BlockSpec((1, n, CONV_CH), lambda i: (i, 0, 0)),
        out_shape=jax.ShapeDtypeStruct((bsz, n, CONV_CH), BF16),
        scratch_shapes=[pltpu.VMEM((n + 32, CONV_CH), F32)],
        compiler_params=_params("parallel"),
        name=name,
    )(p, p, w, b.reshape(1, CONV_CH), ln_g.reshape(1, CONV_CH), ln_b.reshape(1, CONV_CH))


def _fnet_kernel(f_ref, cs_ref, wd_ref, o_ref, z_ref, *, scale):
    n = f_ref.shape[1]
    b = pl.program_id(1)

    @pl.when(pl.program_id(0) == 0)
    def _channel_dft():
        zz = _dot(f_ref[0], cs_ref[...])
        z_ref[b, 0:n] = zz[:, 0:FNET_CH].astype(BF16)
        z_ref[b, n:2 * n] = zz[:, FNET_CH:2 * FNET_CH].astype(BF16)

    o_ref[0] = (_dot(wd_ref[...], z_ref[b]) * scale).astype(o_ref.dtype)


def _dft_tables(n):
    k = np.arange(n, dtype=np.float64)
    ang = 2.0 * np.pi * np.outer(k, k) / n
    return np.cos(ang), np.sin(ang)


def _fnet(p, name="fnet"):
    bsz, n, width = p.shape
    cn, sn = _dft_tables(n)
    wd = jnp.asarray(np.concatenate([cn, -sn], axis=1), BF16)
    cg, sg = _dft_tables(FNET_GROUP_CH)
    eye = np.eye(FNET_GROUPS)
    cs = jnp.asarray(np.concatenate([np.kron(eye, cg), np.kron(eye, sg)], axis=1), BF16)
    tm = _tile(n, 512)
    scale = 1.0 / math.sqrt(n * FNET_GROUP_CH)
    return pl.pallas_call(
        functools.partial(_fnet_kernel, scale=scale),
        grid=(n // tm, bsz),
        in_specs=[
            pl.BlockSpec((1, n, FNET_CH), lambda i, b: (b, 0, (width - FNET_CH) // FNET_CH)),
            pl.BlockSpec((FNET_CH, 2 * FNET_CH), lambda i, b: (0, 0)),
            pl.BlockSpec((tm, 2 * n), lambda i, b: (i, 0)),
        ],
        out_specs=pl.BlockSpec((1, tm, FNET_CH), lambda i, b: (b, i, 0)),
        out_shape=jax.ShapeDtypeStruct((bsz, n, FNET_CH), BF16),
        scratch_shapes=[pltpu.VMEM((bsz, 2 * n, FNET_CH), BF16)],
        compiler_params=_params("arbitrary", "arbitrary"),
        name=name,
    )(p, cs, wd)


def _rot_cols(w):
    a, b, c, d = jnp.split(w, 4, axis=-1)
    return jnp.concatenate([-b, a, -d, c], axis=-1)


def _even_weights(w_in, w_uq, w_ukv, w_gfw, b_gfw, w_gbw, b_gbw):
    d = w_in.shape[0]
    sizes = [MLA_Q_RANK, MLA_KV_RANK, MLA_ROPE, GLA_HEADS * GLA_DK, GLA_HEADS * GLA_DK, GLA_HEADS * GLA_DV,
             GLA_GATE_RANK, GLA_GATE_RANK, GLA_HEADS * GLA_DV]
    qc, kvc, kr, gq, gk, gv, glf, glb, gr = jnp.split(w_in, np.cumsum(sizes)[:-1].tolist(), axis=1)
    tail_pad = jnp.zeros((d, 128 - 2 * MLA_ROPE - 2 * GLA_GATE_RANK), w_in.dtype)
    w_p = jnp.concatenate([qc, gq * (GLA_DK ** -0.5), gk, kvc, kr, _rot_cols(kr), glf, glb, tail_pad, gv, gr],
                          axis=1).astype(BF16)
    uq = w_uq.reshape(MLA_Q_RANK, MLA_HEADS, MLA_NOPE + MLA_ROPE) * (MLA_SCALE * math.log2(math.e))
    qn = uq[:, :, :MLA_NOPE].reshape(MLA_Q_RANK, -1)
    qr = uq[:, :, MLA_NOPE:]
    w_q = jnp.concatenate([qn, qr.reshape(MLA_Q_RANK, -1), _rot_cols(qr).reshape(MLA_Q_RANK, -1)],
                          axis=1).astype(BF16)
    ukv = w_ukv.reshape(MLA_KV_RANK, MLA_HEADS, MLA_NOPE + MLA_V)
    w_kv = jnp.concatenate([ukv[:, :, :MLA_NOPE].reshape(MLA_KV_RANK, -1),
                            ukv[:, :, MLA_NOPE:].reshape(MLA_KV_RANK, -1)], axis=1).astype(BF16)
    hk = GLA_HEADS * GLA_DK
    w_gate = jnp.zeros((128, 2 * hk), F32)
    w_gate = w_gate.at[T_GF:T_GF + GLA_GATE_RANK, :hk].set(w_gfw)
    w_gate = w_gate.at[T_GB:T_GB + GLA_GATE_RANK, hk:].set(w_gbw).astype(BF16)
    b_gate = jnp.concatenate([b_gfw, b_gbw]).reshape(1, 2 * hk)
    return w_p, w_q, w_kv, w_gate, b_gate


def _ffn_weights(w_in, dw_w, dw_b, w_out):
    d, f2 = w_in.shape
    f = f2 // 2
    nf = f // FFN_CHUNK
    wg = w_in[:, :f].astype(BF16)
    wv = w_in[:, f:].astype(BF16)
    taps = jnp.concatenate([dw_w, dw_b[None], jnp.zeros((8 - dw_w.shape[0] - 1, f), F32)], axis=0)
    cwb = taps.reshape(8, nf, FFN_CHUNK).transpose(1, 0, 2)
    wo = w_out.reshape(nf, FFN_CHUNK, d).astype(BF16)
    return wg, wv, cwb, wo


def _rope_tables(n):
    rows = n // GRID_W
    row = jnp.repeat(jnp.arange(rows), GRID_W).astype(F32)
    col = jnp.tile(jnp.arange(GRID_W), rows).astype(F32)
    half = MLA_ROPE // 2
    inv = ROPE_BASE ** (-jnp.arange(0, half, 2, dtype=F32) / half)
    ar = row[:, None] * inv
    ac = col[:, None] * inv
    ang = jnp.concatenate([ar, ar, ac, ac], axis=-1)
    return jnp.cos(ang), jnp.sin(ang)


def kernel(x, c, ctx, c_ctx, mod_w, mod_b, pre_mix_g, post_mix_g, pre_ffn_g, post_ffn_g, ev_in_w, mla_q_norm, mla_kv_norm, mla_w_uq, mla_w_ukv, gla_w_gate_fw, gla_b_gate_fw, gla_w_gate_bw, gla_b_gate_bw, gla_o_norm, ev_out_w, od_in_w, conf_dw_w, conf_dw_b, conf_ln_g, conf_ln_b, od_out_w, ffn_in_w, ffn_dw_w, ffn_dw_b, ffn_out_w):
    bsz, n, d = x.shape
    n_ctx = ctx.shape[1]
    depth = mod_w.shape[0]
    last_ctx_reader = ((depth - 1) // 2) * 2

    rows = -(-(bsz + 1) // 8) * 8
    cvec = jnp.concatenate([c, c_ctx[None], jnp.zeros((rows - bsz - 1, d), F32)], axis=0)
    mod = _modulation(cvec, mod_w, mod_b)
    table = mod.reshape(depth * rows * 6, 1, d)

    cos, sin = _rope_tables(n)
    cosq, sinq = jnp.tile(cos, (1, MLA_HEADS)), jnp.tile(sin, (1, MLA_HEADS))
    ones_q = jnp.ones((n_ctx, MLA_HEADS * MLA_ROPE), F32)
    zeros_q = jnp.zeros((n_ctx, MLA_HEADS * MLA_ROPE), F32)

    def flat(t):
        return t.reshape(1, bsz * n_ctx, t.shape[-1])

    def unflat(t):
        return t.reshape(bsz, n_ctx, t.shape[-1])

    x_lat, x_ctx = x, ctx
    for l in range(depth):
        need_ctx = l < last_ctx_reader
        use_ctx = need_ctx or (l % 2 == 0)
        i = l // 2
        sh1, sc1, g1, sh2, sc2, g2 = (_Mod(table, l * rows * 6 + k, 6) for k in range(6))
        csh1, csc1, cg1, csh2, csc2, cg2 = (_Mod(table, (l * rows + bsz) * 6 + k, 0) for k in range(6))

        if l % 2 == 0:
            w_p, w_q, w_kv, w_gate, b_gate = _even_weights(
                ev_in_w[i], mla_w_uq[i], mla_w_ukv[i], gla_w_gate_fw[i], gla_b_gate_fw[i],
                gla_w_gate_bw[i], gla_b_gate_bw[i])
            w_o = ev_out_w[i].astype(BF16)
            k1 = MLA_HEADS * MLA_V
            p_lat, q_lat, kv_lat = _even_in(x_lat, pre_mix_g[l], sc1, sh1, w_p, mla_q_norm[i], w_q,
                                            mla_kv_norm[i], w_kv, name="even_in_lat")
            p_ctx, q_ctx, kv_ctx = (unflat(t) for t in _even_in(
                flat(x_ctx), pre_mix_g[l], csc1, csh1, w_p, mla_q_norm[i], w_q, mla_kv_norm[i], w_kv,
                name="even_in_ctx"))
            a_lat = _mla(q_lat, cosq, sinq, kv_ctx, p_ctx, kv_lat, p_lat, cos, sin, name="mla_lat")
            g_ctx, g_lat = _gla(p_ctx, p_lat, w_gate, b_gate, gla_o_norm[i], need_ctx)
            x_lat = _mnr(a_lat, g_lat, w_o[:k1], w_o[k1:], x_lat, g1, post_mix_g[l], name="even_out_lat")
            if need_ctx:
                a_ctx = _mla(q_ctx, ones_q, zeros_q, kv_ctx, p_ctx, name="mla_ctx")
                x_ctx = unflat(_mnr(flat(a_ctx), flat(g_ctx), w_o[:k1], w_o[k1:], flat(x_ctx), cg1,
                                    post_mix_g[l], name="even_out_ctx"))
        else:
            w_p = od_in_w[i].astype(BF16)
            w_o = od_out_w[i].astype(BF16)
            w_dw = jnp.concatenate([conf_dw_w[i], jnp.zeros((1, CONV_CH), F32)], axis=0)
            streams = [(x_lat, sc1, sh1, g1, "lat")]
            if need_ctx:
                streams.append((x_ctx, csc1, csh1, cg1, "ctx"))
            outs = []
            for xs, sc, sh, gate, tag in streams:
                per_batch = sc.stride > 0
                xin = xs if per_batch else flat(xs)
                p = _nmm(xin, pre_mix_g[l], w_p, sc, sh, name="odd_in_" + tag)
                p = p if per_batch else unflat(p)
                u = _conformer(p, w_dw, conf_dw_b[i], conf_ln_g[i], conf_ln_b[i], name="conformer_" + tag)
                fm = _fnet(p, name="fnet_" + tag)
                if not per_batch:
                    u, fm = flat(u), flat(fm)
                y = _mnr(u, fm, w_o[:CONV_CH], w_o[CONV_CH:], xin, gate, post_mix_g[l], name="odd_out_" + tag)
                outs.append(y if per_batch else unflat(y))
            x_lat = outs[0]
            if need_ctx:
                x_ctx = outs[1]

        wg, wv, cwb, wo = _ffn_weights(ffn_in_w[l], ffn_dw_w[l], ffn_dw_b[l], ffn_out_w[l])
        x_lat = _ffn(x_lat, pre_ffn_g[l], sc2, sh2, g2, post_ffn_g[l], wg, wv, cwb, wo, n, name="ffn_lat")
        if need_ctx:
            x_ctx = unflat(_ffn(flat(x_ctx), pre_ffn_g[l], csc2, csh2, cg2, post_ffn_g[l], wg, wv, cwb, wo,
                                n_ctx, name="ffn_ctx"))
    return x_lat
```

```python
import functools
import math
from typing import NamedTuple

import numpy as np
import jax
import jax.numpy as jnp
from jax import lax
from jax.experimental import pallas as pl
from jax.experimental.pallas import tpu as pltpu

F32 = jnp.float32
BF16 = jnp.bfloat16

GRID_W = 64
ROPE_BASE = 10000.0
MLA_HEADS = 8
MLA_Q_RANK = 256
MLA_KV_RANK = 128
MLA_NOPE = 64
MLA_ROPE = 32
MLA_V = 64
MLA_SCALE = (MLA_NOPE + MLA_ROPE) ** -0.5
GLA_HEADS = 4
GLA_DK = 64
GLA_DV = 128
GLA_GATE_RANK = 16
GLA_GATE_NORM = 16.0
GLA_CHUNK = 64
CONV_CH = 768
CONV_W = 31
FNET_GROUPS = 4
FNET_GROUP_CH = 64
FNET_CH = FNET_GROUPS * FNET_GROUP_CH
FFN_CHUNK = 256
EPS = 1e-6

P_QC, P_GQ, P_GK = 0, 256, 512
P_KVC, P_TAIL = 768, 896
P_GV, P_GR = 1024, 1536
P_WIDTH = 2048
T_KR, T_KRROT, T_GF, T_GB = 0, 32, 64, 80

NT_DIMS = (((1,), (1,)), ((), ()))
TN_DIMS = (((0,), (0,)), ((), ()))

VMEM_LIMIT = 56 * 1024 * 1024


def _params(*sem):
    return pltpu.CompilerParams(dimension_semantics=sem, vmem_limit_bytes=VMEM_LIMIT)


def _rms(xf, gain):
    return xf * lax.rsqrt(jnp.mean(xf * xf, axis=-1, keepdims=True) + EPS) * gain


def _sigmoid(x):
    return 1.0 / (1.0 + jnp.exp(-x))


def _dot(a, b):
    return jnp.dot(a, b, preferred_element_type=F32)


def _tile(n, want):
    t = min(n, want)
    assert n % t == 0, (n, want)
    return t


class _Mod(NamedTuple):
    table: jax.Array
    first: int
    stride: int


def _mod_spec(m):
    return pl.BlockSpec((1, 1, m.table.shape[2]), lambda b, i: (m.first + b * m.stride, 0, 0))


def _mod_kernel(c_ref, w_ref, b_ref, o_ref):
    c = c_ref[...]
    s = c * _sigmoid(c)
    o_ref[0] = _dot(s.astype(BF16), w_ref[0].astype(BF16)) + b_ref[0]


def _modulation(cvec, mod_w, mod_b):
    depth, d, n6 = mod_w.shape
    rows = cvec.shape[0]
    tn = _tile(n6, 1536)
    return pl.pallas_call(
        _mod_kernel,
        grid=(depth, n6 // tn),
        in_specs=[
            pl.BlockSpec((rows, d), lambda l, j: (0, 0)),
            pl.BlockSpec((1, d, tn), lambda l, j: (l, 0, j)),
            pl.BlockSpec((1, 1, tn), lambda l, j: (l, 0, j)),
        ],
        out_specs=pl.BlockSpec((1, rows, tn), lambda l, j: (l, 0, j)),
        out_shape=jax.ShapeDtypeStruct((depth, rows, n6), F32),
        compiler_params=_params("parallel", "parallel"),
        name="modulation",
    )(cvec, mod_w, mod_b.reshape(depth, 1, n6))


def _nmm_kernel(x_ref, g_ref, sc_ref, sh_ref, w_ref, o_ref, *, col_w):
    h = _rms(x_ref[0], g_ref[...]) * (1.0 + sc_ref[0]) + sh_ref[0]
    hb = h.astype(BF16)
    for j in range(o_ref.shape[2] // col_w):
        cols = slice(j * col_w, (j + 1) * col_w)
        o_ref[0, :, cols] = _dot(hb, w_ref[:, cols]).astype(o_ref.dtype)


def _nmm(x, gain, w, sc, sh, tm=1024, name="nmm"):
    bsz, n, k = x.shape
    nout = w.shape[1]
    tm = _tile(n, tm)
    col_w = next(cw for cw in (512, 256, 128) if nout % cw == 0)
    return pl.pallas_call(
        functools.partial(_nmm_kernel, col_w=col_w),
        grid=(bsz, n // tm),
        in_specs=[pl.BlockSpec((1, tm, k), lambda b, i: (b, i, 0)),
                  pl.BlockSpec((1, k), lambda b, i: (0, 0)),
                  _mod_spec(sc), _mod_spec(sh),
                  pl.BlockSpec((k, nout), lambda b, i: (0, 0))],
        out_specs=pl.BlockSpec((1, tm, nout), lambda b, i: (b, i, 0)),
        out_shape=jax.ShapeDtypeStruct((bsz, n, nout), BF16),
        compiler_params=_params("parallel", "parallel"),
        name=name,
    )(x, gain.reshape(1, k), sc.table, sh.table, w)


def _even_in_kernel(x_ref, g_ref, sc_ref, sh_ref, w_ref, qn_ref, wq_ref, kvn_ref, wkv_ref,
                    p_ref, q_ref, kv_ref):
    h = _rms(x_ref[0], g_ref[...]) * (1.0 + sc_ref[0]) + sh_ref[0]
    hb = h.astype(BF16)

    def up(lat, gain_ref, wu_ref, out_ref):
        lb = _rms(lat, gain_ref[...]).astype(BF16)
        for j in range(out_ref.shape[2] // 512):
            cols = slice(j * 512, (j + 1) * 512)
            out_ref[0, :, cols] = _dot(lb, wu_ref[:, cols]).astype(out_ref.dtype)

    for j in range(P_WIDTH // 512):
        c0 = j * 512
        pj = _dot(hb, w_ref[:, c0:c0 + 512])
        p_ref[0, :, c0:c0 + 512] = pj.astype(p_ref.dtype)
        if c0 <= P_QC < c0 + 512:
            up(pj[:, P_QC - c0:P_QC - c0 + MLA_Q_RANK], qn_ref, wq_ref, q_ref)
        if c0 <= P_KVC < c0 + 512:
            up(pj[:, P_KVC - c0:P_KVC - c0 + MLA_KV_RANK], kvn_ref, wkv_ref, kv_ref)


def _even_in(x, gain, sc, sh, w_p, q_norm, w_q, kv_norm, w_kv, tm=1024, name="even_in"):
    bsz, n, d = x.shape
    tm = _tile(n, tm)

    def whole(a):
        return pl.BlockSpec(a.shape, lambda b, i: (0, 0))

    args = [gain.reshape(1, d), w_p, q_norm.reshape(1, -1), w_q, kv_norm.reshape(1, -1), w_kv]
    widths = (w_p.shape[1], w_q.shape[1], w_kv.shape[1])
    return pl.pallas_call(
        _even_in_kernel,
        grid=(bsz, n // tm),
        in_specs=[pl.BlockSpec((1, tm, d), lambda b, i: (b, i, 0)), whole(args[0]), _mod_spec(sc), _mod_spec(sh)]
        + [whole(a) for a in args[1:]],
        out_specs=[pl.BlockSpec((1, tm, w), lambda b, i: (b, i, 0)) for w in widths],
        out_shape=[jax.ShapeDtypeStruct((bsz, n, w), BF16) for w in widths],
        compiler_params=_params("parallel", "parallel"),
        name=name,
    )(x, args[0], sc.table, sh.table, *args[1:])


def _mnr_kernel(a1_ref, a2_ref, w1_ref, w2_ref, x_ref, gate_ref, pg_ref, o_ref):
    y = _dot(a1_ref[0], w1_ref[...]) + _dot(a2_ref[0], w2_ref[...])
    o_ref[0] = x_ref[0] + gate_ref[0] * _rms(y, pg_ref[...])


def _mnr(a1, a2, w1, w2, x, gate, post_g, tm=1024, name="mnr"):
    bsz, n, d = x.shape
    k1, k2 = a1.shape[2], a2.shape[2]
    tm = _tile(n, tm)
    return pl.pallas_call(
        _mnr_kernel,
        grid=(bsz, n // tm),
        in_specs=[
            pl.BlockSpec((1, tm, k1), lambda b, i: (b, i, 0)),
            pl.BlockSpec((1, tm, k2), lambda b, i: (b, i, 0)),
            pl.BlockSpec((k1, d), lambda b, i: (0, 0)),
            pl.BlockSpec((k2, d), lambda b, i: (0, 0)),
            pl.BlockSpec((1, tm, d), lambda b, i: (b, i, 0)),
            _mod_spec(gate),
            pl.BlockSpec((1, d), lambda b, i: (0, 0)),
        ],
        out_specs=pl.BlockSpec((1, tm, d), lambda b, i: (b, i, 0)),
        out_shape=jax.ShapeDtypeStruct((bsz, n, d), F32),
        compiler_params=_params("parallel", "parallel"),
        name=name,
    )(a1, a2, w1, w2, x, gate.table, post_g.reshape(1, d))


def _ffn_kernel(x_ref, xp_ref, xn_ref, g_ref, sc_ref, sh_ref, gate_ref, pg_ref,
                wg_ref, wv_ref, cw_ref, wo_ref, o_ref, xh_ref, acc_ref, g0_ref, g1_ref, v0_ref, v1_ref, *, seq_len):
    tm = x_ref.shape[1]
    n_chunks, fc, _ = wo_ref.shape
    i = pl.program_id(1)
    gain = g_ref[...]
    scale = 1.0 + sc_ref[0]
    shift = sh_ref[0]

    def modulated(rows):
        return (_rms(rows, gain) * scale + shift).astype(BF16)

    xh_ref[0:tm] = modulated(x_ref[0])
    xh_ref[tm:tm + 16] = modulated(jnp.concatenate([xp_ref[0], xn_ref[0]], axis=0))

    local = lax.broadcasted_iota(jnp.int32, (tm, 1), 0)
    pos = (i * tm + local) % seq_len
    seq_first = pos == 0
    seq_last = pos == seq_len - 1
    tile_first = local == 0
    tile_last = local == tm - 1
    acc_ref[...] = jnp.zeros_like(acc_ref)
    slots = ((g0_ref, v0_ref), (g1_ref, v1_ref))

    def project(c, slot):
        gs_ref, vs_ref = slots[slot]
        cols = pl.ds(pl.multiple_of(c * fc, fc), fc)
        gs_ref[...] = _dot(xh_ref[...], wg_ref[:, cols])
        vs_ref[...] = _dot(xh_ref[0:tm], wv_ref[:, cols])

    def mix(c, slot):
        gs_ref, vs_ref = slots[slot]
        gm = gs_ref[0:tm]
        g_prev = jnp.where(tile_first, gs_ref[tm + 7:tm + 8], pltpu.roll(gm, 1, axis=0))
        g_prev = jnp.where(seq_first, 0.0, g_prev)
        g_next = jnp.where(tile_last, gs_ref[tm + 8:tm + 9], pltpu.roll(gm, tm - 1, axis=0))
        g_next = jnp.where(seq_last, 0.0, g_next)
        cw = cw_ref[c]
        y = g_prev * cw[0:1] + gm * cw[1:2] + g_next * cw[2:3] + cw[3:4]
        u = 0.5 * y * (1.0 + lax.erf(y * (1.0 / math.sqrt(2.0)))) * vs_ref[...]
        acc_ref[...] += _dot(u.astype(BF16), wo_ref[c])

    project(0, 0)

    def body(j, carry):
        c = 2 * j
        project(c + 1, 1)
        mix(c, 0)
        project(c + 2, 0)
        mix(c + 1, 1)
        return carry

    lax.fori_loop(0, (n_chunks - 1) // 2, body, 0)
    if n_chunks % 2 == 0:
        project(n_chunks - 1, 1)
        mix(n_chunks - 2, 0)
        mix(n_chunks - 1, 1)
    else:
        mix(n_chunks - 1, 0)
    o_ref[0] = x_ref[0] + gate_ref[0] * _rms(acc_ref[...], pg_ref[...])


def _ffn(x, pre_g, sc, sh, gate, post_g, wg, wv, cwb, wo, seq_len, tm=1024, name="ffn"):
    bsz, n, d = x.shape
    nf, fc, _ = wo.shape
    f = nf * fc
    tm = _tile(n, tm)
    nb8 = n // 8
    vec_spec = pl.BlockSpec((1, d), lambda b, i: (0, 0))
    return pl.pallas_call(
        functools.partial(_ffn_kernel, seq_len=seq_len),
        grid=(bsz, n // tm),
        in_specs=[
            pl.BlockSpec((1, tm, d), lambda b, i: (b, i, 0)),
            pl.BlockSpec((1, 8, d), lambda b, i: (b, jnp.maximum(i * (tm // 8) - 1, 0), 0)),
            pl.BlockSpec((1, 8, d), lambda b, i: (b, jnp.minimum((i + 1) * (tm // 8), nb8 - 1), 0)),
            vec_spec, _mod_spec(sc), _mod_spec(sh), _mod_spec(gate), vec_spec,
            pl.BlockSpec((d, f), lambda b, i: (0, 0), pipeline_mode=pl.Buffered(1)),
            pl.BlockSpec((d, f), lambda b, i: (0, 0), pipeline_mode=pl.Buffered(1)),
            pl.BlockSpec((nf, 8, fc), lambda b, i: (0, 0, 0), pipeline_mode=pl.Buffered(1)),
            pl.BlockSpec((nf, fc, d), lambda b, i: (0, 0, 0), pipeline_mode=pl.Buffered(1)),
        ],
        out_specs=pl.BlockSpec((1, tm, d), lambda b, i: (b, i, 0)),
        out_shape=jax.ShapeDtypeStruct((bsz, n, d), F32),
        scratch_shapes=[pltpu.VMEM((tm + 16, d), BF16), pltpu.VMEM((tm, d), F32),
                        pltpu.VMEM((tm + 16, fc), F32), pltpu.VMEM((tm + 16, fc), F32),
                        pltpu.VMEM((tm, fc), F32), pltpu.VMEM((tm, fc), F32)],
        compiler_params=_params("parallel", "parallel"),
        name=name,
    )(x, x, x, pre_g.reshape(1, d), sc.table, sh.table, gate.table, post_g.reshape(1, d), wg, wv, cwb, wo)


def _mla_kernel(*refs, n_ctx, n_lat):
    if n_lat:
        (q_ref, cq_ref, sq_ref, kvc_ref, tc_ref, kvl_ref, tl_ref, ck_ref, sk_ref,
         o_ref, kc_ref, vt_ref, ot_ref, s0_ref, s1_ref, p0_ref, p1_ref, qh_ref, oe_ref) = refs
    else:
        (q_ref, cq_ref, sq_ref, kvc_ref, tc_ref,
         o_ref, kc_ref, vt_ref, ot_ref, s0_ref, s1_ref, p0_ref, p1_ref, qh_ref, oe_ref) = refs
    hv = MLA_HEADS * MLA_NOPE
    _, n_blocks, vrows, rb = vt_ref.shape

    @pl.when(pl.program_id(1) == 0)
    def _build_keys():
        r = lax.broadcasted_iota(jnp.int32, (vrows, MLA_V), 0)
        c = lax.broadcasted_iota(jnp.int32, (vrows, MLA_V), 1)
        sel = (r == c).astype(BF16)

        def fill(kv_ref, t_ref, off, n, roped):
            tail = t_ref[0].astype(F32)
            kr = tail[:, T_KR:T_KR + MLA_ROPE]
            if roped:
                kr = kr * ck_ref[...] + tail[:, T_KRROT:T_KRROT + MLA_ROPE] * sk_ref[...]
            kr = kr.astype(BF16)
            is_v = lax.broadcasted_iota(jnp.int32, (vrows, rb), 0) < MLA_V
            for h in range(MLA_HEADS):
                kc_ref[h, off:off + n, 0:MLA_NOPE] = kv_ref[0, :, h * MLA_NOPE:(h + 1) * MLA_NOPE]
                kc_ref[h, off:off + n, MLA_NOPE:MLA_NOPE + MLA_ROPE] = kr
                kc_ref[h, off:off + n, MLA_NOPE + MLA_ROPE:128] = jnp.zeros((n, 128 - MLA_NOPE - MLA_ROPE), BF16)
                vt = lax.dot_general(sel, kv_ref[0, :, hv + h * MLA_V:hv + (h + 1) * MLA_V], NT_DIMS,
                                     preferred_element_type=F32)
                for jb in range(n // rb):
                    vt_ref[h, off // rb + jb] = jnp.where(is_v, vt[:, jb * rb:(jb + 1) * rb], 1.0).astype(BF16)

        fill(kvc_ref, tc_ref, 0, n_ctx, False)
        if n_lat:
            fill(kvl_ref, tl_ref, n_ctx, n_lat, True)

    tq = q_ref.shape[1]
    hr = MLA_HEADS * MLA_ROPE
    qr = (q_ref[0, :, hv:hv + hr].astype(F32) * cq_ref[...]
          + q_ref[0, :, hv + hr:hv + 2 * hr].astype(F32) * sq_ref[...]).astype(BF16)
    pad = jnp.zeros((tq, 128 - MLA_NOPE - MLA_ROPE), BF16)
    for h in range(MLA_HEADS):
        qh_ref[h] = jnp.concatenate([q_ref[0, :, h * MLA_NOPE:(h + 1) * MLA_NOPE],
                                     qr[:, h * MLA_ROPE:(h + 1) * MLA_ROPE], pad], axis=1)

    s_refs = (s0_ref, s1_ref)
    p_refs = (p0_ref, p1_ref)
    maxes = [None] * MLA_HEADS
    for t in range(MLA_HEADS + 2):
        h_s, h_e, h_v = t, t - 1, t - 2
        do_s, do_e, do_v = h_s < MLA_HEADS, 0 <= h_e < MLA_HEADS, 0 <= h_v
        if do_v:
            oe_ref[...] = jnp.zeros_like(oe_ref)

        def block(j, m, h_s=h_s, h_e=h_e, h_v=h_v, do_s=do_s, do_e=do_e, do_v=do_v):
            rows = pl.ds(pl.multiple_of(j * rb, rb), rb)
            if do_s:
                blk = lax.dot_general(kc_ref[h_s, rows, :], qh_ref[h_s], NT_DIMS, preferred_element_type=F32)
                s_refs[h_s % 2][rows, :] = blk
                m = jnp.maximum(m, jnp.max(blk, axis=0, keepdims=True))
            if do_e:
                p_refs[h_e % 2][rows, :] = jnp.exp2(s_refs[h_e % 2][rows, :] - maxes[h_e]).astype(BF16)
            if do_v:
                oe_ref[...] += _dot(vt_ref[h_v, j], p_refs[h_v % 2][rows, :])
            return m

        m = lax.fori_loop(0, n_blocks, block, jnp.full((1, tq), -jnp.inf, F32), unroll=True)
        if do_s:
            maxes[h_s] = m
        if do_v:
            ot_ref[h_v * MLA_V:(h_v + 1) * MLA_V, :] = oe_ref[0:MLA_V] / oe_ref[MLA_V:MLA_V + 1]
    o_ref[0] = ot_ref[...].T.astype(o_ref.dtype)


def _mla(q, cosq, sinq, kv_c, p_c, kv_l=None, p_l=None, cosk=None, sink=None, tq=512, name="mla"):
    bsz, nq, qw = q.shape
    n_ctx = kv_c.shape[1]
    n_lat = 0 if kv_l is None else kv_l.shape[1]
    nk = n_ctx + n_lat
    tq = _tile(nq, tq)
    rb = math.gcd(math.gcd(n_ctx, n_lat), 256)
    vrows = MLA_V + 16
    hr = MLA_HEADS * MLA_ROPE
    tail_blk = P_TAIL // 128
    in_specs = [
        pl.BlockSpec((1, tq, qw), lambda b, i: (b, i, 0)),
        pl.BlockSpec((tq, hr), lambda b, i: (i, 0)),
        pl.BlockSpec((tq, hr), lambda b, i: (i, 0)),
        pl.BlockSpec((1, n_ctx, kv_c.shape[2]), lambda b, i: (b, 0, 0)),
        pl.BlockSpec((1, n_ctx, 128), lambda b, i: (b, 0, tail_blk)),
    ]
    args = [q, cosq, sinq, kv_c, p_c]
    if n_lat:
        in_specs += [
            pl.BlockSpec((1, n_lat, kv_l.shape[2]), lambda b, i: (b, 0, 0)),
            pl.BlockSpec((1, n_lat, 128), lambda b, i: (b, 0, tail_blk)),
            pl.BlockSpec((n_lat, MLA_ROPE), lambda b, i: (0, 0)),
            pl.BlockSpec((n_lat, MLA_ROPE), lambda b, i: (0, 0)),
        ]
        args += [kv_l, p_l, cosk, sink]
    return pl.pallas_call(
        functools.partial(_mla_kernel, n_ctx=n_ctx, n_lat=n_lat),
        grid=(bsz, nq // tq),
        in_specs=in_specs,
        out_specs=pl.BlockSpec((1, tq, MLA_HEADS * MLA_V), lambda b, i: (b, i, 0)),
        out_shape=jax.ShapeDtypeStruct((bsz, nq, MLA_HEADS * MLA_V), BF16),
        scratch_shapes=[pltpu.VMEM((MLA_HEADS, nk, 128), BF16),
                        pltpu.VMEM((MLA_HEADS, nk // rb, vrows, rb), BF16),
                        pltpu.VMEM((MLA_HEADS * MLA_V, tq), F32),
                        pltpu.VMEM((nk, tq), F32), pltpu.VMEM((nk, tq), F32),
                        pltpu.VMEM((nk, tq), BF16), pltpu.VMEM((nk, tq), BF16),
                        pltpu.VMEM((MLA_HEADS, tq, 128), BF16),
                        pltpu.VMEM((vrows, tq), F32)],
        compiler_params=_params("parallel", "arbitrary"),
        name=name,
    )(*args)


def _gla_kernel(*refs, want_ctx):
    (qc_ref, kc_ref, tc_ref, vc_ref, grc_ref, ql_ref, kl_ref, tl_ref, vl_ref, grl_ref,
     wg_ref, bg_ref, on_ref) = refs[:13]
    if want_ctx:
        oc_ref, ol_ref = refs[13:15]
        scratch = refs[15:]
    else:
        oc_ref, ol_ref = None, refs[13]
        scratch = refs[14:]
    lgc_ref, lgl_ref, accc_ref, accl_ref, st_ref = scratch
    cs = GLA_CHUNK
    hk = GLA_HEADS * GLA_DK

    shift = cs.bit_length() - 1
    for t_ref, lg_ref in ((tc_ref, lgc_ref), (tl_ref, lgl_ref)):
        n = t_ref.shape[1]
        tr = _tile(n, 256)
        r = lax.broadcasted_iota(jnp.int32, (tr, tr), 0)
        c = lax.broadcasted_iota(jnp.int32, (tr, tr), 1)
        same_chunk = (r >> shift) == (c >> shift)
        tri_fw = (same_chunk & (r >= c)).astype(BF16)
        tri_bw = (same_chunk & (r <= c)).astype(BF16)

        def cum_body(i, carry, t_ref=t_ref, lg_ref=lg_ref, tr=tr, tri_fw=tri_fw, tri_bw=tri_bw):
            rows = pl.ds(pl.multiple_of(i * tr, tr), tr)
            z = _dot(t_ref[0, rows, :], wg_ref[...]) + bg_ref[...]
            lg = (jnp.minimum(z, 0.0) - jnp.log(1.0 + jnp.exp(-jnp.abs(z)))) * (1.0 / GLA_GATE_NORM)
            hi = lg.astype(BF16)
            rest = lg - hi.astype(F32)
            mid = rest.astype(BF16)
            lo = (rest - mid.astype(F32)).astype(BF16)
            for tri, cols in ((tri_fw, slice(0, hk)), (tri_bw, slice(hk, 2 * hk))):
                lg_ref[rows, cols] = (_dot(tri, lo[:, cols]) + _dot(tri, mid[:, cols])) + _dot(tri, hi[:, cols])
            return carry

        lax.fori_loop(0, n // tr, cum_body, 0, unroll=2)

    br = math.gcd(math.gcd(qc_ref.shape[1], ql_ref.shape[1]), 256)
    cpb = br // cs
    row = lax.broadcasted_iota(jnp.int32, (br, br), 0)
    col = lax.broadcasted_iota(jnp.int32, (br, br), 1)
    same = (row >> shift) == (col >> shift)
    causal = (same & (row >= col), same & (row <= col))
    erow = lax.broadcasted_iota(jnp.int32, (cs, cs), 0)
    ecol = lax.broadcasted_iota(jnp.int32, (cs, cs), 1)
    eye = (erow == ecol).astype(F32)
    st_ref[...] = jnp.zeros_like(st_ref)

    def block_diag(x):
        wide = jnp.concatenate([x] * cpb, axis=1)
        return jnp.where(same, wide, jnp.zeros_like(wide))

    def block(q_ref, k_ref, v_ref, lg_ref, acc_ref, d, r0):
        rows = pl.ds(pl.multiple_of(r0, br), br)
        cum = lg_ref[rows, d * hk:(d + 1) * hk]
        cum3 = cum.reshape(cpb, cs, hk)
        edge = cum3[:, cs - 1:cs, :] if d == 0 else cum3[:, 0:1, :]
        tot = jnp.broadcast_to(edge, cum3.shape).reshape(br, hk)
        half = 0.5 * tot
        q = q_ref[0, rows, :].astype(F32)
        k = k_ref[0, rows, :].astype(F32)
        v = v_ref[0, rows, :]
        qt = (q * jnp.exp(cum - half)).astype(BF16)
        kt = (k * jnp.exp(half - cum)).astype(BF16)
        qs = (q * jnp.exp(cum)).astype(BF16)
        kd = (k * jnp.exp(tot - cum)).astype(BF16)
        order = range(cpb) if d == 0 else range(cpb - 1, -1, -1)
        for h in range(GLA_HEADS):
            ks = slice(h * GLA_DK, (h + 1) * GLA_DK)
            vs = slice(h * GLA_DV, (h + 1) * GLA_DV)
            a = lax.dot_general(qt[:, ks], kt[:, ks], NT_DIMS, preferred_element_type=F32)
            o = _dot(jnp.where(causal[d], a, 0.0).astype(BF16), v[:, vs])
            u = lax.dot_general(block_diag(kd[:, ks]), v[:, vs], TN_DIMS, preferred_element_type=F32)
            st = st_ref[d, h]
            starts = [None] * cpb
            for c in order:
                starts[c] = st.astype(BF16)
                dec = jnp.exp(jnp.sum(eye * edge[c, :, ks], axis=1, keepdims=True))
                st = st * dec + u[c * cs:(c + 1) * cs]
            st_ref[d, h] = st
            acc_ref[d, rows, vs] = o + _dot(block_diag(qs[:, ks]), jnp.concatenate(starts, axis=0))

    def scan(q_ref, k_ref, v_ref, lg_ref, acc_ref):
        n_blocks = q_ref.shape[1] // br

        def body(i, carry):
            block(q_ref, k_ref, v_ref, lg_ref, acc_ref, 0, i * br)
            block(q_ref, k_ref, v_ref, lg_ref, acc_ref, 1, (n_blocks - 1 - i) * br)
            return carry

        lax.fori_loop(0, n_blocks, body, 0, unroll=2)

    def finish(acc_ref, gr_ref, o_ref):
        n = acc_ref.shape[1]
        tr = _tile(n, 256)

        def body(i, carry):
            rows = pl.ds(pl.multiple_of(i * tr, tr), tr)
            gr = gr_ref[0, rows, :].astype(F32)
            gate = gr * _sigmoid(gr)
            for h in range(GLA_HEADS):
                vs = slice(h * GLA_DV, (h + 1) * GLA_DV)
                o = acc_ref[0, rows, vs] + acc_ref[1, rows, vs]
                o_ref[0, rows, vs] = (_rms(o, on_ref[...]) * gate[:, vs]).astype(o_ref.dtype)
            return carry

        lax.fori_loop(0, n // tr, body, 0)

    scan(qc_ref, kc_ref, vc_ref, lgc_ref, accc_ref)
    scan(ql_ref, kl_ref, vl_ref, lgl_ref, accl_ref)
    if want_ctx:
        finish(accc_ref, grc_ref, oc_ref)
    finish(accl_ref, grl_ref, ol_ref)


def _gla(p_c, p_l, w_gate, b_gate, o_norm, want_ctx, name="gla"):
    bsz, n_ctx, _ = p_c.shape
    n_lat = p_l.shape[1]
    hk, hv = GLA_HEADS * GLA_DK, GLA_HEADS * GLA_DV

    def specs(n):
        return [
            pl.BlockSpec((1, n, hk), lambda b: (b, 0, P_GQ // hk)),
            pl.BlockSpec((1, n, hk), lambda b: (b, 0, P_GK // hk)),
            pl.BlockSpec((1, n, 128), lambda b: (b, 0, P_TAIL // 128)),
            pl.BlockSpec((1, n, hv), lambda b: (b, 0, P_GV // hv)),
            pl.BlockSpec((1, n, hv), lambda b: (b, 0, P_GR // hv)),
        ]

    in_specs = specs(n_ctx) + specs(n_lat) + [
        pl.BlockSpec((128, 2 * hk), lambda b: (0, 0)),
        pl.BlockSpec((1, 2 * hk), lambda b: (0, 0)),
        pl.BlockSpec((1, GLA_DV), lambda b: (0, 0)),
    ]
    out_specs = [pl.BlockSpec((1, n_lat, hv), lambda b: (b, 0, 0))]
    out_shape = [jax.ShapeDtypeStruct((bsz, n_lat, hv), BF16)]
    if want_ctx:
        out_specs.insert(0, pl.BlockSpec((1, n_ctx, hv), lambda b: (b, 0, 0)))
        out_shape.insert(0, jax.ShapeDtypeStruct((bsz, n_ctx, hv), BF16))
    outs = pl.pallas_call(
        functools.partial(_gla_kernel, want_ctx=want_ctx),
        grid=(bsz,),
        in_specs=in_specs,
        out_specs=out_specs,
        out_shape=out_shape,
        scratch_shapes=[
            pltpu.VMEM((n_ctx, 2 * hk), F32), pltpu.VMEM((n_lat, 2 * hk), F32),
            pltpu.VMEM((2, n_ctx, hv), F32), pltpu.VMEM((2, n_lat, hv), F32),
            pltpu.VMEM((2, GLA_HEADS, GLA_DK, GLA_DV), F32),
        ],
        compiler_params=_params("parallel"),
        name=name,
    )(p_c, p_c, p_c, p_c, p_c, p_l, p_l, p_l, p_l, p_l, w_gate, b_gate, o_norm.reshape(1, GLA_DV))
    return (outs[0], outs[1]) if want_ctx else (None, outs[0])


def _conf_kernel(a_ref, gt_ref, w_ref, b_ref, lg_ref, lb_ref, o_ref, u_ref):
    n = a_ref.shape[1]
    pad = (CONV_W - 1) // 2
    lead = 16
    u_ref[0:lead] = jnp.zeros((lead, CONV_CH), F32)
    u_ref[lead + n:lead + n + lead] = jnp.zeros((lead, CONV_CH), F32)
    u_ref[lead:lead + n] = a_ref[0].astype(F32) * _sigmoid(gt_ref[0].astype(F32))
    tr = _tile(n, 256)

    def body(i, carry):
        r0 = pl.multiple_of(i * tr, tr)
        ext = u_ref[pl.ds(r0, tr + 2 * lead), :]
        y = jnp.zeros((tr, CONV_CH), F32) + b_ref[...]
        for k in range(CONV_W):
            off = lead - pad + k
            y = y + pltpu.roll(ext, tr + 2 * lead - off, axis=0)[0:tr] * w_ref[k:k + 1, :]
        mu = jnp.mean(y, axis=-1, keepdims=True)
        yc = y - mu
        z = yc * lax.rsqrt(jnp.mean(yc * yc, axis=-1, keepdims=True) + EPS) * lg_ref[...] + lb_ref[...]
        o_ref[0, pl.ds(r0, tr), :] = (z * _sigmoid(z)).astype(o_ref.dtype)
        return carry

    lax.fori_loop(0, n // tr, body, 0)


def _conformer(p, w, b, ln_g, ln_b, name="conformer"):
    bsz, n, _ = p.shape
    vec = pl.BlockSpec((1, CONV_CH), lambda i: (0, 0))
    return pl.pallas_call(
        _conf_kernel,
        grid=(bsz,),
        in_specs=[
            pl.BlockSpec((1, n, CONV_CH), lambda i: (i, 0, 0)),
            pl.BlockSpec((1, n, CONV_CH), lambda i: (i, 0, 1)),
            pl.BlockSpec((CONV_W + 1, CONV_CH), lambda i: (0, 0)),
            vec, vec, vec,
        ],
        out_specs=pl.BlockSpec((1, n, CONV_CH), lambda i: (i, 0, 0)),
        out_shape=jax.ShapeDtypeStruct((bsz, n, CONV_CH), BF16),
        scratch_shapes=[pltpu.VMEM((n + 32, CONV_CH), F32)],
        compiler_params=_params("parallel"),
        name=name,
    )(p, p, w, b.reshape(1, CONV_CH), ln_g.reshape(1, CONV_CH), ln_b.reshape(1, CONV_CH))


def _fnet_kernel(f_ref, cs_ref, wd_ref, o_ref, z_ref, *, scale):
    n = f_ref.shape[1]
    b = pl.program_id(1)

    @pl.when(pl.program_id(0) == 0)
    def _channel_dft():
        zz = _dot(f_ref[0], cs_ref[...])
        z_ref[b, 0:n] = zz[:, 0:FNET_CH].astype(BF16)
        z_ref[b, n:2 * n] = zz[:, FNET_CH:2 * FNET_CH].astype(BF16)

    o_ref[0] = (_dot(wd_ref[...], z_ref[b]) * scale).astype(o_ref.dtype)


def _dft_tables(n):
    k = np.arange(n, dtype=np.float64)
    ang = 2.0 * np.pi * np.outer(k, k) / n
    return np.cos(ang), np.sin(ang)


def _fnet(p, name="fnet"):
    bsz, n, width = p.shape
    cn, sn = _dft_tables(n)
    wd = jnp.asarray(np.concatenate([cn, -sn], axis=1), BF16)
    cg, sg = _dft_tables(FNET_GROUP_CH)
    eye = np.eye(FNET_GROUPS)
    cs = jnp.asarray(np.concatenate([np.kron(eye, cg), np.kron(eye, sg)], axis=1), BF16)
    tm = _tile(n, 512)
    scale = 1.0 / math.sqrt(n * FNET_GROUP_CH)
    return pl.pallas_call(
        functools.partial(_fnet_kernel, scale=scale),
        grid=(n // tm, bsz),
        in_specs=[
            pl.BlockSpec((1, n, FNET_CH), lambda i, b: (b, 0, (width - FNET_CH) // FNET_CH)),
            pl.BlockSpec((FNET_CH, 2 * FNET_CH), lambda i, b: (0, 0)),
            pl.BlockSpec((tm, 2 * n), lambda i, b: (i, 0)),
        ],
        out_specs=pl.BlockSpec((1, tm, FNET_CH), lambda i, b: (b, i, 0)),
        out_shape=jax.ShapeDtypeStruct((bsz, n, FNET_CH), BF16),
        scratch_shapes=[pltpu.VMEM((bsz, 2 * n, FNET_CH), BF16)],
        compiler_params=_params("arbitrary", "arbitrary"),
        name=name,
    )(p, cs, wd)


def _rot_cols(w):
    a, b, c, d = jnp.split(w, 4, axis=-1)
    return jnp.concatenate([-b, a, -d, c], axis=-1)


def _even_weights(w_in, w_uq, w_ukv, w_gfw, b_gfw, w_gbw, b_gbw):
    d = w_in.shape[0]
    sizes = [MLA_Q_RANK, MLA_KV_RANK, MLA_ROPE, GLA_HEADS * GLA_DK, GLA_HEADS * GLA_DK, GLA_HEADS * GLA_DV,
             GLA_GATE_RANK, GLA_GATE_RANK, GLA_HEADS * GLA_DV]
    qc, kvc, kr, gq, gk, gv, glf, glb, gr = jnp.split(w_in, np.cumsum(sizes)[:-1].tolist(), axis=1)
    tail_pad = jnp.zeros((d, 128 - 2 * MLA_ROPE - 2 * GLA_GATE_RANK), w_in.dtype)
    w_p = jnp.concatenate([qc, gq * (GLA_DK ** -0.5), gk, kvc, kr, _rot_cols(kr), glf, glb, tail_pad, gv, gr],
                          axis=1).astype(BF16)
    uq = w_uq.reshape(MLA_Q_RANK, MLA_HEADS, MLA_NOPE + MLA_ROPE) * (MLA_SCALE * math.log2(math.e))
    qn = uq[:, :, :MLA_NOPE].reshape(MLA_Q_RANK, -1)
    qr = uq[:, :, MLA_NOPE:]
    w_q = jnp.concatenate([qn, qr.reshape(MLA_Q_RANK, -1), _rot_cols(qr).reshape(MLA_Q_RANK, -1)],
                          axis=1).astype(BF16)
    ukv = w_ukv.reshape(MLA_KV_RANK, MLA_HEADS, MLA_NOPE + MLA_V)
    w_kv = jnp.concatenate([ukv[:, :, :MLA_NOPE].reshape(MLA_KV_RANK, -1),
                            ukv[:, :, MLA_NOPE:].reshape(MLA_KV_RANK, -1)], axis=1).astype(BF16)
    hk = GLA_HEADS * GLA_DK
    w_gate = jnp.zeros((128, 2 * hk), F32)
    w_gate = w_gate.at[T_GF:T_GF + GLA_GATE_RANK, :hk].set(w_gfw)
    w_gate = w_gate.at[T_GB:T_GB + GLA_GATE_RANK, hk:].set(w_gbw).astype(BF16)
    b_gate = jnp.concatenate([b_gfw, b_gbw]).reshape(1, 2 * hk)
    return w_p, w_q, w_kv, w_gate, b_gate


def _ffn_weights(w_in, dw_w, dw_b, w_out):
    d, f2 = w_in.shape
    f = f2 // 2
    nf = f // FFN_CHUNK
    wg = w_in[:, :f].astype(BF16)
    wv = w_in[:, f:].astype(BF16)
    taps = jnp.concatenate([dw_w, dw_b[None], jnp.zeros((8 - dw_w.shape[0] - 1, f), F32)], axis=0)
    cwb = taps.reshape(8, nf, FFN_CHUNK).transpose(1, 0, 2)
    wo = w_out.reshape(nf, FFN_CHUNK, d).astype(BF16)
    return wg, wv, cwb, wo


def _rope_tables(n):
    rows = n // GRID_W
    row = jnp.repeat(jnp.arange(rows), GRID_W).astype(F32)
    col = jnp.tile(jnp.arange(GRID_W), rows).astype(F32)
    half = MLA_ROPE // 2
    inv = ROPE_BASE ** (-jnp.arange(0, half, 2, dtype=F32) / half)
    ar = row[:, None] * inv
    ac = col[:, None] * inv
    ang = jnp.concatenate([ar, ar, ac, ac], axis=-1)
    return jnp.cos(ang), jnp.sin(ang)


def kernel(x, c, ctx, c_ctx, mod_w, mod_b, pre_mix_g, post_mix_g, pre_ffn_g, post_ffn_g, ev_in_w, mla_q_norm, mla_kv_norm, mla_w_uq, mla_w_ukv, gla_w_gate_fw, gla_b_gate_fw, gla_w_gate_bw, gla_b_gate_bw, gla_o_norm, ev_out_w, od_in_w, conf_dw_w, conf_dw_b, conf_ln_g, conf_ln_b, od_out_w, ffn_in_w, ffn_dw_w, ffn_dw_b, ffn_out_w):
    bsz, n, d = x.shape
    n_ctx = ctx.shape[1]
    depth = mod_w.shape[0]
    last_ctx_reader = ((depth - 1) // 2) * 2

    rows = -(-(bsz + 1) // 8) * 8
    cvec = jnp.concatenate([c, c_ctx[None], jnp.zeros((rows - bsz - 1, d), F32)], axis=0)
    mod = _modulation(cvec, mod_w, mod_b)
    table = mod.reshape(depth * rows * 6, 1, d)

    cos, sin = _rope_tables(n)
    cosq, sinq = jnp.tile(cos, (1, MLA_HEADS)), jnp.tile(sin, (1, MLA_HEADS))
    ones_q = jnp.ones((n_ctx, MLA_HEADS * MLA_ROPE), F32)
    zeros_q = jnp.zeros((n_ctx, MLA_HEADS * MLA_ROPE), F32)

    def flat(t):
        return t.reshape(1, bsz * n_ctx, t.shape[-1])

    def unflat(t):
        return t.reshape(bsz, n_ctx, t.shape[-1])

    x_lat, x_ctx = x, ctx
    for l in range(depth):
        need_ctx = l < last_ctx_reader
        use_ctx = need_ctx or (l % 2 == 0)
        i = l // 2
        sh1, sc1, g1, sh2, sc2, g2 = (_Mod(table, l * rows * 6 + k, 6) for k in range(6))
        csh1, csc1, cg1, csh2, csc2, cg2 = (_Mod(table, (l * rows + bsz) * 6 + k, 0) for k in range(6))

        if l % 2 == 0:
            w_p, w_q, w_kv, w_gate, b_gate = _even_weights(
                ev_in_w[i], mla_w_uq[i], mla_w_ukv[i], gla_w_gate_fw[i], gla_b_gate_fw[i],
                gla_w_gate_bw[i], gla_b_gate_bw[i])
            w_o = ev_out_w[i].astype(BF16)
            k1 = MLA_HEADS * MLA_V
            p_lat, q_lat, kv_lat = _even_in(x_lat, pre_mix_g[l], sc1, sh1, w_p, mla_q_norm[i], w_q,
                                            mla_kv_norm[i], w_kv, name="even_in_lat")
            p_ctx, q_ctx, kv_ctx = (unflat(t) for t in _even_in(
                flat(x_ctx), pre_mix_g[l], csc1, csh1, w_p, mla_q_norm[i], w_q, mla_kv_norm[i], w_kv,
                name="even_in_ctx"))
            a_lat = _mla(q_lat, cosq, sinq, kv_ctx, p_ctx, kv_lat, p_lat, cos, sin, name="mla_lat")
            g_ctx, g_lat = _gla(p_ctx, p_lat, w_gate, b_gate, gla_o_norm[i], need_ctx)
            x_lat = _mnr(a_lat, g_lat, w_o[:k1], w_o[k1:], x_lat, g1, post_mix_g[l], name="even_out_lat")
            if need_ctx:
                a_ctx = _mla(q_ctx, ones_q, zeros_q, kv_ctx, p_ctx, name="mla_ctx")
                x_ctx = unflat(_mnr(flat(a_ctx), flat(g_ctx), w_o[:k1], w_o[k1:], flat(x_ctx), cg1,
                                    post_mix_g[l], name="even_out_ctx"))
        else:
            w_p = od_in_w[i].astype(BF16)
            w_o = od_out_w[i].astype(BF16)
            w_dw = jnp.concatenate([conf_dw_w[i], jnp.zeros((1, CONV_CH), F32)], axis=0)
            streams = [(x_lat, sc1, sh1, g1, "lat")]
            if need_ctx:
                streams.append((x_ctx, csc1, csh1, cg1, "ctx"))
            outs = []
            for xs, sc, sh, gate, tag in streams:
                per_batch = sc.stride > 0
                xin = xs if per_batch else flat(xs)
                p = _nmm(xin, pre_mix_g[l], w_p, sc, sh, name="odd_in_" + tag)
                p = p if per_batch else unflat(p)
                u = _conformer(p, w_dw, conf_dw_b[i], conf_ln_g[i], conf_ln_b[i], name="conformer_" + tag)
                fm = _fnet(p, name="fnet_" + tag)
                if not per_batch:
                    u, fm = flat(u), flat(fm)
                y = _mnr(u, fm, w_o[:CONV_CH], w_o[CONV_CH:], xin, gate, post_mix_g[l], name="odd_out_" + tag)
                outs.append(y if per_batch else unflat(y))
            x_lat = outs[0]
            if need_ctx:
                x_ctx = outs[1]

        wg, wv, cwb, wo = _ffn_weights(ffn_in_w[l], ffn_dw_w[l], ffn_dw_b[l], ffn_out_w[l])
        x_lat = _ffn(x_lat, pre_ffn_g[l], sc2, sh2, g2, post_ffn_g[l], wg, wv, cwb, wo, n, name="ffn_lat")
        if need_ctx:
            x_ctx = unflat(_ffn(flat(x_ctx), pre_ffn_g[l], csc2, csh2, cg2, post_ffn_g[l], wg, wv, cwb, wo,
                                n_ctx, name="ffn_ctx"))
    return x_lat
```

```python
import functools
import math
from typing import NamedTuple

import numpy as np
import jax
import jax.numpy as jnp
from jax import lax
from jax.experimental import pallas as pl
from jax.experimental.pallas import tpu as pltpu

F32 = jnp.float32
BF16 = jnp.bfloat16

GRID_W = 64
ROPE_BASE = 10000.0
MLA_HEADS = 8
MLA_Q_RANK = 256
MLA_KV_RANK = 128
MLA_NOPE = 64
MLA_ROPE = 32
MLA_V = 64
MLA_SCALE = (MLA_NOPE + MLA_ROPE) ** -0.5
GLA_HEADS = 4
GLA_DK = 64
GLA_DV = 128
GLA_GATE_RANK = 16
GLA_GATE_NORM = 16.0
GLA_CHUNK = 64
CONV_CH = 768
CONV_W = 31
FNET_GROUPS = 4
FNET_GROUP_CH = 64
FNET_CH = FNET_GROUPS * FNET_GROUP_CH
FFN_CHUNK = 256
EPS = 1e-6

P_QC, P_GQ, P_GK = 0, 256, 512
P_KVC, P_TAIL = 768, 896
P_GV, P_GR = 1024, 1536
P_WIDTH = 2048
T_KR, T_KRROT, T_GF, T_GB = 0, 32, 64, 80

NT_DIMS = (((1,), (1,)), ((), ()))
TN_DIMS = (((0,), (0,)), ((), ()))

VMEM_LIMIT = 60 * 1024 * 1024


def _params(*sem):
    return pltpu.CompilerParams(dimension_semantics=sem, vmem_limit_bytes=VMEM_LIMIT)


def _rms(xf, gain):
    return xf * lax.rsqrt(jnp.mean(xf * xf, axis=-1, keepdims=True) + EPS) * gain


def _sigmoid(x):
    return 1.0 / (1.0 + jnp.exp(-x))


def _dot(a, b):
    return jnp.dot(a, b, preferred_element_type=F32)


def _tile(n, want):
    t = min(n, want)
    assert n % t == 0, (n, want)
    return t


class _Mod(NamedTuple):
    table: jax.Array
    first: int
    stride: int


def _mod_spec(m):
    return pl.BlockSpec((1, 1, m.table.shape[2]), lambda b, i: (m.first + b * m.stride, 0, 0))


def _mod_kernel(c_ref, w_ref, b_ref, o_ref):
    c = c_ref[...]
    s = c * _sigmoid(c)
    o_ref[0] = _dot(s.astype(BF16), w_ref[0].astype(BF16)) + b_ref[0]


def _modulation(cvec, mod_w, mod_b):
    depth, d, n6 = mod_w.shape
    rows = cvec.shape[0]
    tn = _tile(n6, 1536)
    return pl.pallas_call(
        _mod_kernel,
        grid=(depth, n6 // tn),
        in_specs=[
            pl.BlockSpec((rows, d), lambda l, j: (0, 0)),
            pl.BlockSpec((1, d, tn), lambda l, j: (l, 0, j)),
            pl.BlockSpec((1, 1, tn), lambda l, j: (l, 0, j)),
        ],
        out_specs=pl.BlockSpec((1, rows, tn), lambda l, j: (l, 0, j)),
        out_shape=jax.ShapeDtypeStruct((depth, rows, n6), F32),
        compiler_params=_params("parallel", "parallel"),
        name="modulation",
    )(cvec, mod_w, mod_b.reshape(depth, 1, n6))


def _nmm_kernel(x_ref, g_ref, sc_ref, sh_ref, w_ref, o_ref, *, col_w):
    h = _rms(x_ref[0], g_ref[...]) * (1.0 + sc_ref[0]) + sh_ref[0]
    hb = h.astype(BF16)
    for j in range(o_ref.shape[2] // col_w):
        cols = slice(j * col_w, (j + 1) * col_w)
        o_ref[0, :, cols] = _dot(hb, w_ref[:, cols]).astype(o_ref.dtype)


def _nmm(x, gain, w, sc, sh, tm=1024, name="nmm"):
    bsz, n, k = x.shape
    nout = w.shape[1]
    tm = _tile(n, tm)
    col_w = next(cw for cw in (512, 256, 128) if nout % cw == 0)
    return pl.pallas_call(
        functools.partial(_nmm_kernel, col_w=col_w),
        grid=(bsz, n // tm),
        in_specs=[pl.BlockSpec((1, tm, k), lambda b, i: (b, i, 0)),
                  pl.BlockSpec((1, k), lambda b, i: (0, 0)),
                  _mod_spec(sc), _mod_spec(sh),
                  pl.BlockSpec((k, nout), lambda b, i: (0, 0))],
        out_specs=pl.BlockSpec((1, tm, nout), lambda b, i: (b, i, 0)),
        out_shape=jax.ShapeDtypeStruct((bsz, n, nout), BF16),
        compiler_params=_params("parallel", "parallel"),
        name=name,
    )(x, gain.reshape(1, k), sc.table, sh.table, w)


def _even_in_kernel(x_ref, g_ref, sc_ref, sh_ref, w_ref, qn_ref, wq_ref, kvn_ref, wkv_ref,
                    p_ref, q_ref, kv_ref):
    h = _rms(x_ref[0], g_ref[...]) * (1.0 + sc_ref[0]) + sh_ref[0]
    hb = h.astype(BF16)

    def up(lat, gain_ref, wu_ref, out_ref):
        lb = _rms(lat, gain_ref[...]).astype(BF16)
        for j in range(out_ref.shape[2] // 512):
            cols = slice(j * 512, (j + 1) * 512)
            out_ref[0, :, cols] = _dot(lb, wu_ref[:, cols]).astype(out_ref.dtype)

    for j in range(P_WIDTH // 512):
        c0 = j * 512
        pj = _dot(hb, w_ref[:, c0:c0 + 512])
        p_ref[0, :, c0:c0 + 512] = pj.astype(p_ref.dtype)
        if c0 <= P_QC < c0 + 512:
            up(pj[:, P_QC - c0:P_QC - c0 + MLA_Q_RANK], qn_ref, wq_ref, q_ref)
        if c0 <= P_KVC < c0 + 512:
            up(pj[:, P_KVC - c0:P_KVC - c0 + MLA_KV_RANK], kvn_ref, wkv_ref, kv_ref)


def _even_in(x, gain, sc, sh, w_p, q_norm, w_q, kv_norm, w_kv, tm=1024, name="even_in"):
    bsz, n, d = x.shape
    tm = _tile(n, tm)

    def whole(a):
        return pl.BlockSpec(a.shape, lambda b, i: (0, 0))

    args = [gain.reshape(1, d), w_p, q_norm.reshape(1, -1), w_q, kv_norm.reshape(1, -1), w_kv]
    widths = (w_p.shape[1], w_q.shape[1], w_kv.shape[1])
    return pl.pallas_call(
        _even_in_kernel,
        grid=(bsz, n // tm),
        in_specs=[pl.BlockSpec((1, tm, d), lambda b, i: (b, i, 0)), whole(args[0]), _mod_spec(sc), _mod_spec(sh)]
        + [whole(a) for a in args[1:]],
        out_specs=[pl.BlockSpec((1, tm, w), lambda b, i: (b, i, 0)) for w in widths],
        out_shape=[jax.ShapeDtypeStruct((bsz, n, w), BF16) for w in widths],
        compiler_params=_params("parallel", "parallel"),
        name=name,
    )(x, args[0], sc.table, sh.table, *args[1:])


def _ffn_kernel(a1_ref, a1p_ref, a1n_ref, a2_ref, a2p_ref, a2n_ref, w1_ref, w2_ref, mg_ref, mpg_ref,
                x_ref, xp_ref, xn_ref, g_ref, sc_ref, sh_ref, gate_ref, pg_ref,
                wg_ref, wv_ref, cw_ref, wo_ref, o_ref, xh_ref, acc_ref, g0_ref, g1_ref, v0_ref, v1_ref, *, seq_len):
    tm = x_ref.shape[1]
    n_chunks, fc, _ = wo_ref.shape
    i = pl.program_id(1)
    gain = g_ref[...]
    scale = 1.0 + sc_ref[0]
    shift = sh_ref[0]

    def modulated(rows):
        return (_rms(rows, gain) * scale + shift).astype(BF16)

    def mixed(x, a1, a2):
        y = _dot(a1, w1_ref[...]) + _dot(a2, w2_ref[...])
        return x + mg_ref[0] * _rms(y, mpg_ref[...])

    def halo(p_ref, n_ref):
        return jnp.concatenate([p_ref[0].astype(F32)[8:16], n_ref[0].astype(F32)[0:8]], axis=0).astype(BF16)

    o_ref[0] = mixed(x_ref[0], a1_ref[0], a2_ref[0])
    xh_ref[0:tm] = modulated(o_ref[0])
    x_halo = mixed(jnp.concatenate([xp_ref[0], xn_ref[0]], axis=0), halo(a1p_ref, a1n_ref), halo(a2p_ref, a2n_ref))
    xh_ref[tm:tm + 16] = modulated(x_halo)

    local = lax.broadcasted_iota(jnp.int32, (tm, 1), 0)
    pos = (i * tm + local) % seq_len
    seq_first = pos == 0
    seq_last = pos == seq_len - 1
    tile_first = local == 0
    tile_last = local == tm - 1
    acc_ref[...] = jnp.zeros_like(acc_ref)
    slots = ((g0_ref, v0_ref), (g1_ref, v1_ref))

    def project(c, slot):
        gs_ref, vs_ref = slots[slot]
        cols = pl.ds(pl.multiple_of(c * fc, fc), fc)
        gs_ref[...] = _dot(xh_ref[...], wg_ref[:, cols])
        vs_ref[...] = _dot(xh_ref[0:tm], wv_ref[:, cols])

    def mix(c, slot):
        gs_ref, vs_ref = slots[slot]
        gm = gs_ref[0:tm]
        g_prev = jnp.where(tile_first, gs_ref[tm + 7:tm + 8], pltpu.roll(gm, 1, axis=0))
        g_prev = jnp.where(seq_first, 0.0, g_prev)
        g_next = jnp.where(tile_last, gs_ref[tm + 8:tm + 9], pltpu.roll(gm, tm - 1, axis=0))
        g_next = jnp.where(seq_last, 0.0, g_next)
        cw = cw_ref[c]
        y = g_prev * cw[0:1] + gm * cw[1:2] + g_next * cw[2:3] + cw[3:4]
        u = 0.5 * y * (1.0 + lax.erf(y * (1.0 / math.sqrt(2.0)))) * vs_ref[...]
        acc_ref[...] += _dot(u.astype(BF16), wo_ref[c])

    project(0, 0)

    def body(j, carry):
        c = 2 * j
        project(c + 1, 1)
        mix(c, 0)
        project(c + 2, 0)
        mix(c + 1, 1)
        return carry

    lax.fori_loop(0, (n_chunks - 1) // 2, body, 0)
    if n_chunks % 2 == 0:
        project(n_chunks - 1, 1)
        mix(n_chunks - 2, 0)
        mix(n_chunks - 1, 1)
    else:
        mix(n_chunks - 1, 0)
    o_ref[0] = o_ref[0] + gate_ref[0] * _rms(acc_ref[...], pg_ref[...])


def _ffn(a1, a2, w1, w2, mix_gate, mix_post_g, x, pre_g, sc, sh, gate, post_g, wg, wv, cwb, wo, seq_len,
         tm=1024, name="ffn"):
    bsz, n, d = x.shape
    nf, fc, _ = wo.shape
    f = nf * fc
    tm = _tile(n, tm)
    nb8, nb16 = n // 8, n // 16
    vec_spec = pl.BlockSpec((1, d), lambda b, i: (0, 0))

    def with_halo(a):
        k = a.shape[2]
        return [pl.BlockSpec((1, tm, k), lambda b, i: (b, i, 0)),
                pl.BlockSpec((1, 16, k), lambda b, i: (b, jnp.maximum(i * (tm // 16) - 1, 0), 0)),
                pl.BlockSpec((1, 16, k), lambda b, i: (b, jnp.minimum((i + 1) * (tm // 16), nb16 - 1), 0))]

    return pl.pallas_call(
        functools.partial(_ffn_kernel, seq_len=seq_len),
        grid=(bsz, n // tm),
        in_specs=with_halo(a1) + with_halo(a2) + [
            pl.BlockSpec(w1.shape, lambda b, i: (0, 0), pipeline_mode=pl.Buffered(1)),
            pl.BlockSpec(w2.shape, lambda b, i: (0, 0), pipeline_mode=pl.Buffered(1)),
            _mod_spec(mix_gate), vec_spec,
            pl.BlockSpec((1, tm, d), lambda b, i: (b, i, 0)),
            pl.BlockSpec((1, 8, d), lambda b, i: (b, jnp.maximum(i * (tm // 8) - 1, 0), 0)),
            pl.BlockSpec((1, 8, d), lambda b, i: (b, jnp.minimum((i + 1) * (tm // 8), nb8 - 1), 0)),
            vec_spec, _mod_spec(sc), _mod_spec(sh), _mod_spec(gate), vec_spec,
            pl.BlockSpec((d, f), lambda b, i: (0, 0), pipeline_mode=pl.Buffered(1)),
            pl.BlockSpec((d, f), lambda b, i: (0, 0), pipeline_mode=pl.Buffered(1)),
            pl.BlockSpec((nf, 8, fc), lambda b, i: (0, 0, 0), pipeline_mode=pl.Buffered(1)),
            pl.BlockSpec((nf, fc, d), lambda b, i: (0, 0, 0), pipeline_mode=pl.Buffered(1)),
        ],
        out_specs=pl.BlockSpec((1, tm, d), lambda b, i: (b, i, 0)),
        out_shape=jax.ShapeDtypeStruct((bsz, n, d), F32),
        scratch_shapes=[pltpu.VMEM((tm + 16, d), BF16), pltpu.VMEM((tm, d), F32),
                        pltpu.VMEM((tm + 16, fc), F32), pltpu.VMEM((tm + 16, fc), F32),
                        pltpu.VMEM((tm, fc), F32), pltpu.VMEM((tm, fc), F32)],
        compiler_params=_params("parallel", "parallel"),
        name=name,
    )(a1, a1, a1, a2, a2, a2, w1, w2, mix_gate.table, mix_post_g.reshape(1, d),
      x, x, x, pre_g.reshape(1, d), sc.table, sh.table, gate.table, post_g.reshape(1, d), wg, wv, cwb, wo)


def _mla_kernel(*refs, n_ctx, n_lat):
    if n_lat:
        (q_ref, cq_ref, sq_ref, kvc_ref, tc_ref, kvl_ref, tl_ref, ck_ref, sk_ref,
         o_ref, kc_ref, vt_ref, ot_ref, s0_ref, s1_ref, p0_ref, p1_ref, qh_ref, oe_ref) = refs
    else:
        (q_ref, cq_ref, sq_ref, kvc_ref, tc_ref,
         o_ref, kc_ref, vt_ref, ot_ref, s0_ref, s1_ref, p0_ref, p1_ref, qh_ref, oe_ref) = refs
    hv = MLA_HEADS * MLA_NOPE
    _, n_blocks, vrows, rb = vt_ref.shape

    @pl.when(pl.program_id(1) == 0)
    def _build_keys():
        r = lax.broadcasted_iota(jnp.int32, (vrows, MLA_V), 0)
        c = lax.broadcasted_iota(jnp.int32, (vrows, MLA_V), 1)
        sel = (r == c).astype(BF16)

        def fill(kv_ref, t_ref, off, n, roped):
            tail = t_ref[0].astype(F32)
            kr = tail[:, T_KR:T_KR + MLA_ROPE]
            if roped:
                kr = kr * ck_ref[...] + tail[:, T_KRROT:T_KRROT + MLA_ROPE] * sk_ref[...]
            kr = kr.astype(BF16)
            is_v = lax.broadcasted_iota(jnp.int32, (vrows, rb), 0) < MLA_V
            for h in range(MLA_HEADS):
                kc_ref[h, off:off + n, 0:MLA_NOPE] = kv_ref[0, :, h * MLA_NOPE:(h + 1) * MLA_NOPE]
                kc_ref[h, off:off + n, MLA_NOPE:MLA_NOPE + MLA_ROPE] = kr
                kc_ref[h, off:off + n, MLA_NOPE + MLA_ROPE:128] = jnp.zeros((n, 128 - MLA_NOPE - MLA_ROPE), BF16)
                vt = lax.dot_general(sel, kv_ref[0, :, hv + h * MLA_V:hv + (h + 1) * MLA_V], NT_DIMS,
                                     preferred_element_type=F32)
                for jb in range(n // rb):
                    vt_ref[h, off // rb + jb] = jnp.where(is_v, vt[:, jb * rb:(jb + 1) * rb], 1.0).astype(BF16)

        fill(kvc_ref, tc_ref, 0, n_ctx, False)
        if n_lat:
            fill(kvl_ref, tl_ref, n_ctx, n_lat, True)

    tq = q_ref.shape[1]
    hr = MLA_HEADS * MLA_ROPE
    qr = (q_ref[0, :, hv:hv + hr].astype(F32) * cq_ref[...]
          + q_ref[0, :, hv + hr:hv + 2 * hr].astype(F32) * sq_ref[...]).astype(BF16)
    pad = jnp.zeros((tq, 128 - MLA_NOPE - MLA_ROPE), BF16)
    for h in range(MLA_HEADS):
        qh_ref[h] = jnp.concatenate([q_ref[0, :, h * MLA_NOPE:(h + 1) * MLA_NOPE],
                                     qr[:, h * MLA_ROPE:(h + 1) * MLA_ROPE], pad], axis=1)

    s_refs = (s0_ref, s1_ref)
    p_refs = (p0_ref, p1_ref)
    maxes = [None] * MLA_HEADS
    for t in range(MLA_HEADS + 2):
        h_s, h_e, h_v = t, t - 1, t - 2
        do_s, do_e, do_v = h_s < MLA_HEADS, 0 <= h_e < MLA_HEADS, 0 <= h_v
        if do_v:
            oe_ref[...] = jnp.zeros_like(oe_ref)

        def block(j, m, h_s=h_s, h_e=h_e, h_v=h_v, do_s=do_s, do_e=do_e, do_v=do_v):
            rows = pl.ds(pl.multiple_of(j * rb, rb), rb)
            if do_s:
                blk = lax.dot_general(kc_ref[h_s, rows, :], qh_ref[h_s], NT_DIMS, preferred_element_type=F32)
                s_refs[h_s % 2][rows, :] = blk
                m = jnp.maximum(m, jnp.max(blk, axis=0, keepdims=True))
            if do_e:
                p_refs[h_e % 2][rows, :] = jnp.exp2(s_refs[h_e % 2][rows, :] - maxes[h_e]).astype(BF16)
            if do_v:
                oe_ref[...] += _dot(vt_ref[h_v, j], p_refs[h_v % 2][rows, :])
            return m

        m = lax.fori_loop(0, n_blocks, block, jnp.full((1, tq), -jnp.inf, F32), unroll=True)
        if do_s:
            maxes[h_s] = m
        if do_v:
            ot_ref[h_v * MLA_V:(h_v + 1) * MLA_V, :] = oe_ref[0:MLA_V] / oe_ref[MLA_V:MLA_V + 1]
    o_ref[0] = ot_ref[...].T.astype(o_ref.dtype)


def _mla(q, cosq, sinq, kv_c, p_c, kv_l=None, p_l=None, cosk=None, sink=None, tq=512, name="mla"):
    bsz, nq, qw = q.shape
    n_ctx = kv_c.shape[1]
    n_lat = 0 if kv_l is None else kv_l.shape[1]
    nk = n_ctx + n_lat
    tq = _tile(nq, tq)
    rb = math.gcd(math.gcd(n_ctx, n_lat), 256)
    vrows = MLA_V + 16
    hr = MLA_HEADS * MLA_ROPE
    tail_blk = P_TAIL // 128
    in_specs = [
        pl.BlockSpec((1, tq, qw), lambda b, i: (b, i, 0)),
        pl.BlockSpec((tq, hr), lambda b, i: (i, 0)),
        pl.BlockSpec((tq, hr), lambda b, i: (i, 0)),
        pl.BlockSpec((1, n_ctx, kv_c.shape[2]), lambda b, i: (b, 0, 0)),
        pl.BlockSpec((1, n_ctx, 128), lambda b, i: (b, 0, tail_blk)),
    ]
    args = [q, cosq, sinq, kv_c, p_c]
    if n_lat:
        in_specs += [
            pl.BlockSpec((1, n_lat, kv_l.shape[2]), lambda b, i: (b, 0, 0)),
            pl.BlockSpec((1, n_lat, 128), lambda b, i: (b, 0, tail_blk)),
            pl.BlockSpec((n_lat, MLA_ROPE), lambda b, i: (0, 0)),
            pl.BlockSpec((n_lat, MLA_ROPE), lambda b, i: (0, 0)),
        ]
        args += [kv_l, p_l, cosk, sink]
    return pl.pallas_call(
        functools.partial(_mla_kernel, n_ctx=n_ctx, n_lat=n_lat),
        grid=(bsz, nq // tq),
        in_specs=in_specs,
        out_specs=pl.BlockSpec((1, tq, MLA_HEADS * MLA_V), lambda b, i: (b, i, 0)),
        out_shape=jax.ShapeDtypeStruct((bsz, nq, MLA_HEADS * MLA_V), BF16),
        scratch_shapes=[pltpu.VMEM((MLA_HEADS, nk, 128), BF16),
                        pltpu.VMEM((MLA_HEADS, nk // rb, vrows, rb), BF16),
                        pltpu.VMEM((MLA_HEADS * MLA_V, tq), F32),
                        pltpu.VMEM((nk, tq), F32), pltpu.VMEM((nk, tq), F32),
                        pltpu.VMEM((nk, tq), BF16), pltpu.VMEM((nk, tq), BF16),
                        pltpu.VMEM((MLA_HEADS, tq, 128), BF16),
                        pltpu.VMEM((vrows, tq), F32)],
        compiler_params=_params("parallel", "arbitrary"),
        name=name,
    )(*args)


def _gla_kernel(*refs, want_ctx):
    (qc_ref, kc_ref, tc_ref, vc_ref, grc_ref, ql_ref, kl_ref, tl_ref, vl_ref, grl_ref,
     wg_ref, bg_ref, on_ref) = refs[:13]
    if want_ctx:
        oc_ref, ol_ref = refs[13:15]
        scratch = refs[15:]
    else:
        oc_ref, ol_ref = None, refs[13]
        scratch = refs[14:]
    lgc_ref, lgl_ref, accc_ref, accl_ref, st_ref = scratch
    cs = GLA_CHUNK
    hk = GLA_HEADS * GLA_DK

    shift = cs.bit_length() - 1
    for t_ref, lg_ref in ((tc_ref, lgc_ref), (tl_ref, lgl_ref)):
        n = t_ref.shape[1]
        tr = _tile(n, 256)
        r = lax.broadcasted_iota(jnp.int32, (tr, tr), 0)
        c = lax.broadcasted_iota(jnp.int32, (tr, tr), 1)
        same_chunk = (r >> shift) == (c >> shift)
        tri_fw = (same_chunk & (r >= c)).astype(BF16)
        tri_bw = (same_chunk & (r <= c)).astype(BF16)

        def cum_body(i, carry, t_ref=t_ref, lg_ref=lg_ref, tr=tr, tri_fw=tri_fw, tri_bw=tri_bw):
            rows = pl.ds(pl.multiple_of(i * tr, tr), tr)
            z = _dot(t_ref[0, rows, :], wg_ref[...]) + bg_ref[...]
            lg = (jnp.minimum(z, 0.0) - jnp.log(1.0 + jnp.exp(-jnp.abs(z)))) * (1.0 / GLA_GATE_NORM)
            hi = lg.astype(BF16)
            rest = lg - hi.astype(F32)
            mid = rest.astype(BF16)
            lo = (rest - mid.astype(F32)).astype(BF16)
            for tri, cols in ((tri_fw, slice(0, hk)), (tri_bw, slice(hk, 2 * hk))):
                lg_ref[rows, cols] = (_dot(tri, lo[:, cols]) + _dot(tri, mid[:, cols])) + _dot(tri, hi[:, cols])
            return carry

        lax.fori_loop(0, n // tr, cum_body, 0, unroll=2)

    br = math.gcd(math.gcd(qc_ref.shape[1], ql_ref.shape[1]), 256)
    cpb = br // cs
    row = lax.broadcasted_iota(jnp.int32, (br, br), 0)
    col = lax.broadcasted_iota(jnp.int32, (br, br), 1)
    same = (row >> shift) == (col >> shift)
    causal = (same & (row >= col), same & (row <= col))
    erow = lax.broadcasted_iota(jnp.int32, (cs, cs), 0)
    ecol = lax.broadcasted_iota(jnp.int32, (cs, cs), 1)
    eye = (erow == ecol).astype(F32)
    st_ref[...] = jnp.zeros_like(st_ref)

    def block_diag(x):
        wide = jnp.concatenate([x] * cpb, axis=1)
        return jnp.where(same, wide, jnp.zeros_like(wide))

    def block(q_ref, k_ref, v_ref, lg_ref, acc_ref, d, r0):
        rows = pl.ds(pl.multiple_of(r0, br), br)
        cum = lg_ref[rows, d * hk:(d + 1) * hk]
        cum3 = cum.reshape(cpb, cs, hk)
        edge = cum3[:, cs - 1:cs, :] if d == 0 else cum3[:, 0:1, :]
        tot = jnp.broadcast_to(edge, cum3.shape).reshape(br, hk)
        half = 0.5 * tot
        q = q_ref[0, rows, :].astype(F32)
        k = k_ref[0, rows, :].astype(F32)
        v = v_ref[0, rows, :]
        qt = (q * jnp.exp(cum - half)).astype(BF16)
        kt = (k * jnp.exp(half - cum)).astype(BF16)
        qs = (q * jnp.exp(cum)).astype(BF16)
        kd = (k * jnp.exp(tot - cum)).astype(BF16)
        order = range(cpb) if d == 0 else range(cpb - 1, -1, -1)
        for h in range(GLA_HEADS):
            ks = slice(h * GLA_DK, (h + 1) * GLA_DK)
            vs = slice(h * GLA_DV, (h + 1) * GLA_DV)
            a = lax.dot_general(qt[:, ks], kt[:, ks], NT_DIMS, preferred_element_type=F32)
            o = _dot(jnp.where(causal[d], a, 0.0).astype(BF16), v[:, vs])
            u = lax.dot_general(block_diag(kd[:, ks]), v[:, vs], TN_DIMS, preferred_element_type=F32)
            st = st_ref[d, h]
            starts = [None] * cpb
            for c in order:
                starts[c] = st.astype(BF16)
                dec = jnp.exp(jnp.sum(eye * edge[c, :, ks], axis=1, keepdims=True))
                st = st * dec + u[c * cs:(c + 1) * cs]
            st_ref[d, h] = st
            acc_ref[d, rows, vs] = o + _dot(block_diag(qs[:, ks]), jnp.concatenate(starts, axis=0))

    def scan(q_ref, k_ref, v_ref, lg_ref, acc_ref):
        n_blocks = q_ref.shape[1] // br

        def body(i, carry):
            block(q_ref, k_ref, v_ref, lg_ref, acc_ref, 0, i * br)
            block(q_ref, k_ref, v_ref, lg_ref, acc_ref, 1, (n_blocks - 1 - i) * br)
            return carry

        lax.fori_loop(0, n_blocks, body, 0, unroll=2)

    def finish(acc_ref, gr_ref, o_ref):
        n = acc_ref.shape[1]
        tr = _tile(n, 256)

        def body(i, carry):
            rows = pl.ds(pl.multiple_of(i * tr, tr), tr)
            gr = gr_ref[0, rows, :].astype(F32)
            gate = gr * _sigmoid(gr)
            for h in range(GLA_HEADS):
                vs = slice(h * GLA_DV, (h + 1) * GLA_DV)
                o = acc_ref[0, rows, vs] + acc_ref[1, rows, vs]
                o_ref[0, rows, vs] = (_rms(o, on_ref[...]) * gate[:, vs]).astype(o_ref.dtype)
            return carry

        lax.fori_loop(0, n // tr, body, 0)

    scan(qc_ref, kc_ref, vc_ref, lgc_ref, accc_ref)
    scan(ql_ref, kl_ref, vl_ref, lgl_ref, accl_ref)
    if want_ctx:
        finish(accc_ref, grc_ref, oc_ref)
    finish(accl_ref, grl_ref, ol_ref)


def _gla(p_c, p_l, w_gate, b_gate, o_norm, want_ctx, name="gla"):
    bsz, n_ctx, _ = p_c.shape
    n_lat = p_l.shape[1]
    hk, hv = GLA_HEADS * GLA_DK, GLA_HEADS * GLA_DV

    def specs(n):
        return [
            pl.BlockSpec((1, n, hk), lambda b: (b, 0, P_GQ // hk)),
            pl.BlockSpec((1, n, hk), lambda b: (b, 0, P_GK // hk)),
            pl.BlockSpec((1, n, 128), lambda b: (b, 0, P_TAIL // 128)),
            pl.BlockSpec((1, n, hv), lambda b: (b, 0, P_GV // hv)),
            pl.BlockSpec((1, n, hv), lambda b: (b, 0, P_GR // hv)),
        ]

    in_specs = specs(n_ctx) + specs(n_lat) + [
        pl.BlockSpec((128, 2 * hk), lambda b: (0, 0)),
        pl.BlockSpec((1, 2 * hk), lambda b: (0, 0)),
        pl.BlockSpec((1, GLA_DV), lambda b: (0, 0)),
    ]
    out_specs = [pl.BlockSpec((1, n_lat, hv), lambda b: (b, 0, 0))]
    out_shape = [jax.ShapeDtypeStruct((bsz, n_lat, hv), BF16)]
    if want_ctx:
        out_specs.insert(0, pl.BlockSpec((1, n_ctx, hv), lambda b: (b, 0, 0)))
        out_shape.insert(0, jax.ShapeDtypeStruct((bsz, n_ctx, hv), BF16))
    outs = pl.pallas_call(
        functools.partial(_gla_kernel, want_ctx=want_ctx),
        grid=(bsz,),
        in_specs=in_specs,
        out_specs=out_specs,
        out_shape=out_shape,
        scratch_shapes=[
            pltpu.VMEM((n_ctx, 2 * hk), F32), pltpu.VMEM((n_lat, 2 * hk), F32),
            pltpu.VMEM((2, n_ctx, hv), F32), pltpu.VMEM((2, n_lat, hv), F32),
            pltpu.VMEM((2, GLA_HEADS, GLA_DK, GLA_DV), F32),
        ],
        compiler_params=_params("parallel"),
        name=name,
    )(p_c, p_c, p_c, p_c, p_c, p_l, p_l, p_l, p_l, p_l, w_gate, b_gate, o_norm.reshape(1, GLA_DV))
    return (outs[0], outs[1]) if want_ctx else (None, outs[0])


def _conf_kernel(a_ref, gt_ref, w_ref, b_ref, lg_ref, lb_ref, o_ref, u_ref):
    n = a_ref.shape[1]
    pad = (CONV_W - 1) // 2
    lead = 16
    u_ref[0:lead] = jnp.zeros((lead, CONV_CH), F32)
    u_ref[lead + n:lead + n + lead] = jnp.zeros((lead, CONV_CH), F32)
    u_ref[lead:lead + n] = a_ref[0].astype(F32) * _sigmoid(gt_ref[0].astype(F32))
    tr = _tile(n, 256)

    def body(i, carry):
        r0 = pl.multiple_of(i * tr, tr)
        ext = u_ref[pl.ds(r0, tr + 2 * lead), :]
        y = jnp.zeros((tr, CONV_CH), F32) + b_ref[...]
        for k in range(CONV_W):
            off = lead - pad + k
            y = y + pltpu.roll(ext, tr + 2 * lead - off, axis=0)[0:tr] * w_ref[k:k + 1, :]
        mu = jnp.mean(y, axis=-1, keepdims=True)
        yc = y - mu
        z = yc * lax.rsqrt(jnp.mean(yc * yc, axis=-1, keepdims=True) + EPS) * lg_ref[...] + lb_ref[...]
        o_ref[0, pl.ds(r0, tr), :] = (z * _sigmoid(z)).astype(o_ref.dtype)
        return carry

    lax.fori_loop(0, n // tr, body, 0)


def _conformer(p, w, b, ln_g, ln_b, name="conformer"):
    bsz, n, _ = p.shape
    vec = pl.BlockSpec((1, CONV_CH), lambda i: (0, 0))
    return pl.pallas_call(
        _conf_kernel,
        grid=(bsz,),
        in_specs=[
            pl.BlockSpec((1, n, CONV_CH), lambda i: (i, 0, 0)),
            pl.BlockSpec((1, n, CONV_CH), lambda i: (i, 0, 1)),
            pl.BlockSpec((CONV_W + 1, CONV_CH), lambda i: (0, 0)),
            vec, vec, vec,
        ],
        out_specs=pl.BlockSpec((1, n, CONV_CH), lambda i: (i, 0, 0)),
        out_shape=jax.ShapeDtypeStruct((bsz, n, CONV_CH), BF16),
        scratch_shapes=[pltpu.VMEM((n + 32, CONV_CH), F32)],
        compiler_params=_params("parallel"),
        name=name,
    )(p, p, w, b.reshape(1, CONV_CH), ln_g.reshape(1, CONV_CH), ln_b.reshape(1, CONV_CH))


def _fnet_kernel(f_ref, cs_ref, wd_ref, o_ref, z_ref, *, scale):
    n = f_ref.shape[1]
    b = pl.program_id(1)

    @pl.when(pl.program_id(0) == 0)
    def _channel_dft():
        zz = _dot(f_ref[0], cs_ref[...])
        z_ref[b, 0:n] = zz[:, 0:FNET_CH].astype(BF16)
        z_ref[b, n:2 * n] = zz[:, FNET_CH:2 * FNET_CH].astype(BF16)

    o_ref[0] = (_dot(wd_ref[...], z_ref[b]) * scale).astype(o_ref.dtype)


def _dft_tables(n):
    k = np.arange(n, dtype=np.float64)
    ang = 2.0 * np.pi * np.outer(k, k) / n
    return np.cos(ang), np.sin(ang)


def _fnet(p, name="fnet"):
    bsz, n, width = p.shape
    cn, sn = _dft_tables(n)
    wd = jnp.asarray(np.concatenate([cn, -sn], axis=1), BF16)
    cg, sg = _dft_tables(FNET_GROUP_CH)
    eye = np.eye(FNET_GROUPS)
    cs = jnp.asarray(np.concatenate([np.kron(eye, cg), np.kron(eye, sg)], axis=1), BF16)
    tm = _tile(n, 512)
    scale = 1.0 / math.sqrt(n * FNET_GROUP_CH)
    return pl.pallas_call(
        functools.partial(_fnet_kernel, scale=scale),
        grid=(n // tm, bsz),
        in_specs=[
            pl.BlockSpec((1, n, FNET_CH), lambda i, b: (b, 0, (width - FNET_CH) // FNET_CH)),
            pl.BlockSpec((FNET_CH, 2 * FNET_CH), lambda i, b: (0, 0)),
            pl.BlockSpec((tm, 2 * n), lambda i, b: (i, 0)),
        ],
        out_specs=pl.BlockSpec((1, tm, FNET_CH), lambda i, b: (b, i, 0)),
        out_shape=jax.ShapeDtypeStruct((bsz, n, FNET_CH), BF16),
        scratch_shapes=[pltpu.VMEM((bsz, 2 * n, FNET_CH), BF16)],
        compiler_params=_params("arbitrary", "arbitrary"),
        name=name,
    )(p, cs, wd)


def _rot_cols(w):
    a, b, c, d = jnp.split(w, 4, axis=-1)
    return jnp.concatenate([-b, a, -d, c], axis=-1)


def _even_weights(w_in, w_uq, w_ukv, w_gfw, b_gfw, w_gbw, b_gbw):
    d = w_in.shape[0]
    sizes = [MLA_Q_RANK, MLA_KV_RANK, MLA_ROPE, GLA_HEADS * GLA_DK, GLA_HEADS * GLA_DK, GLA_HEADS * GLA_DV,
             GLA_GATE_RANK, GLA_GATE_RANK, GLA_HEADS * GLA_DV]
    qc, kvc, kr, gq, gk, gv, glf, glb, gr = jnp.split(w_in, np.cumsum(sizes)[:-1].tolist(), axis=1)
    tail_pad = jnp.zeros((d, 128 - 2 * MLA_ROPE - 2 * GLA_GATE_RANK), w_in.dtype)
    w_p = jnp.concatenate([qc, gq * (GLA_DK ** -0.5), gk, kvc, kr, _rot_cols(kr), glf, glb, tail_pad, gv, gr],
                          axis=1).astype(BF16)
    uq = w_uq.reshape(MLA_Q_RANK, MLA_HEADS, MLA_NOPE + MLA_ROPE) * (MLA_SCALE * math.log2(math.e))
    qn = uq[:, :, :MLA_NOPE].reshape(MLA_Q_RANK, -1)
    qr = uq[:, :, MLA_NOPE:]
    w_q = jnp.concatenate([qn, qr.reshape(MLA_Q_RANK, -1), _rot_cols(qr).reshape(MLA_Q_RANK, -1)],
                          axis=1).astype(BF16)
    ukv = w_ukv.reshape(MLA_KV_RANK, MLA_HEADS, MLA_NOPE + MLA_V)
    w_kv = jnp.concatenate([ukv[:, :, :MLA_NOPE].reshape(MLA_KV_RANK, -1),
                            ukv[:, :, MLA_NOPE:].reshape(MLA_KV_RANK, -1)], axis=1).astype(BF16)
    hk = GLA_HEADS * GLA_DK
    w_gate = jnp.zeros((128, 2 * hk), F32)
    w_gate = w_gate.at[T_GF:T_GF + GLA_GATE_RANK, :hk].set(w_gfw)
    w_gate = w_gate.at[T_GB:T_GB + GLA_GATE_RANK, hk:].set(w_gbw).astype(BF16)
    b_gate = jnp.concatenate([b_gfw, b_gbw]).reshape(1, 2 * hk)
    return w_p, w_q, w_kv, w_gate, b_gate


def _ffn_weights(w_in, dw_w, dw_b, w_out):
    d, f2 = w_in.shape
    f = f2 // 2
    nf = f // FFN_CHUNK
    wg = w_in[:, :f].astype(BF16)
    wv = w_in[:, f:].astype(BF16)
    taps = jnp.concatenate([dw_w, dw_b[None], jnp.zeros((8 - dw_w.shape[0] - 1, f), F32)], axis=0)
    cwb = taps.reshape(8, nf, FFN_CHUNK).transpose(1, 0, 2)
    wo = w_out.reshape(nf, FFN_CHUNK, d).astype(BF16)
    return wg, wv, cwb, wo


def _rope_tables(n):
    rows = n // GRID_W
    row = jnp.repeat(jnp.arange(rows), GRID_W).astype(F32)
    col = jnp.tile(jnp.arange(GRID_W), rows).astype(F32)
    half = MLA_ROPE // 2
    inv = ROPE_BASE ** (-jnp.arange(0, half, 2, dtype=F32) / half)
    ar = row[:, None] * inv
    ac = col[:, None] * inv
    ang = jnp.concatenate([ar, ar, ac, ac], axis=-1)
    return jnp.cos(ang), jnp.sin(ang)


def kernel(x, c, ctx, c_ctx, mod_w, mod_b, pre_mix_g, post_mix_g, pre_ffn_g, post_ffn_g, ev_in_w, mla_q_norm, mla_kv_norm, mla_w_uq, mla_w_ukv, gla_w_gate_fw, gla_b_gate_fw, gla_w_gate_bw, gla_b_gate_bw, gla_o_norm, ev_out_w, od_in_w, conf_dw_w, conf_dw_b, conf_ln_g, conf_ln_b, od_out_w, ffn_in_w, ffn_dw_w, ffn_dw_b, ffn_out_w):
    bsz, n, d = x.shape
    n_ctx = ctx.shape[1]
    depth = mod_w.shape[0]
    last_ctx_reader = ((depth - 1) // 2) * 2

    rows = -(-(bsz + 1) // 8) * 8
    cvec = jnp.concatenate([c, c_ctx[None], jnp.zeros((rows - bsz - 1, d), F32)], axis=0)
    mod = _modulation(cvec, mod_w, mod_b)
    table = mod.reshape(depth * rows * 6, 1, d)

    cos, sin = _rope_tables(n)
    cosq, sinq = jnp.tile(cos, (1, MLA_HEADS)), jnp.tile(sin, (1, MLA_HEADS))
    ones_q = jnp.ones((n_ctx, MLA_HEADS * MLA_ROPE), F32)
    zeros_q = jnp.zeros((n_ctx, MLA_HEADS * MLA_ROPE), F32)

    def flat(t):
        return t.reshape(1, bsz * n_ctx, t.shape[-1])

    def unflat(t):
        return t.reshape(bsz, n_ctx, t.shape[-1])

    x_lat, x_ctx = x, ctx
    for l in range(depth):
        need_ctx = l < last_ctx_reader
        use_ctx = need_ctx or (l % 2 == 0)
        i = l // 2
        sh1, sc1, g1, sh2, sc2, g2 = (_Mod(table, l * rows * 6 + k, 6) for k in range(6))
        csh1, csc1, cg1, csh2, csc2, cg2 = (_Mod(table, (l * rows + bsz) * 6 + k, 0) for k in range(6))

        if l % 2 == 0:
            w_p, w_q, w_kv, w_gate, b_gate = _even_weights(
                ev_in_w[i], mla_w_uq[i], mla_w_ukv[i], gla_w_gate_fw[i], gla_b_gate_fw[i],
                gla_w_gate_bw[i], gla_b_gate_bw[i])
            w_o = ev_out_w[i].astype(BF16)
            k1 = MLA_HEADS * MLA_V
            p_lat, q_lat, kv_lat = _even_in(x_lat, pre_mix_g[l], sc1, sh1, w_p, mla_q_norm[i], w_q,
                                            mla_kv_norm[i], w_kv, name="even_in_lat")
            p_ctx, q_ctx, kv_ctx = (unflat(t) for t in _even_in(
                flat(x_ctx), pre_mix_g[l], csc1, csh1, w_p, mla_q_norm[i], w_q, mla_kv_norm[i], w_kv,
                name="even_in_ctx"))
            a_lat = _mla(q_lat, cosq, sinq, kv_ctx, p_ctx, kv_lat, p_lat, cos, sin, name="mla_lat")
            g_ctx, g_lat = _gla(p_ctx, p_lat, w_gate, b_gate, gla_o_norm[i], need_ctx)
            mixed_lat = (a_lat, g_lat)
            if need_ctx:
                a_ctx = _mla(q_ctx, ones_q, zeros_q, kv_ctx, p_ctx, name="mla_ctx")
                mixed_ctx = (flat(a_ctx), flat(g_ctx))
        else:
            w_p = od_in_w[i].astype(BF16)
            w_o = od_out_w[i].astype(BF16)
            w_dw = jnp.concatenate([conf_dw_w[i], jnp.zeros((1, CONV_CH), F32)], axis=0)
            k1 = CONV_CH
            p_lat = _nmm(x_lat, pre_mix_g[l], w_p, sc1, sh1, name="odd_in_lat")
            mixed_lat = (_conformer(p_lat, w_dw, conf_dw_b[i], conf_ln_g[i], conf_ln_b[i], name="conformer_lat"),
                         _fnet(p_lat, name="fnet_lat"))
            if need_ctx:
                p_ctx = unflat(_nmm(flat(x_ctx), pre_mix_g[l], w_p, csc1, csh1, name="odd_in_ctx"))
                mixed_ctx = (flat(_conformer(p_ctx, w_dw, conf_dw_b[i], conf_ln_g[i], conf_ln_b[i],
                                             name="conformer_ctx")),
                             flat(_fnet(p_ctx, name="fnet_ctx")))

        wg, wv, cwb, wo = _ffn_weights(ffn_in_w[l], ffn_dw_w[l], ffn_dw_b[l], ffn_out_w[l])
        x_lat = _ffn(*mixed_lat, w_o[:k1], w_o[k1:], g1, post_mix_g[l], x_lat, pre_ffn_g[l], sc2, sh2, g2,
                     post_ffn_g[l], wg, wv, cwb, wo, n, name="ffn_lat")
        if need_ctx:
            x_ctx = unflat(_ffn(*mixed_ctx, w_o[:k1], w_o[k1:], cg1, post_mix_g[l], flat(x_ctx), pre_ffn_g[l],
                                csc2, csh2, cg2, post_ffn_g[l], wg, wv, cwb, wo, n_ctx, name="ffn_ctx"))
    return x_lat
```

```python
import functools
import math
from typing import NamedTuple

import numpy as np
import jax
import jax.numpy as jnp
from jax import lax
from jax.experimental import pallas as pl
from jax.experimental.pallas import tpu as pltpu

F32 = jnp.float32
BF16 = jnp.bfloat16

GRID_W = 64
ROPE_BASE = 10000.0
MLA_HEADS = 8
MLA_Q_RANK = 256
MLA_KV_RANK = 128
MLA_NOPE = 64
MLA_ROPE = 32
MLA_V = 64
MLA_SCALE = (MLA_NOPE + MLA_ROPE) ** -0.5
GLA_HEADS = 4
GLA_DK = 64
GLA_DV = 128
GLA_GATE_RANK = 16
GLA_GATE_NORM = 16.0
GLA_CHUNK = 64
CONV_CH = 768
CONV_W = 31
FNET_GROUPS = 4
FNET_GROUP_CH = 64
FNET_CH = FNET_GROUPS * FNET_GROUP_CH
FFN_CHUNK = 256
EPS = 1e-6

P_QC, P_GQ, P_GK = 0, 256, 512
P_KVC, P_TAIL = 768, 896
P_GV, P_GR = 1024, 1536
P_WIDTH = 2048
T_KR, T_KRROT, T_GF, T_GB = 0, 32, 64, 80

NT_DIMS = (((1,), (1,)), ((), ()))
TN_DIMS = (((0,), (0,)), ((), ()))

VMEM_LIMIT = 60 * 1024 * 1024


def _params(*sem):
    return pltpu.CompilerParams(dimension_semantics=sem, vmem_limit_bytes=VMEM_LIMIT)


def _rms(xf, gain):
    return xf * lax.rsqrt(jnp.mean(xf * xf, axis=-1, keepdims=True) + EPS) * gain


def _sigmoid(x):
    return 1.0 / (1.0 + jnp.exp(-x))


def _dot(a, b):
    return jnp.dot(a, b, preferred_element_type=F32)


def _tile(n, want):
    t = min(n, want)
    assert n % t == 0, (n, want)
    return t


class _Mod(NamedTuple):
    table: jax.Array
    first: int
    stride: int


def _mod_spec(m):
    return pl.BlockSpec((1, 1, m.table.shape[2]), lambda b, i: (m.first + b * m.stride, 0, 0))


def _mod_kernel(c_ref, w_ref, b_ref, o_ref):
    c = c_ref[...]
    s = c * _sigmoid(c)
    o_ref[0] = _dot(s.astype(BF16), w_ref[0].astype(BF16)) + b_ref[0]


def _modulation(cvec, mod_w, mod_b):
    depth, d, n6 = mod_w.shape
    rows = cvec.shape[0]
    tn = _tile(n6, 1536)
    return pl.pallas_call(
        _mod_kernel,
        grid=(depth, n6 // tn),
        in_specs=[
            pl.BlockSpec((rows, d), lambda l, j: (0, 0)),
            pl.BlockSpec((1, d, tn), lambda l, j: (l, 0, j)),
            pl.BlockSpec((1, 1, tn), lambda l, j: (l, 0, j)),
        ],
        out_specs=pl.BlockSpec((1, rows, tn), lambda l, j: (l, 0, j)),
        out_shape=jax.ShapeDtypeStruct((depth, rows, n6), F32),
        compiler_params=_params("parallel", "parallel"),
        name="modulation",
    )(cvec, mod_w, mod_b.reshape(depth, 1, n6))


def _nmm_kernel(x_ref, g_ref, sc_ref, sh_ref, w_ref, o_ref, *, col_w):
    h = _rms(x_ref[0], g_ref[...]) * (1.0 + sc_ref[0]) + sh_ref[0]
    hb = h.astype(BF16)
    for j in range(o_ref.shape[2] // col_w):
        cols = slice(j * col_w, (j + 1) * col_w)
        o_ref[0, :, cols] = _dot(hb, w_ref[:, cols]).astype(o_ref.dtype)


def _nmm(x, gain, w, sc, sh, tm=1024, name="nmm"):
    bsz, n, k = x.shape
    nout = w.shape[1]
    tm = _tile(n, tm)
    col_w = next(cw for cw in (512, 256, 128) if nout % cw == 0)
    return pl.pallas_call(
        functools.partial(_nmm_kernel, col_w=col_w),
        grid=(bsz, n // tm),
        in_specs=[pl.BlockSpec((1, tm, k), lambda b, i: (b, i, 0)),
                  pl.BlockSpec((1, k), lambda b, i: (0, 0)),
                  _mod_spec(sc), _mod_spec(sh),
                  pl.BlockSpec((k, nout), lambda b, i: (0, 0))],
        out_specs=pl.BlockSpec((1, tm, nout), lambda b, i: (b, i, 0)),
        out_shape=jax.ShapeDtypeStruct((bsz, n, nout), BF16),
        compiler_params=_params("parallel", "parallel"),
        name=name,
    )(x, gain.reshape(1, k), sc.table, sh.table, w)


def _even_in_kernel(x_ref, g_ref, sc_ref, sh_ref, w_ref, qn_ref, wq_ref, kvn_ref, wkv_ref,
                    p_ref, q_ref, kv_ref):
    h = _rms(x_ref[0], g_ref[...]) * (1.0 + sc_ref[0]) + sh_ref[0]
    hb = h.astype(BF16)

    def up(lat, gain_ref, wu_ref, out_ref):
        lb = _rms(lat, gain_ref[...]).astype(BF16)
        for j in range(out_ref.shape[2] // 512):
            cols = slice(j * 512, (j + 1) * 512)
            out_ref[0, :, cols] = _dot(lb, wu_ref[:, cols]).astype(out_ref.dtype)

    for j in range(P_WIDTH // 512):
        c0 = j * 512
        pj = _dot(hb, w_ref[:, c0:c0 + 512])
        p_ref[0, :, c0:c0 + 512] = pj.astype(p_ref.dtype)
        if c0 <= P_QC < c0 + 512:
            up(pj[:, P_QC - c0:P_QC - c0 + MLA_Q_RANK], qn_ref, wq_ref, q_ref)
        if c0 <= P_KVC < c0 + 512:
            up(pj[:, P_KVC - c0:P_KVC - c0 + MLA_KV_RANK], kvn_ref, wkv_ref, kv_ref)


def _even_in(x, gain, sc, sh, w_p, q_norm, w_q, kv_norm, w_kv, tm=1024, name="even_in"):
    bsz, n, d = x.shape
    tm = _tile(n, tm)

    def whole(a):
        return pl.BlockSpec(a.shape, lambda b, i: (0, 0))

    args = [gain.reshape(1, d), w_p, q_norm.reshape(1, -1), w_q, kv_norm.reshape(1, -1), w_kv]
    widths = (w_p.shape[1], w_q.shape[1], w_kv.shape[1])
    return pl.pallas_call(
        _even_in_kernel,
        grid=(bsz, n // tm),
        in_specs=[pl.BlockSpec((1, tm, d), lambda b, i: (b, i, 0)), whole(args[0]), _mod_spec(sc), _mod_spec(sh)]
        + [whole(a) for a in args[1:]],
        out_specs=[pl.BlockSpec((1, tm, w), lambda b, i: (b, i, 0)) for w in widths],
        out_shape=[jax.ShapeDtypeStruct((bsz, n, w), BF16) for w in widths],
        compiler_params=_params("parallel", "parallel"),
        name=name,
    )(x, args[0], sc.table, sh.table, *args[1:])


def _ffn_kernel(a1_ref, a1p_ref, a1n_ref, a2_ref, a2p_ref, a2n_ref, w1_ref, w2_ref, mg_ref, mpg_ref,
                x_ref, xp_ref, xn_ref, g_ref, sc_ref, sh_ref, gate_ref, pg_ref,
                wg_ref, wv_ref, cw_ref, wo_ref, o_ref, xh_ref, acc_ref, g0_ref, g1_ref, v0_ref, v1_ref, *, seq_len):
    tm = x_ref.shape[1]
    n_chunks, fc, _ = wo_ref.shape
    i = pl.program_id(1)
    gain = g_ref[...]
    scale = 1.0 + sc_ref[0]
    shift = sh_ref[0]

    def modulated(rows):
        return (_rms(rows, gain) * scale + shift).astype(BF16)

    def mixed(x, a1, a2):
        y = _dot(a1, w1_ref[...]) + _dot(a2, w2_ref[...])
        return x + mg_ref[0] * _rms(y, mpg_ref[...])

    def halo(p_ref, n_ref):
        return jnp.concatenate([p_ref[0].astype(F32)[8:16], n_ref[0].astype(F32)[0:8]], axis=0).astype(BF16)

    o_ref[0] = mixed(x_ref[0], a1_ref[0], a2_ref[0])
    xh_ref[0:tm] = modulated(o_ref[0])
    x_halo = mixed(jnp.concatenate([xp_ref[0], xn_ref[0]], axis=0), halo(a1p_ref, a1n_ref), halo(a2p_ref, a2n_ref))
    xh_ref[tm:tm + 16] = modulated(x_halo)

    local = lax.broadcasted_iota(jnp.int32, (tm, 1), 0)
    pos = (i * tm + local) % seq_len
    seq_first = pos == 0
    seq_last = pos == seq_len - 1
    tile_first = local == 0
    tile_last = local == tm - 1
    acc_ref[...] = jnp.zeros_like(acc_ref)
    slots = ((g0_ref, v0_ref), (g1_ref, v1_ref))

    def project(c, slot):
        gs_ref, vs_ref = slots[slot]
        cols = pl.ds(pl.multiple_of(c * fc, fc), fc)
        gs_ref[...] = _dot(xh_ref[...], wg_ref[:, cols])
        vs_ref[...] = _dot(xh_ref[0:tm], wv_ref[:, cols])

    def mix(c, slot):
        gs_ref, vs_ref = slots[slot]
        gm = gs_ref[0:tm]
        g_prev = jnp.where(tile_first, gs_ref[tm + 7:tm + 8], pltpu.roll(gm, 1, axis=0))
        g_prev = jnp.where(seq_first, 0.0, g_prev)
        g_next = jnp.where(tile_last, gs_ref[tm + 8:tm + 9], pltpu.roll(gm, tm - 1, axis=0))
        g_next = jnp.where(seq_last, 0.0, g_next)
        cw = cw_ref[c]
        y = g_prev * cw[0:1] + gm * cw[1:2] + g_next * cw[2:3] + cw[3:4]
        u = 0.5 * y * (1.0 + lax.erf(y * (1.0 / math.sqrt(2.0)))) * vs_ref[...]
        acc_ref[...] += _dot(u.astype(BF16), wo_ref[c])

    project(0, 0)

    def body(j, carry):
        c = 2 * j
        project(c + 1, 1)
        mix(c, 0)
        project(c + 2, 0)
        mix(c + 1, 1)
        return carry

    lax.fori_loop(0, (n_chunks - 1) // 2, body, 0)
    if n_chunks % 2 == 0:
        project(n_chunks - 1, 1)
        mix(n_chunks - 2, 0)
        mix(n_chunks - 1, 1)
    else:
        mix(n_chunks - 1, 0)
    o_ref[0] = o_ref[0] + gate_ref[0] * _rms(acc_ref[...], pg_ref[...])


def _ffn(a1, a2, w1, w2, mix_gate, mix_post_g, x, pre_g, sc, sh, gate, post_g, wg, wv, cwb, wo, seq_len,
         tm=1024, name="ffn"):
    bsz, n, d = x.shape
    nf, fc, _ = wo.shape
    f = nf * fc
    tm = _tile(n, tm)
    nb8, nb16 = n // 8, n // 16
    vec_spec = pl.BlockSpec((1, d), lambda b, i: (0, 0))

    def with_halo(a):
        k = a.shape[2]
        return [pl.BlockSpec((1, tm, k), lambda b, i: (b, i, 0)),
                pl.BlockSpec((1, 16, k), lambda b, i: (b, jnp.maximum(i * (tm // 16) - 1, 0), 0)),
                pl.BlockSpec((1, 16, k), lambda b, i: (b, jnp.minimum((i + 1) * (tm // 16), nb16 - 1), 0))]

    return pl.pallas_call(
        functools.partial(_ffn_kernel, seq_len=seq_len),
        grid=(bsz, n // tm),
        in_specs=with_halo(a1) + with_halo(a2) + [
            pl.BlockSpec(w1.shape, lambda b, i: (0, 0), pipeline_mode=pl.Buffered(1)),
            pl.BlockSpec(w2.shape, lambda b, i: (0, 0), pipeline_mode=pl.Buffered(1)),
            _mod_spec(mix_gate), vec_spec,
            pl.BlockSpec((1, tm, d), lambda b, i: (b, i, 0)),
            pl.BlockSpec((1, 8, d), lambda b, i: (b, jnp.maximum(i * (tm // 8) - 1, 0), 0)),
            pl.BlockSpec((1, 8, d), lambda b, i: (b, jnp.minimum((i + 1) * (tm // 8), nb8 - 1), 0)),
            vec_spec, _mod_spec(sc), _mod_spec(sh), _mod_spec(gate), vec_spec,
            pl.BlockSpec((d, f), lambda b, i: (0, 0), pipeline_mode=pl.Buffered(1)),
            pl.BlockSpec((d, f), lambda b, i: (0, 0), pipeline_mode=pl.Buffered(1)),
            pl.BlockSpec((nf, 8, fc), lambda b, i: (0, 0, 0), pipeline_mode=pl.Buffered(1)),
            pl.BlockSpec((nf, fc, d), lambda b, i: (0, 0, 0), pipeline_mode=pl.Buffered(1)),
        ],
        out_specs=pl.BlockSpec((1, tm, d), lambda b, i: (b, i, 0)),
        out_shape=jax.ShapeDtypeStruct((bsz, n, d), F32),
        scratch_shapes=[pltpu.VMEM((tm + 16, d), BF16), pltpu.VMEM((tm, d), F32),
                        pltpu.VMEM((tm + 16, fc), F32), pltpu.VMEM((tm + 16, fc), F32),
                        pltpu.VMEM((tm, fc), F32), pltpu.VMEM((tm, fc), F32)],
        compiler_params=_params("parallel", "parallel"),
        name=name,
    )(a1, a1, a1, a2, a2, a2, w1, w2, mix_gate.table, mix_post_g.reshape(1, d),
      x, x, x, pre_g.reshape(1, d), sc.table, sh.table, gate.table, post_g.reshape(1, d), wg, wv, cwb, wo)


def _mla_kernel(*refs, n_ctx, n_lat):
    if n_lat:
        (q_ref, cq_ref, sq_ref, kvc_ref, tc_ref, kvl_ref, tl_ref, ck_ref, sk_ref,
         o_ref, kc_ref, vt_ref, ot_ref, s0_ref, s1_ref, p0_ref, p1_ref, qh_ref, oe_ref) = refs
    else:
        (q_ref, cq_ref, sq_ref, kvc_ref, tc_ref,
         o_ref, kc_ref, vt_ref, ot_ref, s0_ref, s1_ref, p0_ref, p1_ref, qh_ref, oe_ref) = refs
    hv = MLA_HEADS * MLA_NOPE
    _, n_blocks, vrows, rb = vt_ref.shape

    @pl.when(pl.program_id(1) == 0)
    def _build_keys():
        r = lax.broadcasted_iota(jnp.int32, (vrows, MLA_V), 0)
        c = lax.broadcasted_iota(jnp.int32, (vrows, MLA_V), 1)
        sel = (r == c).astype(BF16)

        def fill(kv_ref, t_ref, off, n, roped):
            tail = t_ref[0].astype(F32)
            kr = tail[:, T_KR:T_KR + MLA_ROPE]
            if roped:
                kr = kr * ck_ref[...] + tail[:, T_KRROT:T_KRROT + MLA_ROPE] * sk_ref[...]
            kr = kr.astype(BF16)
            is_v = lax.broadcasted_iota(jnp.int32, (vrows, rb), 0) < MLA_V
            for h in range(MLA_HEADS):
                kc_ref[h, off:off + n, 0:MLA_NOPE] = kv_ref[0, :, h * MLA_NOPE:(h + 1) * MLA_NOPE]
                kc_ref[h, off:off + n, MLA_NOPE:MLA_NOPE + MLA_ROPE] = kr
                kc_ref[h, off:off + n, MLA_NOPE + MLA_ROPE:128] = jnp.zeros((n, 128 - MLA_NOPE - MLA_ROPE), BF16)
                vt = lax.dot_general(sel, kv_ref[0, :, hv + h * MLA_V:hv + (h + 1) * MLA_V], NT_DIMS,
                                     preferred_element_type=F32)
                for jb in range(n // rb):
                    vt_ref[h, off // rb + jb] = jnp.where(is_v, vt[:, jb * rb:(jb + 1) * rb], 1.0).astype(BF16)

        fill(kvc_ref, tc_ref, 0, n_ctx, False)
        if n_lat:
            fill(kvl_ref, tl_ref, n_ctx, n_lat, True)

    tq = q_ref.shape[1]
    hr = MLA_HEADS * MLA_ROPE
    qr = (q_ref[0, :, hv:hv + hr].astype(F32) * cq_ref[...]
          + q_ref[0, :, hv + hr:hv + 2 * hr].astype(F32) * sq_ref[...]).astype(BF16)
    pad = jnp.zeros((tq, 128 - MLA_NOPE - MLA_ROPE), BF16)
    for h in range(MLA_HEADS):
        qh_ref[h] = jnp.concatenate([q_ref[0, :, h * MLA_NOPE:(h + 1) * MLA_NOPE],
                                     qr[:, h * MLA_ROPE:(h + 1) * MLA_ROPE], pad], axis=1)

    s_refs = (s0_ref, s1_ref)
    p_refs = (p0_ref, p1_ref)
    maxes = [None] * MLA_HEADS
    for t in range(MLA_HEADS + 2):
        h_s, h_e, h_v = t, t - 1, t - 2
        do_s, do_e, do_v = h_s < MLA_HEADS, 0 <= h_e < MLA_HEADS, 0 <= h_v
        if do_v:
            oe_ref[...] = jnp.zeros_like(oe_ref)

        def block(j, m, h_s=h_s, h_e=h_e, h_v=h_v, do_s=do_s, do_e=do_e, do_v=do_v):
            rows = pl.ds(pl.multiple_of(j * rb, rb), rb)
            if do_s:
                blk = lax.dot_general(kc_ref[h_s, rows, :], qh_ref[h_s], NT_DIMS, preferred_element_type=F32)
                s_refs[h_s % 2][rows, :] = blk
                m = jnp.maximum(m, jnp.max(blk, axis=0, keepdims=True))
            if do_e:
                p_refs[h_e % 2][rows, :] = jnp.exp2(s_refs[h_e % 2][rows, :] - maxes[h_e]).astype(BF16)
            if do_v:
                oe_ref[...] += _dot(vt_ref[h_v, j], p_refs[h_v % 2][rows, :])
            return m

        m = lax.fori_loop(0, n_blocks, block, jnp.full((1, tq), -jnp.inf, F32), unroll=True)
        if do_s:
            maxes[h_s] = m
        if do_v:
            ot_ref[h_v * MLA_V:(h_v + 1) * MLA_V, :] = oe_ref[0:MLA_V] / oe_ref[MLA_V:MLA_V + 1]
    o_ref[0] = ot_ref[...].T.astype(o_ref.dtype)


def _mla(q, cosq, sinq, kv_c, p_c, kv_l=None, p_l=None, cosk=None, sink=None, tq=512, name="mla"):
    bsz, nq, qw = q.shape
    n_ctx = kv_c.shape[1]
    n_lat = 0 if kv_l is None else kv_l.shape[1]
    nk = n_ctx + n_lat
    tq = _tile(nq, tq)
    rb = math.gcd(math.gcd(n_ctx, n_lat), 256)
    vrows = MLA_V + 16
    hr = MLA_HEADS * MLA_ROPE
    tail_blk = P_TAIL // 128
    in_specs = [
        pl.BlockSpec((1, tq, qw), lambda b, i: (b, i, 0)),
        pl.BlockSpec((tq, hr), lambda b, i: (i, 0)),
        pl.BlockSpec((tq, hr), lambda b, i: (i, 0)),
        pl.BlockSpec((1, n_ctx, kv_c.shape[2]), lambda b, i: (b, 0, 0)),
        pl.BlockSpec((1, n_ctx, 128), lambda b, i: (b, 0, tail_blk)),
    ]
    args = [q, cosq, sinq, kv_c, p_c]
    if n_lat:
        in_specs += [
            pl.BlockSpec((1, n_lat, kv_l.shape[2]), lambda b, i: (b, 0, 0)),
            pl.BlockSpec((1, n_lat, 128), lambda b, i: (b, 0, tail_blk)),
            pl.BlockSpec((n_lat, MLA_ROPE), lambda b, i: (0, 0)),
            pl.BlockSpec((n_lat, MLA_ROPE), lambda b, i: (0, 0)),
        ]
        args += [kv_l, p_l, cosk, sink]
    return pl.pallas_call(
        functools.partial(_mla_kernel, n_ctx=n_ctx, n_lat=n_lat),
        grid=(bsz, nq // tq),
        in_specs=in_specs,
        out_specs=pl.BlockSpec((1, tq, MLA_HEADS * MLA_V), lambda b, i: (b, i, 0)),
        out_shape=jax.ShapeDtypeStruct((bsz, nq, MLA_HEADS * MLA_V), BF16),
        scratch_shapes=[pltpu.VMEM((MLA_HEADS, nk, 128), BF16),
                        pltpu.VMEM((MLA_HEADS, nk // rb, vrows, rb), BF16),
                        pltpu.VMEM((MLA_HEADS * MLA_V, tq), F32),
                        pltpu.VMEM((nk, tq), F32), pltpu.VMEM((nk, tq), F32),
                        pltpu.VMEM((nk, tq), BF16), pltpu.VMEM((nk, tq), BF16),
                        pltpu.VMEM((MLA_HEADS, tq, 128), BF16),
                        pltpu.VMEM((vrows, tq), F32)],
        compiler_params=_params("parallel", "arbitrary"),
        name=name,
    )(*args)


def _gla_kernel(*refs, want_ctx):
    (qc_ref, kc_ref, tc_ref, vc_ref, grc_ref, ql_ref, kl_ref, tl_ref, vl_ref, grl_ref,
     wg_ref, bg_ref, on_ref) = refs[:13]
    if want_ctx:
        oc_ref, ol_ref = refs[13:15]
        scratch = refs[15:]
    else:
        oc_ref, ol_ref = None, refs[13]
        scratch = refs[14:]
    lgc_ref, lgl_ref, accc_ref, accl_ref, st_ref = scratch
    cs = GLA_CHUNK
    hk = GLA_HEADS * GLA_DK

    shift = cs.bit_length() - 1
    for t_ref, lg_ref in ((tc_ref, lgc_ref), (tl_ref, lgl_ref)):
        n = t_ref.shape[1]
        tr = _tile(n, 256)
        r = lax.broadcasted_iota(jnp.int32, (tr, tr), 0)
        c = lax.broadcasted_iota(jnp.int32, (tr, tr), 1)
        same_chunk = (r >> shift) == (c >> shift)
        tri_fw = (same_chunk & (r >= c)).astype(BF16)
        tri_bw = (same_chunk & (r <= c)).astype(BF16)

        def cum_body(i, carry, t_ref=t_ref, lg_ref=lg_ref, tr=tr, tri_fw=tri_fw, tri_bw=tri_bw):
            rows = pl.ds(pl.multiple_of(i * tr, tr), tr)
            z = _dot(t_ref[0, rows, :], wg_ref[...]) + bg_ref[...]
            lg = (jnp.minimum(z, 0.0) - jnp.log(1.0 + jnp.exp(-jnp.abs(z)))) * (1.0 / GLA_GATE_NORM)
            hi = lg.astype(BF16)
            rest = lg - hi.astype(F32)
            mid = rest.astype(BF16)
            lo = (rest - mid.astype(F32)).astype(BF16)
            for tri, cols in ((tri_fw, slice(0, hk)), (tri_bw, slice(hk, 2 * hk))):
                lg_ref[rows, cols] = (_dot(tri, lo[:, cols]) + _dot(tri, mid[:, cols])) + _dot(tri, hi[:, cols])
            return carry

        lax.fori_loop(0, n // tr, cum_body, 0, unroll=2)

    br = math.gcd(math.gcd(qc_ref.shape[1], ql_ref.shape[1]), 256)
    cpb = br // cs
    row = lax.broadcasted_iota(jnp.int32, (br, br), 0)
    col = lax.broadcasted_iota(jnp.int32, (br, br), 1)
    same = (row >> shift) == (col >> shift)
    causal = (same & (row >= col), same & (row <= col))
    erow = lax.broadcasted_iota(jnp.int32, (cs, cs), 0)
    ecol = lax.broadcasted_iota(jnp.int32, (cs, cs), 1)
    eye = (erow == ecol).astype(F32)
    st_ref[...] = jnp.zeros_like(st_ref)

    def block_diag(x):
        wide = jnp.concatenate([x] * cpb, axis=1)
        return jnp.where(same, wide, jnp.zeros_like(wide))

    def block(q_ref, k_ref, v_ref, lg_ref, acc_ref, d, r0):
        rows = pl.ds(pl.multiple_of(r0, br), br)
        cum = lg_ref[rows, d * hk:(d + 1) * hk]
        cum3 = cum.reshape(cpb, cs, hk)
        edge = cum3[:, cs - 1:cs, :] if d == 0 else cum3[:, 0:1, :]
        tot = jnp.broadcast_to(edge, cum3.shape).reshape(br, hk)
        half = 0.5 * tot
        q = q_ref[0, rows, :].astype(F32)
        k = k_ref[0, rows, :].astype(F32)
        v = v_ref[0, rows, :]
        qt = (q * jnp.exp(cum - half)).astype(BF16)
        kt = (k * jnp.exp(half - cum)).astype(BF16)
        qs = (q * jnp.exp(cum)).astype(BF16)
        kd = (k * jnp.exp(tot - cum)).astype(BF16)
        order = range(cpb) if d == 0 else range(cpb - 1, -1, -1)
        for h in range(GLA_HEADS):
            ks = slice(h * GLA_DK, (h + 1) * GLA_DK)
            vs = slice(h * GLA_DV, (h + 1) * GLA_DV)
            a = lax.dot_general(qt[:, ks], kt[:, ks], NT_DIMS, preferred_element_type=F32)
            o = _dot(jnp.where(causal[d], a, 0.0).astype(BF16), v[:, vs])
            u = lax.dot_general(block_diag(kd[:, ks]), v[:, vs], TN_DIMS, preferred_element_type=F32)
            st = st_ref[d, h]
            starts = [None] * cpb
            for c in order:
                starts[c] = st.astype(BF16)
                dec = jnp.exp(jnp.sum(eye * edge[c, :, ks], axis=1, keepdims=True))
                st = st * dec + u[c * cs:(c + 1) * cs]
            st_ref[d, h] = st
            acc_ref[d, rows, vs] = o + _dot(block_diag(qs[:, ks]), jnp.concatenate(starts, axis=0))

    def scan(q_ref, k_ref, v_ref, lg_ref, acc_ref):
        n_blocks = q_ref.shape[1] // br

        def body(i, carry):
            block(q_ref, k_ref, v_ref, lg_ref, acc_ref, 0, i * br)
            block(q_ref, k_ref, v_ref, lg_ref, acc_ref, 1, (n_blocks - 1 - i) * br)
            return carry

        lax.fori_loop(0, n_blocks, body, 0, unroll=2)

    def finish(acc_ref, gr_ref, o_ref):
        n = acc_ref.shape[1]
        tr = _tile(n, 256)

        def body(i, carry):
            rows = pl.ds(pl.multiple_of(i * tr, tr), tr)
            gr = gr_ref[0, rows, :].astype(F32)
            gate = gr * _sigmoid(gr)
            for h in range(GLA_HEADS):
                vs = slice(h * GLA_DV, (h + 1) * GLA_DV)
                o = acc_ref[0, rows, vs] + acc_ref[1, rows, vs]
                o_ref[0, rows, vs] = (_rms(o, on_ref[...]) * gate[:, vs]).astype(o_ref.dtype)
            return carry

        lax.fori_loop(0, n // tr, body, 0)

    scan(qc_ref, kc_ref, vc_ref, lgc_ref, accc_ref)
    scan(ql_ref, kl_ref, vl_ref, lgl_ref, accl_ref)
    if want_ctx:
        finish(accc_ref, grc_ref, oc_ref)
    finish(accl_ref, grl_ref, ol_ref)


def _gla(p_c, p_l, w_gate, b_gate, o_norm, want_ctx, name="gla"):
    bsz, n_ctx, _ = p_c.shape
    n_lat = p_l.shape[1]
    hk, hv = GLA_HEADS * GLA_DK, GLA_HEADS * GLA_DV

    def specs(n):
        return [
            pl.BlockSpec((1, n, hk), lambda b: (b, 0, P_GQ // hk)),
            pl.BlockSpec((1, n, hk), lambda b: (b, 0, P_GK // hk)),
            pl.BlockSpec((1, n, 128), lambda b: (b, 0, P_TAIL // 128)),
            pl.BlockSpec((1, n, hv), lambda b: (b, 0, P_GV // hv)),
            pl.BlockSpec((1, n, hv), lambda b: (b, 0, P_GR // hv)),
        ]

    in_specs = specs(n_ctx) + specs(n_lat) + [
        pl.BlockSpec((128, 2 * hk), lambda b: (0, 0)),
        pl.BlockSpec((1, 2 * hk), lambda b: (0, 0)),
        pl.BlockSpec((1, GLA_DV), lambda b: (0, 0)),
    ]
    out_specs = [pl.BlockSpec((1, n_lat, hv), lambda b: (b, 0, 0))]
    out_shape = [jax.ShapeDtypeStruct((bsz, n_lat, hv), BF16)]
    if want_ctx:
        out_specs.insert(0, pl.BlockSpec((1, n_ctx, hv), lambda b: (b, 0, 0)))
        out_shape.insert(0, jax.ShapeDtypeStruct((bsz, n_ctx, hv), BF16))
    outs = pl.pallas_call(
        functools.partial(_gla_kernel, want_ctx=want_ctx),
        grid=(bsz,),
        in_specs=in_specs,
        out_specs=out_specs,
        out_shape=out_shape,
        scratch_shapes=[
            pltpu.VMEM((n_ctx, 2 * hk), F32), pltpu.VMEM((n_lat, 2 * hk), F32),
            pltpu.VMEM((2, n_ctx, hv), F32), pltpu.VMEM((2, n_lat, hv), F32),
            pltpu.VMEM((2, GLA_HEADS, GLA_DK, GLA_DV), F32),
        ],
        compiler_params=_params("parallel"),
        name=name,
    )(p_c, p_c, p_c, p_c, p_c, p_l, p_l, p_l, p_l, p_l, w_gate, b_gate, o_norm.reshape(1, GLA_DV))
    return (outs[0], outs[1]) if want_ctx else (None, outs[0])


def _conf_kernel(a_ref, gt_ref, w_ref, b_ref, lg_ref, lb_ref, o_ref, u_ref):
    n = a_ref.shape[1]
    pad = (CONV_W - 1) // 2
    lead = 16
    u_ref[0:lead] = jnp.zeros((lead, CONV_CH), F32)
    u_ref[lead + n:lead + n + lead] = jnp.zeros((lead, CONV_CH), F32)
    u_ref[lead:lead + n] = a_ref[0].astype(F32) * _sigmoid(gt_ref[0].astype(F32))
    tr = _tile(n, 256)

    def body(i, carry):
        r0 = pl.multiple_of(i * tr, tr)
        ext = u_ref[pl.ds(r0, tr + 2 * lead), :]
        y = jnp.zeros((tr, CONV_CH), F32) + b_ref[...]
        for k in range(CONV_W):
            off = lead - pad + k
            y = y + pltpu.roll(ext, tr + 2 * lead - off, axis=0)[0:tr] * w_ref[k:k + 1, :]
        mu = jnp.mean(y, axis=-1, keepdims=True)
        yc = y - mu
        z = yc * lax.rsqrt(jnp.mean(yc * yc, axis=-1, keepdims=True) + EPS) * lg_ref[...] + lb_ref[...]
        o_ref[0, pl.ds(r0, tr), :] = (z * _sigmoid(z)).astype(o_ref.dtype)
        return carry

    lax.fori_loop(0, n // tr, body, 0)


def _conformer(p, w, b, ln_g, ln_b, name="conformer"):
    bsz, n, _ = p.shape
    vec = pl.BlockSpec((1, CONV_CH), lambda i: (0, 0))
    return pl.pallas_call(
        _conf_kernel,
        grid=(bsz,),
        in_specs=[
            pl.BlockSpec((1, n, CONV_CH), lambda i: (i, 0, 0)),
            pl.BlockSpec((1, n, CONV_CH), lambda i: (i, 0, 1)),
            pl.BlockSpec((CONV_W + 1, CONV_CH), lambda i: (0, 0)),
            vec, vec, vec,
        ],
        out_specs=pl.BlockSpec((1, n, CONV_CH), lambda i: (i, 0, 0)),
        out_shape=jax.ShapeDtypeStruct((bsz, n, CONV_CH), BF16),
        scratch_shapes=[pltpu.VMEM((n + 32, CONV_CH), F32)],
        compiler_params=_params("parallel"),
        name=name,
    )(p, p, w, b.reshape(1, CONV_CH), ln_g.reshape(1, CONV_CH), ln_b.reshape(1, CONV_CH))


def _fnet_kernel(fe_ref, fo_ref, cs_ref, we_ref, wo_ref, o_ref, z_ref, *, scale):
    h = fe_ref.shape[1]
    b = pl.program_id(1)

    @pl.when(pl.program_id(0) == 0)
    def _channel_dft():
        for k, f_ref in enumerate((fe_ref, fo_ref)):
            zz = _dot(f_ref[0], cs_ref[...])
            z_ref[b, k, 0:h] = zz[:, 0:FNET_CH].astype(BF16)
            z_ref[b, k, h:2 * h] = zz[:, FNET_CH:2 * FNET_CH].astype(BF16)

    ge = _dot(we_ref[...], z_ref[b, 0])
    go = _dot(wo_ref[...], z_ref[b, 1])
    o_ref[0, 0] = ((ge + go) * scale).astype(o_ref.dtype)
    o_ref[0, 1] = ((ge - go) * scale).astype(o_ref.dtype)


def _dft_tables(n):
    k = np.arange(n, dtype=np.float64)
    ang = 2.0 * np.pi * np.outer(k, k) / n
    return np.cos(ang), np.sin(ang)


def _fnet(p, name="fnet"):
    bsz, n, width = p.shape
    h = n // 2
    j = np.arange(h, dtype=np.float64)
    tables = []
    for parity in (0, 1):
        ang = 2.0 * np.pi * np.outer(j, 2.0 * j + parity) / n
        tables.append(jnp.asarray(np.concatenate([np.cos(ang), -np.sin(ang)], axis=1), BF16))
    cg, sg = _dft_tables(FNET_GROUP_CH)
    eye = np.eye(FNET_GROUPS)
    cs = jnp.asarray(np.concatenate([np.kron(eye, cg), np.kron(eye, sg)], axis=1), BF16)
    f = p[:, :, width - FNET_CH:]
    tm = _tile(h, 512)
    scale = 1.0 / math.sqrt(n * FNET_GROUP_CH)
    half_spec = pl.BlockSpec((1, h, FNET_CH), lambda i, b: (b, 0, 0))
    out = pl.pallas_call(
        functools.partial(_fnet_kernel, scale=scale),
        grid=(h // tm, bsz),
        in_specs=[
            half_spec, half_spec,
            pl.BlockSpec((FNET_CH, 2 * FNET_CH), lambda i, b: (0, 0)),
            pl.BlockSpec((tm, n), lambda i, b: (i, 0)),
            pl.BlockSpec((tm, n), lambda i, b: (i, 0)),
        ],
        out_specs=pl.BlockSpec((1, 2, tm, FNET_CH), lambda i, b: (b, 0, i, 0)),
        out_shape=jax.ShapeDtypeStruct((bsz, 2, h, FNET_CH), BF16),
        scratch_shapes=[pltpu.VMEM((bsz, 2, n, FNET_CH), BF16)],
        compiler_params=_params("arbitrary", "arbitrary"),
        name=name,
    )(f[:, 0::2], f[:, 1::2], cs, *tables)
    return out.reshape(bsz, n, FNET_CH)


def _rot_cols(w):
    a, b, c, d = jnp.split(w, 4, axis=-1)
    return jnp.concatenate([-b, a, -d, c], axis=-1)


def _even_weights(w_in, w_uq, w_ukv, w_gfw, b_gfw, w_gbw, b_gbw):
    d = w_in.shape[0]
    sizes = [MLA_Q_RANK, MLA_KV_RANK, MLA_ROPE, GLA_HEADS * GLA_DK, GLA_HEADS * GLA_DK, GLA_HEADS * GLA_DV,
             GLA_GATE_RANK, GLA_GATE_RANK, GLA_HEADS * GLA_DV]
    qc, kvc, kr, gq, gk, gv, glf, glb, gr = jnp.split(w_in, np.cumsum(sizes)[:-1].tolist(), axis=1)
    tail_pad = jnp.zeros((d, 128 - 2 * MLA_ROPE - 2 * GLA_GATE_RANK), w_in.dtype)
    w_p = jnp.concatenate([qc, gq * (GLA_DK ** -0.5), gk, kvc, kr, _rot_cols(kr), glf, glb, tail_pad, gv, gr],
                          axis=1).astype(BF16)
    uq = w_uq.reshape(MLA_Q_RANK, MLA_HEADS, MLA_NOPE + MLA_ROPE) * (MLA_SCALE * math.log2(math.e))
    qn = uq[:, :, :MLA_NOPE].reshape(MLA_Q_RANK, -1)
    qr = uq[:, :, MLA_NOPE:]
    w_q = jnp.concatenate([qn, qr.reshape(MLA_Q_RANK, -1), _rot_cols(qr).reshape(MLA_Q_RANK, -1)],
                          axis=1).astype(BF16)
    ukv = w_ukv.reshape(MLA_KV_RANK, MLA_HEADS, MLA_NOPE + MLA_V)
    w_kv = jnp.concatenate([ukv[:, :, :MLA_NOPE].reshape(MLA_KV_RANK, -1),
                            ukv[:, :, MLA_NOPE:].reshape(MLA_KV_RANK, -1)], axis=1).astype(BF16)
    hk = GLA_HEADS * GLA_DK
    w_gate = jnp.zeros((128, 2 * hk), F32)
    w_gate = w_gate.at[T_GF:T_GF + GLA_GATE_RANK, :hk].set(w_gfw)
    w_gate = w_gate.at[T_GB:T_GB + GLA_GATE_RANK, hk:].set(w_gbw).astype(BF16)
    b_gate = jnp.concatenate([b_gfw, b_gbw]).reshape(1, 2 * hk)
    return w_p, w_q, w_kv, w_gate, b_gate


def _ffn_weights(w_in, dw_w, dw_b, w_out):
    d, f2 = w_in.shape
    f = f2 // 2
    nf = f // FFN_CHUNK
    wg = w_in[:, :f].astype(BF16)
    wv = w_in[:, f:].astype(BF16)
    taps = jnp.concatenate([dw_w, dw_b[None], jnp.zeros((8 - dw_w.shape[0] - 1, f), F32)], axis=0)
    cwb = taps.reshape(8, nf, FFN_CHUNK).transpose(1, 0, 2)
    wo = w_out.reshape(nf, FFN_CHUNK, d).astype(BF16)
    return wg, wv, cwb, wo


def _rope_tables(n):
    rows = n // GRID_W
    row = jnp.repeat(jnp.arange(rows), GRID_W).astype(F32)
    col = jnp.tile(jnp.arange(GRID_W), rows).astype(F32)
    half = MLA_ROPE // 2
    inv = ROPE_BASE ** (-jnp.arange(0, half, 2, dtype=F32) / half)
    ar = row[:, None] * inv
    ac = col[:, None] * inv
    ang = jnp.concatenate([ar, ar, ac, ac], axis=-1)
    return jnp.cos(ang), jnp.sin(ang)


def kernel(x, c, ctx, c_ctx, mod_w, mod_b, pre_mix_g, post_mix_g, pre_ffn_g, post_ffn_g, ev_in_w, mla_q_norm, mla_kv_norm, mla_w_uq, mla_w_ukv, gla_w_gate_fw, gla_b_gate_fw, gla_w_gate_bw, gla_b_gate_bw, gla_o_norm, ev_out_w, od_in_w, conf_dw_w, conf_dw_b, conf_ln_g, conf_ln_b, od_out_w, ffn_in_w, ffn_dw_w, ffn_dw_b, ffn_out_w):
    bsz, n, d = x.shape
    n_ctx = ctx.shape[1]
    depth = mod_w.shape[0]
    last_ctx_reader = ((depth - 1) // 2) * 2

    rows = -(-(bsz + 1) // 8) * 8
    cvec = jnp.concatenate([c, c_ctx[None], jnp.zeros((rows - bsz - 1, d), F32)], axis=0)
    mod = _modulation(cvec, mod_w, mod_b)
    table = mod.reshape(depth * rows * 6, 1, d)

    cos, sin = _rope_tables(n)
    cosq, sinq = jnp.tile(cos, (1, MLA_HEADS)), jnp.tile(sin, (1, MLA_HEADS))
    ones_q = jnp.ones((n_ctx, MLA_HEADS * MLA_ROPE), F32)
    zeros_q = jnp.zeros((n_ctx, MLA_HEADS * MLA_ROPE), F32)

    def flat(t):
        return t.reshape(1, bsz * n_ctx, t.shape[-1])

    def unflat(t):
        return t.reshape(bsz, n_ctx, t.shape[-1])

    x_lat, x_ctx = x, ctx
    for l in range(depth):
        need_ctx = l < last_ctx_reader
        use_ctx = need_ctx or (l % 2 == 0)
        i = l // 2
        sh1, sc1, g1, sh2, sc2, g2 = (_Mod(table, l * rows * 6 + k, 6) for k in range(6))
        csh1, csc1, cg1, csh2, csc2, cg2 = (_Mod(table, (l * rows + bsz) * 6 + k, 0) for k in range(6))

        if l % 2 == 0:
            w_p, w_q, w_kv, w_gate, b_gate = _even_weights(
                ev_in_w[i], mla_w_uq[i], mla_w_ukv[i], gla_w_gate_fw[i], gla_b_gate_fw[i],
                gla_w_gate_bw[i], gla_b_gate_bw[i])
            w_o = ev_out_w[i].astype(BF16)
            k1 = MLA_HEADS * MLA_V
            p_lat, q_lat, kv_lat = _even_in(x_lat, pre_mix_g[l], sc1, sh1, w_p, mla_q_norm[i], w_q,
                                            mla_kv_norm[i], w_kv, name="even_in_lat")
            p_ctx, q_ctx, kv_ctx = (unflat(t) for t in _even_in(
                flat(x_ctx), pre_mix_g[l], csc1, csh1, w_p, mla_q_norm[i], w_q, mla_kv_norm[i], w_kv,
                name="even_in_ctx"))
            a_lat = _mla(q_lat, cosq, sinq, kv_ctx, p_ctx, kv_lat, p_lat, cos, sin, name="mla_lat")
            g_ctx, g_lat = _gla(p_ctx, p_lat, w_gate, b_gate, gla_o_norm[i], need_ctx)
            mixed_lat = (a_lat, g_lat)
            if need_ctx:
                a_ctx = _mla(q_ctx, ones_q, zeros_q, kv_ctx, p_ctx, name="mla_ctx")
                mixed_ctx = (flat(a_ctx), flat(g_ctx))
        else:
            w_p = od_in_w[i].astype(BF16)
            w_o = od_out_w[i].astype(BF16)
            w_dw = jnp.concatenate([conf_dw_w[i], jnp.zeros((1, CONV_CH), F32)], axis=0)
            k1 = CONV_CH
            p_lat = _nmm(x_lat, pre_mix_g[l], w_p, sc1, sh1, name="odd_in_lat")
            mixed_lat = (_conformer(p_lat, w_dw, conf_dw_b[i], conf_ln_g[i], conf_ln_b[i], name="conformer_lat"),
                         _fnet(p_lat, name="fnet_lat"))
            if need_ctx:
                p_ctx = unflat(_nmm(flat(x_ctx), pre_mix_g[l], w_p, csc1, csh1, name="odd_in_ctx"))
                mixed_ctx = (flat(_conformer(p_ctx, w_dw, conf_dw_b[i], conf_ln_g[i], conf_ln_b[i],
                                             name="conformer_ctx")),
                             flat(_fnet(p_ctx, name="fnet_ctx")))

        wg, wv, cwb, wo = _ffn_weights(ffn_in_w[l], ffn_dw_w[l], ffn_dw_b[l], ffn_out_w[l])
        x_lat = _ffn(*mixed_lat, w_o[:k1], w_o[k1:], g1, post_mix_g[l], x_lat, pre_ffn_g[l], sc2, sh2, g2,
                     post_ffn_g[l], wg, wv, cwb, wo, n, name="ffn_lat")
        if need_ctx:
            x_ctx = unflat(_ffn(*mixed_ctx, w_o[:k1], w_o[k1:], cg1, post_mix_g[l], flat(x_ctx), pre_ffn_g[l],
                                csc2, csh2, cg2, post_ffn_g[l], wg, wv, cwb, wo, n_ctx, name="ffn_ctx"))
    return x_lat
```
